```python
import math
import jax, jax.numpy as jnp
from jax import lax
import numpy as np

D_MODEL = 1024
BATCH = 8
SEQ = 8192
DEPTH = 1

ATT_HEADS = 8
ATT_KV_HEADS = 2
ATT_HEAD_DIM = 64
WINDOW = 128
ATT_BLOCK = 128
ML_HEADS = 4
ML_HEAD_DIM = 128
ML_CHUNK = 128
CONV_WIDTH = 3
ATT_WIDTH = ATT_HEADS * ATT_HEAD_DIM
KV_WIDTH = ATT_KV_HEADS * ATT_HEAD_DIM
ML_WIDTH = ML_HEADS * ML_HEAD_DIM
MIX_WIDTH = ATT_WIDTH + ML_WIDTH
N_GATES = 4 * ML_HEADS
IN_SECTIONS = (ATT_WIDTH, KV_WIDTH, KV_WIDTH, ML_WIDTH, ML_WIDTH, ML_WIDTH, ML_WIDTH, N_GATES)
IN_WIDTH = sum(IN_SECTIONS)
MEM_TOKENS = 256
MEM_HEADS = 4
MEM_HEAD_DIM = 128
MEM_WIDTH = MEM_HEADS * MEM_HEAD_DIM
N_EXPERTS = 16
EXPERT_FF = 1024
CAPACITY_FACTOR = 2
EPS = 1e-6

kernel_name = "hymba_swa_mlstm_ec_moe_encoder"


def rmsnorm(x, g):
    xf = x.astype(jnp.float32)
    y = xf * lax.rsqrt(jnp.mean(xf * xf, axis=-1, keepdims=True) + EPS)
    return (y * g.astype(jnp.float32)).astype(x.dtype)


def alibi_slopes(n):
    return jnp.exp2(-8.0 * jnp.arange(1, n + 1, dtype=jnp.float32) / n)


def windowed_sink_attention(q, k, v, sink):
    B, S, H, Dh = q.shape
    KV = k.shape[2]
    G = H // KV
    NB = S // ATT_BLOCK
    qb = q.reshape(B, NB, ATT_BLOCK, KV, G, Dh)

    def band(t):
        tp = jnp.pad(t, ((0, 0), (ATT_BLOCK, ATT_BLOCK), (0, 0), (0, 0)))
        tp = tp.reshape(B, NB + 2, ATT_BLOCK, KV, Dh)
        return jnp.concatenate([tp[:, :-2], tp[:, 1:-1], tp[:, 2:]], axis=2)

    kb, vb = band(k), band(v)
    scores = jnp.einsum('bnqkgd,bnskd->bnkgqs', qb, kb).astype(jnp.float32) * (Dh ** -0.5)
    qi = jnp.arange(ATT_BLOCK)[:, None]
    si = jnp.arange(3 * ATT_BLOCK)[None, :]
    dist = qi + ATT_BLOCK - si
    key_pos = jnp.arange(NB)[:, None, None] * ATT_BLOCK - ATT_BLOCK + si[None]
    valid = (jnp.abs(dist) <= WINDOW)[None] & (key_pos >= 0) & (key_pos < S)
    slopes = alibi_slopes(H).reshape(KV, G)
    bias = -slopes[:, :, None, None] * jnp.abs(dist).astype(jnp.float32)
    scores = jnp.where(valid[None, :, None, None], scores + bias[None, None], -jnp.inf)
    sink_l = sink.astype(jnp.float32).reshape(KV, G)[None, None, :, :, None, None]
    m = jnp.maximum(jnp.max(scores, axis=-1, keepdims=True), sink_l)
    p = jnp.exp(scores - m)
    denom = jnp.sum(p, axis=-1, keepdims=True) + jnp.exp(sink_l - m)
    out = jnp.einsum('bnkgqs,bnskd->bnqkgd', (p / denom).astype(v.dtype), vb)
    return out.reshape(B, S, H * Dh)


def centred_depthwise_conv(x, w, b):
    C = x.shape[-1]
    pad = (CONV_WIDTH - 1) // 2
    y = lax.conv_general_dilated(x, w[:, None, :].astype(x.dtype), window_strides=(1,),
                                 padding=[(pad, pad)], dimension_numbers=('NWC', 'WIO', 'NWC'),
                                 feature_group_count=C)
    return y + b.astype(x.dtype)


def mlstm_chunkwise(q, k, v, log_i, log_f):
    B, S, H, Dh = q.shape
    L = ML_CHUNK
    NC = S // L
    f32 = jnp.float32

    def chunks(t):
        t = t.astype(f32).reshape((B, NC, L, H) + t.shape[3:])
        return jnp.moveaxis(t, 3, 2)

    qc = chunks(q)
    kc = chunks(k) * (Dh ** -0.5)
    vc = chunks(v)
    ic = chunks(log_i)
    F = jnp.cumsum(chunks(log_f), axis=-1)
    F_end = F[..., -1]
    w_end = F_end[..., None] - F + ic
    m_loc = jnp.max(w_end, axis=-1)
    e_end = jnp.exp(w_end - m_loc[..., None])
    C_loc = jnp.einsum('bchs,bchsk,bchsv->bchkv', e_end, kc, vc)
    n_loc = jnp.einsum('bchs,bchsk->bchk', e_end, kc)

    def step(carry, inp):
        C, n, m = carry
        g, ml, Cl, nl = inp
        m_new = jnp.maximum(g + m, ml)
        a = jnp.exp(g + m - m_new)
        b = jnp.exp(ml - m_new)
        new = (a[..., None, None] * C + b[..., None, None] * Cl, a[..., None] * n + b[..., None] * nl, m_new)
        return new, (C, n, m)

    init = (jnp.zeros((B, H, Dh, Dh), f32), jnp.zeros((B, H, Dh), f32), jnp.zeros((B, H), f32))
    xs = (jnp.moveaxis(F_end, 1, 0), jnp.moveaxis(m_loc, 1, 0), jnp.moveaxis(C_loc, 1, 0), jnp.moveaxis(n_loc, 1, 0))
    _, (C0, n0, m0) = lax.scan(step, init, xs)
    C0 = jnp.moveaxis(C0, 0, 1)
    n0 = jnp.moveaxis(n0, 0, 1)
    m0 = jnp.moveaxis(m0, 0, 1)
    tri = jnp.tril(jnp.ones((L, L), dtype=bool))
    logD = jnp.where(tri, F[..., :, None] - F[..., None, :] + ic[..., None, :], -jnp.inf)
    a = F + m0[..., None]
    m_t = jnp.maximum(a, jnp.max(logD, axis=-1))
    e_inter = jnp.exp(a - m_t)
    s_qk = jnp.einsum('bchtd,bchsd->bchts', qc, kc) * jnp.exp(logD - m_t[..., None])
    num = e_inter[..., None] * jnp.einsum('bchtd,bchdv->bchtv', qc, C0) + jnp.einsum('bchts,bchsv->bchtv', s_qk, vc)
    den = e_inter * jnp.einsum('bchtd,bchd->bcht', qc, n0) + jnp.sum(s_qk, axis=-1)
    h = num / jnp.maximum(jnp.abs(den), jnp.exp(-m_t))[..., None]
    return jnp.moveaxis(h, 2, 3).reshape(B, S, H, Dh).astype(q.dtype)


def parallel_head_group_mixer(h, w_in, q_g, k_g, sink, conv_w, conv_b, gate_b, out_g, w_out):
    B, S, _ = h.shape
    splits = [int(c) for c in np.cumsum(IN_SECTIONS)[:-1]]
    aq, ak, av, mq, mk, mv, mo, gates = jnp.split(h @ w_in, splits, axis=-1)
    aq = rmsnorm(aq.reshape(B, S, ATT_HEADS, ATT_HEAD_DIM), q_g)
    ak = rmsnorm(ak.reshape(B, S, ATT_KV_HEADS, ATT_HEAD_DIM), k_g)
    av = av.reshape(B, S, ATT_KV_HEADS, ATT_HEAD_DIM)
    att_out = windowed_sink_attention(aq, ak, av, sink)
    qk = jax.nn.silu(centred_depthwise_conv(jnp.concatenate([mq, mk], axis=-1), conv_w, conv_b))
    mq, mk = jnp.split(qk, 2, axis=-1)
    mq = mq.reshape(B, S, ML_HEADS, ML_HEAD_DIM)
    mk = mk.reshape(B, S, ML_HEADS, ML_HEAD_DIM)
    mv = mv.reshape(B, S, ML_HEADS, ML_HEAD_DIM)
    g = gates.astype(jnp.float32) + gate_b.astype(jnp.float32)
    i_f, f_f, i_b, f_b = jnp.split(g, 4, axis=-1)
    h_fwd = mlstm_chunkwise(mq, mk, mv, i_f, jax.nn.log_sigmoid(f_f))
    rev = lambda t: jnp.flip(t, axis=1)
    h_bwd = rev(mlstm_chunkwise(rev(mq), rev(mk), rev(mv), rev(i_b), rev(jax.nn.log_sigmoid(f_b))))
    ml = rmsnorm(h_fwd + h_bwd, out_g.reshape(ML_HEADS, ML_HEAD_DIM)).reshape(B, S, ML_WIDTH)
    ml_out = jax.nn.sigmoid(mo) * ml
    return jnp.concatenate([att_out, ml_out], axis=-1) @ w_out


def memory_cross_attention(h, memn, w_q, w_kv, q_g, k_g, w_o):
    B, S, _ = h.shape
    M = memn.shape[1]
    q = rmsnorm((h @ w_q).reshape(B, S, MEM_HEADS, MEM_HEAD_DIM), q_g)
    k, v = jnp.split(memn @ w_kv, 2, axis=-1)
    k = rmsnorm(k.reshape(B, M, MEM_HEADS, MEM_HEAD_DIM), k_g)
    v = v.reshape(B, M, MEM_HEADS, MEM_HEAD_DIM)
    s = jnp.einsum('bshd,bmhd->bhsm', q, k).astype(jnp.float32) * (MEM_HEAD_DIM ** -0.5)
    p = jax.nn.softmax(s, axis=-1)
    o = jnp.einsum('bhsm,bmhd->bshd', p.astype(v.dtype), v).reshape(B, S, MEM_WIDTH)
    return o @ w_o


def expert_choice_ffn(h, w_router, b_router, w_gate, w_up, w_down):
    B, S, _ = h.shape
    C = CAPACITY_FACTOR * S // N_EXPERTS
    logits = (h @ w_router).astype(jnp.float32) + b_router.astype(jnp.float32)
    aff = jax.nn.softmax(logits, axis=-1)
    gval, idx = lax.top_k(jnp.swapaxes(aff, 1, 2), C)
    bidx = jnp.arange(B)[:, None, None]
    xe = h[bidx, idx]
    hid = jax.nn.silu(jnp.einsum('becd,edf->becf', xe, w_gate)) * jnp.einsum('becd,edf->becf', xe, w_up)
    ye = jnp.einsum('becf,efd->becd', hid, w_down) * gval[..., None].astype(h.dtype)
    return jnp.zeros_like(h).at[bidx, idx].add(ye)


def setup_inputs(seed: int = 0) -> dict:
    key = jax.random.key(seed)
    ks = jax.random.split(key, 32)
    f32 = jnp.float32
    L = DEPTH
    nrm = lambda k, shape, scale: jax.random.normal(k, shape, f32) * scale
    gain = lambda k, shape: 1.0 + 0.05 * jax.random.normal(k, shape, f32)
    fbias = jnp.linspace(3.0, 6.0, ML_HEADS, dtype=f32)
    ml_gate_b = jnp.concatenate([nrm(ks[10], (L, ML_HEADS), 0.1),
                                 fbias + nrm(ks[11], (L, ML_HEADS), 0.1),
                                 nrm(ks[12], (L, ML_HEADS), 0.1),
                                 fbias + nrm(ks[13], (L, ML_HEADS), 0.1)], axis=-1)
    return {
        "x": nrm(ks[0], (BATCH, SEQ, D_MODEL), 1.0),
        "mem": nrm(ks[1], (BATCH, MEM_TOKENS, D_MODEL), 1.0),
        "norm_mix_g": gain(ks[2], (L, D_MODEL)),
        "w_in": nrm(ks[3], (L, D_MODEL, IN_WIDTH), D_MODEL ** -0.5),
        "att_q_norm_g": gain(ks[4], (L, ATT_HEAD_DIM)),
        "att_k_norm_g": gain(ks[5], (L, ATT_HEAD_DIM)),
        "att_sink": nrm(ks[6], (L, ATT_HEADS), 0.5),
        "ml_conv_w": nrm(ks[7], (L, CONV_WIDTH, 2 * ML_WIDTH), CONV_WIDTH ** -0.5),
        "ml_conv_b": nrm(ks[8], (L, 2 * ML_WIDTH), 0.02),
        "ml_gate_b": ml_gate_b,
        "ml_out_norm_g": gain(ks[9], (L, ML_WIDTH)),
        "w_out": nrm(ks[14], (L, MIX_WIDTH, D_MODEL), MIX_WIDTH ** -0.5),
        "norm_mem_g": gain(ks[15], (L, D_MODEL)),
        "mem_kv_norm_g": gain(ks[16], (L, D_MODEL)),
        "w_mem_q": nrm(ks[17], (L, D_MODEL, MEM_WIDTH), D_MODEL ** -0.5),
        "w_mem_kv": nrm(ks[18], (L, D_MODEL, 2 * MEM_WIDTH), D_MODEL ** -0.5),
        "mem_q_norm_g": gain(ks[19], (L, MEM_HEAD_DIM)),
        "mem_k_norm_g": gain(ks[20], (L, MEM_HEAD_DIM)),
        "w_mem_o": nrm(ks[21], (L, MEM_WIDTH, D_MODEL), MEM_WIDTH ** -0.5),
        "norm_ffn_g": gain(ks[22], (L, D_MODEL)),
        "w_router": nrm(ks[23], (L, D_MODEL, N_EXPERTS), D_MODEL ** -0.5),
        "b_router": nrm(ks[24], (L, N_EXPERTS), 0.01),
        "w_exp_gate": nrm(ks[25], (L, N_EXPERTS, D_MODEL, EXPERT_FF), D_MODEL ** -0.5),
        "w_exp_up": nrm(ks[26], (L, N_EXPERTS, D_MODEL, EXPERT_FF), D_MODEL ** -0.5),
        "w_exp_down": nrm(ks[27], (L, N_EXPERTS, EXPERT_FF, D_MODEL), EXPERT_FF ** -0.5),
    }


def reference(x, mem, norm_mix_g, w_in, att_q_norm_g, att_k_norm_g, att_sink, ml_conv_w, ml_conv_b,
              ml_gate_b, ml_out_norm_g, w_out, norm_mem_g, mem_kv_norm_g, w_mem_q, w_mem_kv,
              mem_q_norm_g, mem_k_norm_g, w_mem_o, norm_ffn_g, w_router, b_router,
              w_exp_gate, w_exp_up, w_exp_down):
    for l in range(DEPTH):
        x = x + parallel_head_group_mixer(rmsnorm(x, norm_mix_g[l]), w_in[l], att_q_norm_g[l], att_k_norm_g[l],
                                          att_sink[l], ml_conv_w[l], ml_conv_b[l], ml_gate_b[l],
                                          ml_out_norm_g[l], w_out[l])
        x = x + memory_cross_attention(rmsnorm(x, norm_mem_g[l]), rmsnorm(mem, mem_kv_norm_g[l]),
                                       w_mem_q[l], w_mem_kv[l], mem_q_norm_g[l], mem_k_norm_g[l], w_mem_o[l])
        x = x + expert_choice_ffn(rmsnorm(x, norm_ffn_g[l]), w_router[l], b_router[l],
                                  w_exp_gate[l], w_exp_up[l], w_exp_down[l])
    return x
```

```python
import functools

import jax
import jax.numpy as jnp
from jax import lax
from jax.experimental import pallas as pl
from jax.experimental.pallas import tpu as pltpu

F32 = jnp.float32
BF16 = jnp.bfloat16
I32 = jnp.int32

EPS = 1e-6
LANES = 128
BLK = 128
ATT_HEADS, ATT_KV, ATT_DH = 8, 2, 64
ML_HEADS, ML_DH = 4, 128
MEM_HEADS, MEM_DH = 4, 128
N_EXPERTS = 16
CAPACITY_FACTOR = 2
NEG = -1e30
HALO = 16
VMEM_LIMIT = 48 * 1024 * 1024

_NT = (((1,), (1,)), ((), ()))


def _dot(a, b):
    return jnp.dot(a, b, preferred_element_type=F32)


def _dot_nt(a, b):
    return lax.dot_general(a, b, _NT, preferred_element_type=F32)


def _split3(x):
    hi = x.astype(BF16)
    r1 = x - hi.astype(F32)
    mid = r1.astype(BF16)
    lo = (r1 - mid.astype(F32)).astype(BF16)
    return hi, mid, lo


def _dot01_nt(m01, x):
    hi, mid, lo = _split3(x)
    return _dot_nt(m01, hi) + _dot_nt(m01, mid) + _dot_nt(m01, lo)


def _dot01(x, m01):
    hi, mid, lo = _split3(x)
    return _dot(hi, m01) + _dot(mid, m01) + _dot(lo, m01)


def _rms(x, g):
    ms = jnp.mean(x * x, axis=-1, keepdims=True)
    return x * lax.rsqrt(ms + EPS) * g


def _sigmoid(x):
    return 1.0 / (1.0 + jnp.exp(-x))


def _iota(shape, dim):
    return lax.broadcasted_iota(I32, shape, dim)


def _mem_kv_kernel(mem_ref, g_ref, w_ref, kg_ref, k_ref, v_ref):
    mn = _rms(mem_ref[0], g_ref[...]).astype(BF16)
    kv = _dot(mn, w_ref[...])
    width = MEM_HEADS * MEM_DH
    for h in range(MEM_HEADS):
        sl = slice(h * MEM_DH, (h + 1) * MEM_DH)
        k_ref[0, :, sl] = _rms(kv[:, sl], kg_ref[...]).astype(BF16)
    v_ref[0] = kv[:, width:].astype(BF16)


def _mem_kv(mem, g, w_kv, k_g):
    B, M, D = mem.shape
    width = MEM_HEADS * MEM_DH
    full = lambda *s: pl.BlockSpec(s, lambda b: (0,) * len(s))
    return pl.pallas_call(
        _mem_kv_kernel,
        grid=(B,),
        in_specs=[pl.BlockSpec((1, M, D), lambda b: (b, 0, 0)), full(1, D), full(D, 2 * width), full(1, MEM_DH)],
        out_specs=[pl.BlockSpec((1, M, width), lambda b: (b, 0, 0))] * 2,
        out_shape=[jax.ShapeDtypeStruct((B, M, width), BF16)] * 2,
        compiler_params=pltpu.CompilerParams(dimension_semantics=("arbitrary",), vmem_limit_bytes=VMEM_LIMIT),
        name="mem_kv",
    )(mem, g, w_kv, k_g)


_AQ = (0, 512)
_AK = (512, 640)
_AV = (640, 768)
_MQK = (768, 1792)
_MV = (1792, 2304)
_MO = (2304, 2816)
_GATES = (2816, 2832)


def _inproj_kernel(x_ref, g_ref, w_ref, wgt_ref, gb_ref, qg_ref, kg_ref, bdq_ref, bdk_ref,
                   aq_ref, ak_ref, av_ref, mqk_ref, mv_ref, mo_ref, gt_ref):
    h = _rms(x_ref[0], g_ref[...]).astype(BF16)
    sec = lambda s: _dot(h, w_ref[:, s[0]:s[1]])
    aq = sec(_AQ)
    ssq = _dot((aq * aq).astype(BF16), bdq_ref[...])
    aq_ref[0] = (aq * lax.rsqrt(ssq * (1.0 / ATT_DH) + EPS) * qg_ref[...]).astype(BF16)
    ak = sec(_AK)
    ssk = _dot((ak * ak).astype(BF16), bdk_ref[...])
    ak_ref[0] = (ak * lax.rsqrt(ssk * (1.0 / ATT_DH) + EPS) * kg_ref[...]).astype(BF16)
    av_ref[0] = sec(_AV).astype(BF16)
    mqk_ref[0] = sec(_MQK).astype(BF16)
    mv_ref[0] = sec(_MV).astype(BF16)
    mo_ref[0] = sec(_MO).astype(BF16)
    gt = _dot_nt(wgt_ref[...], h) + gb_ref[...]
    logsig = jnp.minimum(gt, 0.0) - jnp.log(1.0 + jnp.exp(-jnp.abs(gt)))
    row = _iota(gt.shape, 0)
    is_forget = ((row // ML_HEADS) % 2) == 1
    gt_ref[0] = jnp.where(is_forget, logsig, gt)


def _inproj(x, g, w_main, wg_t, gate_b, q_g, k_g, tm):
    B, S, D = x.shape
    bdq = (jnp.arange(512)[:, None] // ATT_DH == jnp.arange(512)[None, :] // ATT_DH).astype(BF16)
    bdk = bdq[:128, :128]
    full = lambda *s: pl.BlockSpec(s, lambda b, i: (0,) * len(s))
    tok = lambda w: pl.BlockSpec((1, tm, w), lambda b, i: (b, i, 0))
    widths = (512, 128, 128, 1024, 512, 512)
    return pl.pallas_call(
        _inproj_kernel,
        grid=(B, S // tm),
        in_specs=[tok(D), full(1, D), full(D, w_main.shape[1]), full(16, D), full(16, 1),
                  full(1, 512), full(1, 128), full(512, 512), full(128, 128)],
        out_specs=[tok(w) for w in widths] + [pl.BlockSpec((1, 16, tm), lambda b, i: (b, 0, i))],
        out_shape=[jax.ShapeDtypeStruct((B, S, w), BF16) for w in widths]
        + [jax.ShapeDtypeStruct((B, 16, S), F32)],
        compiler_params=pltpu.CompilerParams(dimension_semantics=("arbitrary", "arbitrary"),
                                             vmem_limit_bytes=VMEM_LIMIT),
        name="inproj",
    )(x, g, w_main, wg_t, gate_b, q_g, k_g, bdq, bdk)


def _attn_kernel(sink_ref, q_ref, kp_ref, kc_ref, kn_ref, vp_ref, vc_ref, vn_ref, o_ref):
    n = pl.program_id(1)
    nb = pl.num_programs(1)
    shape = (BLK, 3 * BLK)
    qi = _iota(shape, 0)
    si = _iota(shape, 1)
    adist = jnp.abs(qi + BLK - si)
    valid = adist <= BLK
    valid = valid & ((si >= BLK) | (n > 0))
    valid = valid & ((si < 2 * BLK) | (n < nb - 1))
    adist = adist.astype(F32)
    q = q_ref[0]
    outs = []
    for kv in range(ATT_KV):
        sl = slice(kv * ATT_DH, (kv + 1) * ATT_DH)
        kband = jnp.concatenate([kp_ref[0][:, sl], kc_ref[0][:, sl], kn_ref[0][:, sl]], axis=0)
        vband = jnp.concatenate([vp_ref[0][:, sl], vc_ref[0][:, sl], vn_ref[0][:, sl]], axis=0)
        for g in range(ATT_HEADS // ATT_KV):
            h = kv * (ATT_HEADS // ATT_KV) + g
            qh = q[:, h * ATT_DH:(h + 1) * ATT_DH]
            s = _dot_nt(qh, kband)
            slope = 2.0 ** (-8.0 * (h + 1) / ATT_HEADS)
            s = jnp.where(valid, s - slope * adist, NEG)
            sink = sink_ref[h]
            m = jnp.maximum(jnp.max(s, axis=-1, keepdims=True), sink)
            p = jnp.exp(s - m)
            denom = jnp.sum(p, axis=-1, keepdims=True) + jnp.exp(sink - m)
            outs.append(_dot(p.astype(BF16), vband) / denom)
    o_ref[0] = jnp.concatenate(outs, axis=1).astype(BF16)


def _attention(sink, aq, ak, av):
    B, S, _ = aq.shape
    nb = S // BLK
    kvw = ATT_KV * ATT_DH
    prev = pl.BlockSpec((1, BLK, kvw), lambda b, n, *_: (b, jnp.maximum(n - 1, 0), 0))
    cur = pl.BlockSpec((1, BLK, kvw), lambda b, n, *_: (b, n, 0))
    nxt = pl.BlockSpec((1, BLK, kvw), lambda b, n, *_: (b, jnp.minimum(n + 1, nb - 1), 0))
    qspec = pl.BlockSpec((1, BLK, ATT_HEADS * ATT_DH), lambda b, n, *_: (b, n, 0))
    return pl.pallas_call(
        _attn_kernel,
        grid_spec=pltpu.PrefetchScalarGridSpec(
            num_scalar_prefetch=1, grid=(B, nb),
            in_specs=[qspec, prev, cur, nxt, prev, cur, nxt], out_specs=qspec),
        out_shape=jax.ShapeDtypeStruct(aq.shape, BF16),
        compiler_params=pltpu.CompilerParams(dimension_semantics=("arbitrary", "arbitrary"),
                                             vmem_limit_bytes=VMEM_LIMIT),
        name="win_attn",
    )(sink, aq, ak, ak, ak, av, av, av)


def _mlstm_kernel(xf_ref, xfp_ref, xfn_ref, xb_ref, xbp_ref, xbn_ref, vf_ref, vb_ref, gf_ref, gb_ref,
                  cw_ref, cb_ref, hf_ref, hb_ref, c_ref, n_ref, m_ref):
    c = pl.program_id(1)
    nc = pl.num_programs(1)
    L = BLK
    width = ML_HEADS * ML_DH

    @pl.when(c == 0)
    def _():
        c_ref[...] = jnp.zeros_like(c_ref)
        n_ref[...] = jnp.zeros_like(n_ref)
        m_ref[...] = jnp.zeros_like(m_ref)

    row = _iota((L, 1), 0)
    tt = _iota((L, L), 0)
    ss = _iota((L, L), 1)

    def conv_silu(x_ref, p_ref, nx_ref, first, last):
        x = x_ref[0].astype(F32)
        prow = jnp.where(first, 0.0, p_ref[0, HALO - 1:HALO, :].astype(F32))
        nrow = jnp.where(last, 0.0, nx_ref[0, 0:1, :].astype(F32))
        xp = jnp.where(row == 0, prow, pltpu.roll(x, 1, axis=0))
        xn = jnp.where(row == L - 1, nrow, pltpu.roll(x, L - 1, axis=0))
        y = cw_ref[0:1, :] * xp + cw_ref[1:2, :] * x + cw_ref[2:3, :] * xn + cb_ref[...]
        return y * _sigmoid(y)

    def direction(d, qk, v_ref, g_ref, out_ref):
        keep = (ss <= tt) if d == 0 else (ss >= tt)
        cum_nt = keep.astype(BF16)
        cum_nn = ((tt <= ss) if d == 0 else (tt >= ss)).astype(BF16)
        eye = (tt == ss).astype(BF16)
        gt = g_ref[0]
        li = gt[8 * d:8 * d + 4, :]
        lf = gt[8 * d + 4:8 * d + 8, :]
        f_row = _dot01(lf, cum_nn)
        f_col = _dot01_nt(cum_nt, lf)
        li_col = _dot01_nt(eye, li)
        end = L - 1 if d == 0 else 0
        f_end = f_col[end:end + 1, :]
        g_row = li - f_row
        w_end = f_end + li_col - f_col
        m_loc = jnp.max(w_end, axis=0, keepdims=True)
        v_all = v_ref[0]
        for h in range(ML_HEADS):
            u = ML_HEADS * d + h
            hs = slice(h * ML_DH, (h + 1) * ML_DH)
            q = qk[:, hs]
            k = qk[:, width + h * ML_DH: width + (h + 1) * ML_DH] * (ML_DH ** -0.5)
            v = v_all[:, hs]
            qb = q.astype(BF16)
            kb = k.astype(BF16)
            fc = f_col[:, h:h + 1]
            m0 = m_ref[u][:, 0:1]
            c0 = c_ref[u]
            n0 = n_ref[u]
            log_d = jnp.where(keep, fc + g_row[h:h + 1, :], NEG)
            a = fc + m0
            m_t = jnp.maximum(a, jnp.max(log_d, axis=1, keepdims=True))
            e_inter = jnp.exp(a - m_t)
            s_qk = _dot_nt(qb, kb) * jnp.exp(log_d - m_t)
            num = e_inter * _dot(qb, c0.astype(BF16)) + _dot(s_qk.astype(BF16), v)
            den = e_inter * jnp.sum(q * n0, axis=1, keepdims=True) + jnp.sum(s_qk, axis=1, keepdims=True)
            out_ref[0, :, hs] = num / jnp.maximum(jnp.abs(den), jnp.exp(-m_t))
            ml = m_loc[:, h:h + 1]
            ks = k * jnp.exp(w_end[:, h:h + 1] - ml)
            c_loc = _dot(ks.T.astype(BF16), v)
            n_loc = jnp.sum(ks, axis=0, keepdims=True)
            fe = f_end[:, h:h + 1]
            m_new = jnp.maximum(fe + m0, ml)
            ca = jnp.exp(fe + m0 - m_new)
            cb = jnp.exp(ml - m_new)
            c_ref[u] = ca * c0 + cb * c_loc
            n_ref[u] = ca * n0 + cb * n_loc
            m_ref[u] = jnp.broadcast_to(m_new, (1, LANES))

    qk_f = conv_silu(xf_ref, xfp_ref, xfn_ref, c == 0, c == nc - 1)
    direction(0, qk_f, vf_ref, gf_ref, hf_ref)
    qk_b = conv_silu(xb_ref, xbp_ref, xbn_ref, c == nc - 1, c == 0)
    direction(1, qk_b, vb_ref, gb_ref, hb_ref)


def _mlstm(mqk, mv, gates_t, conv_w, conv_b):
    B, S, _ = mqk.shape
    nc = S // BLK
    width = ML_HEADS * ML_DH
    per = BLK // HALO
    nh = S // HALO
    fwd = lambda c: c
    bwd = lambda c: nc - 1 - c

    def specs(ci):
        own = pl.BlockSpec((1, BLK, 2 * width), lambda b, c: (b, ci(c), 0))
        prev = pl.BlockSpec((1, HALO, 2 * width), lambda b, c: (b, jnp.maximum(ci(c) * per - 1, 0), 0))
        nxt = pl.BlockSpec((1, HALO, 2 * width), lambda b, c: (b, jnp.minimum((ci(c) + 1) * per, nh - 1), 0))
        return [own, prev, nxt]

    vspec = lambda ci: pl.BlockSpec((1, BLK, width), lambda b, c: (b, ci(c), 0))
    gspec = lambda ci: pl.BlockSpec((1, 16, BLK), lambda b, c: (b, 0, ci(c)))
    full = lambda *s: pl.BlockSpec(s, lambda b, c: (0,) * len(s))
    units = 2 * ML_HEADS
    return pl.pallas_call(
        _mlstm_kernel,
        grid=(B, nc),
        in_specs=specs(fwd) + specs(bwd) + [vspec(fwd), vspec(bwd), gspec(fwd), gspec(bwd),
                                            full(3, 2 * width), full(1, 2 * width)],
        out_specs=[vspec(fwd), vspec(bwd)],
        out_shape=[jax.ShapeDtypeStruct((B, S, width), F32)] * 2,
        scratch_shapes=[pltpu.VMEM((units, ML_DH, ML_DH), F32), pltpu.VMEM((units, 1, ML_DH), F32),
                        pltpu.VMEM((units, 1, LANES), F32)],
        compiler_params=pltpu.CompilerParams(dimension_semantics=("arbitrary", "arbitrary"),
                                             vmem_limit_bytes=VMEM_LIMIT),
        name="mlstm",
    )(mqk, mqk, mqk, mqk, mqk, mqk, mv, mv, gates_t, gates_t, conv_w, conv_b)


def _mix_kernel(x_ref, att_ref, hf_ref, hb_ref, mo_ref, og_ref, wo_ref,
                gm_ref, wq_ref, mqg_ref, k_ref, v_ref, wmo_ref,
                gf_ref, wr_ref, wrt_ref, br_ref, brt_ref,
                y_ref, hn_ref, afft_ref):
    x = x_ref[0]
    width = ML_HEADS * ML_DH
    ml = hf_ref[0] + hb_ref[0]
    mo = mo_ref[0].astype(F32)
    parts = []
    for h in range(ML_HEADS):
        sl = slice(h * ML_DH, (h + 1) * ML_DH)
        parts.append((_sigmoid(mo[:, sl]) * _rms(ml[:, sl], og_ref[:, sl])).astype(BF16))
    ml_out = jnp.concatenate(parts, axis=1)
    y1 = x + _dot(att_ref[0], wo_ref[0:width, :]) + _dot(ml_out, wo_ref[width:2 * width, :])
    h2 = _rms(y1, gm_ref[...]).astype(BF16)
    qm = _dot(h2, wq_ref[...])
    k = k_ref[0]
    v = v_ref[0]
    outs = []
    for h in range(MEM_HEADS):
        sl = slice(h * MEM_DH, (h + 1) * MEM_DH)
        qh = (_rms(qm[:, sl], mqg_ref[...]) * (MEM_DH ** -0.5)).astype(BF16)
        s = _dot_nt(qh, k[:, sl])
        p = jnp.exp(s - jnp.max(s, axis=-1, keepdims=True))
        o = _dot(p.astype(BF16), v[:, sl]) / jnp.sum(p, axis=-1, keepdims=True)
        outs.append(o.astype(BF16))
    y2 = y1 + _dot(jnp.concatenate(outs, axis=1), wmo_ref[...])
    y_ref[0] = y2
    h3 = _rms(y2, gf_ref[...])
    h_hi = h3.astype(BF16)
    h_lo = (h3 - h_hi.astype(F32)).astype(BF16)
    wr = wr_ref[...]
    w_hi = wr.astype(BF16)
    w_lo = (wr - w_hi.astype(F32)).astype(BF16)
    logits = _dot(h_hi, w_hi) + _dot(h_lo, w_hi) + _dot(h_hi, w_lo) + br_ref[...]
    pe = jnp.exp(logits - jnp.max(logits, axis=-1, keepdims=True))
    aff = pe / jnp.sum(pe, axis=-1, keepdims=True)
    wrt = wrt_ref[...]
    wt_hi = wrt.astype(BF16)
    wt_lo = (wrt - wt_hi.astype(F32)).astype(BF16)
    logits_t = _dot_nt(wt_hi, h_hi) + _dot_nt(wt_hi, h_lo) + _dot_nt(wt_lo, h_hi) + brt_ref[...]
    pt = jnp.exp(logits_t - jnp.max(logits_t, axis=0, keepdims=True))
    afft_ref[0] = pt / jnp.sum(pt, axis=0, keepdims=True)
    d = h3.shape[1]
    hn_ref[0, :, 0:d] = h3
    pad = jnp.zeros((h3.shape[0], LANES - N_EXPERTS), F32)
    hn_ref[0, :, d:d + LANES] = jnp.concatenate([aff, pad], axis=1)


def _mix(x, att, hf, hb, mo, out_g, w_out, g_mem, w_q, mq_g, mem_k, mem_v, w_mo, g_ffn, w_r, b_r, tm):
    B, S, D = x.shape
    M = mem_k.shape[1]
    width = ML_HEADS * ML_DH
    mw = MEM_HEADS * MEM_DH
    full = lambda *s: pl.BlockSpec(s, lambda b, i: (0,) * len(s))
    tok = lambda w: pl.BlockSpec((1, tm, w), lambda b, i: (b, i, 0))
    memspec = pl.BlockSpec((1, M, mw), lambda b, i: (b, 0, 0))
    return pl.pallas_call(
        _mix_kernel,
        grid=(B, S // tm),
        in_specs=[tok(D), tok(width), tok(width), tok(width), tok(width), full(1, width), full(2 * width, D),
                  full(1, D), full(D, mw), full(1, MEM_DH), memspec, memspec, full(mw, D),
                  full(1, D), full(D, N_EXPERTS), full(N_EXPERTS, D), full(1, N_EXPERTS), full(N_EXPERTS, 1)],
        out_specs=[tok(D), tok(D + LANES), pl.BlockSpec((1, N_EXPERTS, tm), lambda b, i: (b, 0, i))],
        out_shape=[jax.ShapeDtypeStruct((B, S, D), F32), jax.ShapeDtypeStruct((B, S, D + LANES), F32),
                   jax.ShapeDtypeStruct((B, N_EXPERTS, S), F32)],
        compiler_params=pltpu.CompilerParams(dimension_semantics=("arbitrary", "arbitrary"),
                                             vmem_limit_bytes=VMEM_LIMIT),
        name="mix_mem_router",
    )(x, att, hf, hb, mo, out_g, w_out, g_mem, w_q, mq_g, mem_k, mem_v, w_mo, g_ffn,
      w_r, w_r.T, b_r, b_r.T)


def _topc_kernel(aff2_ref, aff3_ref, idx_ref, *, cap):
    a2 = aff2_ref[0]
    bits2 = pltpu.bitcast(a2, I32)
    capf = float(cap)

    def bisect(i, lo):
        cand = lo | jnp.left_shift(jnp.int32(1), 30 - i)
        cnt = jnp.sum((bits2 >= cand).astype(F32), axis=1, keepdims=True)
        return jnp.where(cnt >= capf, cand, lo)

    thr_all = lax.fori_loop(0, 31, bisect, jnp.zeros((N_EXPERTS, 1), I32))
    need_all = capf - jnp.sum((bits2 > thr_all).astype(F32), axis=1, keepdims=True)

    T = aff3_ref.shape[2]
    tri_u = (_iota((LANES, LANES), 0) <= _iota((LANES, LANES), 1)).astype(BF16)
    tri_l = (_iota((LANES, LANES), 1) <= _iota((LANES, LANES), 0)).astype(BF16)
    ones8 = jnp.ones((8, LANES), BF16)
    before = _iota((T, T), 1) < _iota((T, T), 0)
    kcol = _iota((T, 1), 0).astype(F32)
    j = _iota((1, cap), 1).astype(F32)

    def tile_starts(maskb):
        tot_row = _dot_nt(ones8, maskb)[0:1, :]
        return jnp.sum(jnp.where(before, tot_row, 0.0), axis=1, keepdims=True)

    for e in range(N_EXPERTS):
        bits = pltpu.bitcast(aff3_ref[0, e], I32)
        thr = thr_all[e:e + 1, :]
        need = need_all[e:e + 1, :]
        gt = bits > thr
        eq = bits == thr
        eqb = eq.astype(BF16)
        eq_rank = _dot(eqb, tri_u) + tile_starts(eqb) - eq.astype(F32)
        sel = gt | (eq & (eq_rank < need))
        selb = sel.astype(BF16)
        cs = _dot(selb, tri_u)
        start = tile_starts(selb)
        end = start + cs[:, LANES - 1:LANES]
        onehot = (start <= j) & (j < end)
        ohf = onehot.astype(F32)
        tile_of = jnp.sum(ohf * kcol, axis=0, keepdims=True)
        j_loc = j - jnp.sum(ohf * start, axis=0, keepdims=True)
        cs_t = _dot_nt(tri_l, selb)
        r_t = _dot(cs_t.astype(BF16), onehot.astype(BF16))
        local = jnp.sum((r_t <= j_loc).astype(F32), axis=0, keepdims=True)
        idx_ref[0, e:e + 1, :] = (tile_of * LANES + local).astype(I32)


def _topc(aff_t, cap):
    B, E, S = aff_t.shape
    T = S // LANES
    aff3 = aff_t.reshape(B, E, T, LANES)
    return pl.pallas_call(
        functools.partial(_topc_kernel, cap=cap),
        grid=(B,),
        in_specs=[pl.BlockSpec((1, E, S), lambda b: (b, 0, 0)), pl.BlockSpec((1, E, T, LANES), lambda b: (b, 0, 0, 0))],
        out_specs=pl.BlockSpec((1, E, cap), lambda b: (b, 0, 0)),
        out_shape=jax.ShapeDtypeStruct((B, E, cap), I32),
        compiler_params=pltpu.CompilerParams(dimension_semantics=("arbitrary",), vmem_limit_bytes=VMEM_LIMIT),
        name="topc",
    )(aff_t, aff3)


def _ffn_kernel(idx_ref, hn_hbm, acc_in_hbm, wg_ref, wu_ref, wd_ref, acc_hbm, xe_ref, ab_ref, sems,
                *, seq, cap, d_model, row_block):
    del acc_in_hbm
    e = pl.program_id(0)
    b = pl.program_id(1)
    base = b * seq

    def x_copy(jrow):
        r = base + idx_ref[0, 0, jrow]
        return pltpu.make_async_copy(hn_hbm.at[pl.ds(r, 1)], xe_ref.at[pl.ds(jrow, 1)], sems.at[0])

    def a_copy(jrow):
        r = base + idx_ref[0, 0, jrow]
        return pltpu.make_async_copy(acc_hbm.at[pl.ds(r, 1)], ab_ref.at[pl.ds(jrow, 1)], sems.at[1])

    def o_copy(jrow):
        r = base + idx_ref[0, 0, jrow]
        return pltpu.make_async_copy(ab_ref.at[pl.ds(jrow, 1)], acc_hbm.at[pl.ds(r, 1)], sems.at[2])

    @pl.loop(0, cap)
    def _(jrow):
        x_copy(jrow).start()
        a_copy(jrow).start()

    @pl.loop(0, cap)
    def _(jrow):
        x_copy(jrow).wait()
        a_copy(jrow).wait()

    lane = _iota((1, LANES), 1)
    pick = (lane == e).astype(F32)

    @pl.loop(0, cap // row_block)
    def _(rb):
        rows = pl.ds(pl.multiple_of(rb * row_block, row_block), row_block)
        xb = xe_ref[rows, 0:d_model].astype(BF16)
        gval = jnp.sum(xe_ref[rows, d_model:d_model + LANES] * pick, axis=1, keepdims=True)
        hg = _dot(xb, wg_ref[0])
        hu = _dot(xb, wu_ref[0])
        hid = (hg * _sigmoid(hg) * hu).astype(BF16)
        ab_ref[rows, :] += _dot(hid, wd_ref[0]) * gval

    @pl.loop(0, cap)
    def _(jrow):
        o_copy(jrow).start()

    @pl.loop(0, cap)
    def _(jrow):
        o_copy(jrow).wait()


def _expert_ffn(idx, hn_ext, acc, wg, wu, wd, seq):
    B, E, cap = idx.shape
    N, dx = hn_ext.shape
    d_model = acc.shape[1]
    ff = wg.shape[2]
    idx3 = idx.reshape(B * E, 1, cap)
    any_spec = pl.BlockSpec(memory_space=pl.ANY)
    wspec = lambda r, c: pl.BlockSpec((1, r, c), lambda e, b: (e, 0, 0))
    kern = functools.partial(_ffn_kernel, seq=seq, cap=cap, d_model=d_model, row_block=256 if cap % 256 == 0 else cap)
    return pl.pallas_call(
        kern,
        grid=(E, B),
        in_specs=[pl.BlockSpec((1, 1, cap), lambda e, b: (b * E + e, 0, 0), memory_space=pltpu.SMEM),
                  any_spec, any_spec, wspec(d_model, ff), wspec(d_model, ff), wspec(ff, d_model)],
        out_specs=any_spec,
        out_shape=jax.ShapeDtypeStruct(acc.shape, F32),
        scratch_shapes=[pltpu.VMEM((cap, dx), F32), pltpu.VMEM((cap, d_model), F32), pltpu.SemaphoreType.DMA((3,))],
        input_output_aliases={2: 0},
        compiler_params=pltpu.CompilerParams(dimension_semantics=("arbitrary", "arbitrary"),
                                             vmem_limit_bytes=VMEM_LIMIT),
        name="expert_ffn",
    )(idx3, hn_ext, acc, wg, wu, wd)


def _layer(x, mem, norm_mix_g, w_in, att_q_norm_g, att_k_norm_g, att_sink, ml_conv_w, ml_conv_b,
           ml_gate_b, ml_out_norm_g, w_out, norm_mem_g, mem_kv_norm_g, w_mem_q, w_mem_kv,
           mem_q_norm_g, mem_k_norm_g, w_mem_o, norm_ffn_g, w_router, b_router,
           w_exp_gate, w_exp_up, w_exp_down):
    B, S, D = x.shape
    row = lambda v: v.reshape(1, -1).astype(F32)
    tm_in = min(512, S)
    tm_mix = min(256, S)
    cap = CAPACITY_FACTOR * S // N_EXPERTS

    mem_k, mem_v = _mem_kv(mem, row(mem_kv_norm_g), w_mem_kv.astype(BF16), row(mem_k_norm_g))

    w_main = w_in[:, :_GATES[0]].astype(BF16)
    wg_t = w_in[:, _GATES[0]:_GATES[1]].T.astype(BF16)
    q_g = jnp.tile(row(att_q_norm_g), (1, ATT_HEADS)) * (ATT_DH ** -0.5)
    k_g = jnp.tile(row(att_k_norm_g), (1, ATT_KV))
    aq, ak, av, mqk, mv, mo, gates_t = _inproj(x, row(norm_mix_g), w_main, wg_t, ml_gate_b.reshape(-1, 1).astype(F32),
                                               q_g, k_g, tm_in)

    att = _attention(att_sink.astype(F32), aq, ak, av)
    hf, hb = _mlstm(mqk, mv, gates_t, ml_conv_w.astype(F32), row(ml_conv_b))

    y2, hn_ext, aff_t = _mix(x, att, hf, hb, mo, row(ml_out_norm_g), w_out.astype(BF16), row(norm_mem_g),
                             w_mem_q.astype(BF16), row(mem_q_norm_g), mem_k, mem_v, w_mem_o.astype(BF16),
                             row(norm_ffn_g), w_router.astype(F32), row(b_router), tm_mix)

    idx = _topc(aff_t, cap)
    out = _expert_ffn(idx, hn_ext.reshape(B * S, D + LANES), y2.reshape(B * S, D),
                      w_exp_gate.astype(BF16), w_exp_up.astype(BF16), w_exp_down.astype(BF16), S)
    return out.reshape(B, S, D)


def kernel(x, mem, norm_mix_g, w_in, att_q_norm_g, att_k_norm_g, att_sink, ml_conv_w, ml_conv_b, ml_gate_b,
           ml_out_norm_g, w_out, norm_mem_g, mem_kv_norm_g, w_mem_q, w_mem_kv, mem_q_norm_g, mem_k_norm_g,
           w_mem_o, norm_ffn_g, w_router, b_router, w_exp_gate, w_exp_up, w_exp_down):
    params = (norm_mix_g, w_in, att_q_norm_g, att_k_norm_g, att_sink, ml_conv_w, ml_conv_b, ml_gate_b,
              ml_out_norm_g, w_out, norm_mem_g, mem_kv_norm_g, w_mem_q, w_mem_kv, mem_q_norm_g, mem_k_norm_g,
              w_mem_o, norm_ffn_g, w_router, b_router, w_exp_gate, w_exp_up, w_exp_down)
    depth = norm_mix_g.shape[0]
    for l in range(depth):
        x = _layer(x, mem, *[p[l] for p in params])
    return x
```

```python
import functools

import jax
import jax.numpy as jnp
from jax import lax
from jax.experimental import pallas as pl
from jax.experimental.pallas import tpu as pltpu

F32 = jnp.float32
BF16 = jnp.bfloat16
I32 = jnp.int32

EPS = 1e-6
LANES = 128
BLK = 128
ATT_HEADS, ATT_KV, ATT_DH = 8, 2, 64
ML_HEADS, ML_DH = 4, 128
MEM_HEADS, MEM_DH = 4, 128
N_EXPERTS = 16
CAPACITY_FACTOR = 2
NEG = -1e30
HALO = 16
VMEM_LIMIT = 48 * 1024 * 1024

_NT = (((1,), (1,)), ((), ()))


def _dot(a, b):
    return jnp.dot(a, b, preferred_element_type=F32)


def _dot_nt(a, b):
    return lax.dot_general(a, b, _NT, preferred_element_type=F32)


def _split3(x):
    hi = x.astype(BF16)
    r1 = x - hi.astype(F32)
    mid = r1.astype(BF16)
    lo = (r1 - mid.astype(F32)).astype(BF16)
    return hi, mid, lo


def _dot01_nt(m01, x):
    hi, mid, lo = _split3(x)
    return _dot_nt(m01, hi) + _dot_nt(m01, mid) + _dot_nt(m01, lo)


def _dot01(x, m01):
    hi, mid, lo = _split3(x)
    return _dot(hi, m01) + _dot(mid, m01) + _dot(lo, m01)


def _rms(x, g):
    ms = jnp.mean(x * x, axis=-1, keepdims=True)
    return x * lax.rsqrt(ms + EPS) * g


def _sigmoid(x):
    return 1.0 / (1.0 + jnp.exp(-x))


def _iota(shape, dim):
    return lax.broadcasted_iota(I32, shape, dim)


def _mem_kv_kernel(mem_ref, g_ref, w_ref, kg_ref, k_ref, v_ref):
    mn = _rms(mem_ref[0], g_ref[...]).astype(BF16)
    kv = _dot(mn, w_ref[...])
    width = MEM_HEADS * MEM_DH
    for h in range(MEM_HEADS):
        sl = slice(h * MEM_DH, (h + 1) * MEM_DH)
        k_ref[0, :, sl] = _rms(kv[:, sl], kg_ref[...]).astype(BF16)
    v_ref[0] = kv[:, width:].astype(BF16)


def _mem_kv(mem, g, w_kv, k_g):
    B, M, D = mem.shape
    width = MEM_HEADS * MEM_DH
    full = lambda *s: pl.BlockSpec(s, lambda b: (0,) * len(s))
    return pl.pallas_call(
        _mem_kv_kernel,
        grid=(B,),
        in_specs=[pl.BlockSpec((1, M, D), lambda b: (b, 0, 0)), full(1, D), full(D, 2 * width), full(1, MEM_DH)],
        out_specs=[pl.BlockSpec((1, M, width), lambda b: (b, 0, 0))] * 2,
        out_shape=[jax.ShapeDtypeStruct((B, M, width), BF16)] * 2,
        compiler_params=pltpu.CompilerParams(dimension_semantics=("arbitrary",), vmem_limit_bytes=VMEM_LIMIT),
        name="mem_kv",
    )(mem, g, w_kv, k_g)


_AQ = (0, 512)
_AK = (512, 640)
_AV = (640, 768)
_MQK = (768, 1792)
_MV = (1792, 2304)
_MO = (2304, 2816)
_GATES = (2816, 2832)


def _inproj_kernel(x_ref, g_ref, w_ref, wgt_ref, gb_ref, qg_ref, kg_ref, bdq_ref, bdk_ref,
                   aq_ref, ak_ref, av_ref, mqk_ref, mv_ref, mo_ref, gt_ref):
    h = _rms(x_ref[0], g_ref[...]).astype(BF16)
    sec = lambda s: _dot(h, w_ref[:, s[0]:s[1]])
    aq = sec(_AQ)
    ssq = _dot((aq * aq).astype(BF16), bdq_ref[...])
    aq_ref[0] = (aq * lax.rsqrt(ssq * (1.0 / ATT_DH) + EPS) * qg_ref[...]).astype(BF16)
    ak = sec(_AK)
    ssk = _dot((ak * ak).astype(BF16), bdk_ref[...])
    ak_ref[0] = (ak * lax.rsqrt(ssk * (1.0 / ATT_DH) + EPS) * kg_ref[...]).astype(BF16)
    av_ref[0] = sec(_AV).astype(BF16)
    mqk_ref[0] = sec(_MQK).astype(BF16)
    mv_ref[0] = sec(_MV).astype(BF16)
    mo_ref[0] = sec(_MO).astype(BF16)
    gt = _dot_nt(wgt_ref[...], h) + gb_ref[...]
    logsig = jnp.minimum(gt, 0.0) - jnp.log(1.0 + jnp.exp(-jnp.abs(gt)))
    row = _iota(gt.shape, 0)
    is_forget = ((row // ML_HEADS) % 2) == 1
    gt_ref[0] = jnp.where(is_forget, logsig, gt)


def _inproj(x, g, w_main, wg_t, gate_b, q_g, k_g, tm):
    B, S, D = x.shape
    bdq = (jnp.arange(512)[:, None] // ATT_DH == jnp.arange(512)[None, :] // ATT_DH).astype(BF16)
    bdk = bdq[:128, :128]
    full = lambda *s: pl.BlockSpec(s, lambda b, i: (0,) * len(s))
    tok = lambda w: pl.BlockSpec((1, tm, w), lambda b, i: (b, i, 0))
    widths = (512, 128, 128, 1024, 512, 512)
    return pl.pallas_call(
        _inproj_kernel,
        grid=(B, S // tm),
        in_specs=[tok(D), full(1, D), full(D, w_main.shape[1]), full(16, D), full(16, 1),
                  full(1, 512), full(1, 128), full(512, 512), full(128, 128)],
        out_specs=[tok(w) for w in widths] + [pl.BlockSpec((1, 16, tm), lambda b, i: (b, 0, i))],
        out_shape=[jax.ShapeDtypeStruct((B, S, w), BF16) for w in widths]
        + [jax.ShapeDtypeStruct((B, 16, S), F32)],
        compiler_params=pltpu.CompilerParams(dimension_semantics=("arbitrary", "arbitrary"),
                                             vmem_limit_bytes=VMEM_LIMIT),
        name="inproj",
    )(x, g, w_main, wg_t, gate_b, q_g, k_g, bdq, bdk)


def _attn_kernel(sink_ref, q_ref, kp_ref, kc_ref, kn_ref, vp_ref, vc_ref, vn_ref, o_ref):
    n = pl.program_id(1)
    nb = pl.num_programs(1)
    shape = (BLK, 3 * BLK)
    qi = _iota(shape, 0)
    si = _iota(shape, 1)
    adist = jnp.abs(qi + BLK - si)
    valid = adist <= BLK
    valid = valid & ((si >= BLK) | (n > 0))
    valid = valid & ((si < 2 * BLK) | (n < nb - 1))
    adist = adist.astype(F32)
    q = q_ref[0]
    outs = []
    for kv in range(ATT_KV):
        sl = slice(kv * ATT_DH, (kv + 1) * ATT_DH)
        kband = jnp.concatenate([kp_ref[0][:, sl], kc_ref[0][:, sl], kn_ref[0][:, sl]], axis=0)
        vband = jnp.concatenate([vp_ref[0][:, sl], vc_ref[0][:, sl], vn_ref[0][:, sl]], axis=0)
        for g in range(ATT_HEADS // ATT_KV):
            h = kv * (ATT_HEADS // ATT_KV) + g
            qh = q[:, h * ATT_DH:(h + 1) * ATT_DH]
            s = _dot_nt(qh, kband)
            slope = 2.0 ** (-8.0 * (h + 1) / ATT_HEADS)
            s = jnp.where(valid, s - slope * adist, NEG)
            sink = sink_ref[h]
            m = jnp.maximum(jnp.max(s, axis=-1, keepdims=True), sink)
            p = jnp.exp(s - m)
            denom = jnp.sum(p, axis=-1, keepdims=True) + jnp.exp(sink - m)
            outs.append(_dot(p.astype(BF16), vband) / denom)
    o_ref[0] = jnp.concatenate(outs, axis=1).astype(BF16)


def _attention(sink, aq, ak, av):
    B, S, _ = aq.shape
    nb = S // BLK
    kvw = ATT_KV * ATT_DH
    prev = pl.BlockSpec((1, BLK, kvw), lambda b, n, *_: (b, jnp.maximum(n - 1, 0), 0))
    cur = pl.BlockSpec((1, BLK, kvw), lambda b, n, *_: (b, n, 0))
    nxt = pl.BlockSpec((1, BLK, kvw), lambda b, n, *_: (b, jnp.minimum(n + 1, nb - 1), 0))
    qspec = pl.BlockSpec((1, BLK, ATT_HEADS * ATT_DH), lambda b, n, *_: (b, n, 0))
    return pl.pallas_call(
        _attn_kernel,
        grid_spec=pltpu.PrefetchScalarGridSpec(
            num_scalar_prefetch=1, grid=(B, nb),
            in_specs=[qspec, prev, cur, nxt, prev, cur, nxt], out_specs=qspec),
        out_shape=jax.ShapeDtypeStruct(aq.shape, BF16),
        compiler_params=pltpu.CompilerParams(dimension_semantics=("arbitrary", "arbitrary"),
                                             vmem_limit_bytes=VMEM_LIMIT),
        name="win_attn",
    )(sink, aq, ak, ak, ak, av, av, av)


def _mlstm_kernel(xf_ref, xfp_ref, xfn_ref, xb_ref, xbp_ref, xbn_ref, vf_ref, vb_ref, gf_ref, gb_ref,
                  cw_ref, cb_ref, hf_ref, hb_ref, c_ref, n_ref, m_ref):
    c = pl.program_id(1)
    nc = pl.num_programs(1)
    L = BLK
    width = ML_HEADS * ML_DH

    @pl.when(c == 0)
    def _():
        c_ref[...] = jnp.zeros_like(c_ref)
        n_ref[...] = jnp.zeros_like(n_ref)
        m_ref[...] = jnp.zeros_like(m_ref)

    row = _iota((L, 1), 0)
    tt = _iota((L, L), 0)
    ss = _iota((L, L), 1)

    def conv_silu(x_ref, p_ref, nx_ref, first, last):
        x = x_ref[0].astype(F32)
        prow = jnp.where(first, 0.0, p_ref[0, HALO - 1:HALO, :].astype(F32))
        nrow = jnp.where(last, 0.0, nx_ref[0, 0:1, :].astype(F32))
        xp = jnp.where(row == 0, prow, pltpu.roll(x, 1, axis=0))
        xn = jnp.where(row == L - 1, nrow, pltpu.roll(x, L - 1, axis=0))
        y = cw_ref[0:1, :] * xp + cw_ref[1:2, :] * x + cw_ref[2:3, :] * xn + cb_ref[...]
        return y * _sigmoid(y)

    def direction(d, qk, v_ref, g_ref, out_ref):
        keep = (ss <= tt) if d == 0 else (ss >= tt)
        cum_nt = keep.astype(BF16)
        cum_nn = ((tt <= ss) if d == 0 else (tt >= ss)).astype(BF16)
        eye = (tt == ss).astype(BF16)
        gt = g_ref[0]
        li = gt[8 * d:8 * d + 4, :]
        lf = gt[8 * d + 4:8 * d + 8, :]
        f_row = _dot01(lf, cum_nn)
        f_col = _dot01_nt(cum_nt, lf)
        li_col = _dot01_nt(eye, li)
        end = L - 1 if d == 0 else 0
        f_end = f_col[end:end + 1, :]
        g_row = li - f_row
        w_end = f_end + li_col - f_col
        m_loc = jnp.max(w_end, axis=0, keepdims=True)
        v_all = v_ref[0]
        for h in range(ML_HEADS):
            u = ML_HEADS * d + h
            hs = slice(h * ML_DH, (h + 1) * ML_DH)
            q = qk[:, hs]
            k = qk[:, width + h * ML_DH: width + (h + 1) * ML_DH] * (ML_DH ** -0.5)
            v = v_all[:, hs]
            qb = q.astype(BF16)
            kb = k.astype(BF16)
            fc = f_col[:, h:h + 1]
            m0 = m_ref[u][:, 0:1]
            c0 = c_ref[u]
            n0 = n_ref[u]
            log_d = jnp.where(keep, fc + g_row[h:h + 1, :], NEG)
            a = fc + m0
            m_t = jnp.maximum(a, jnp.max(log_d, axis=1, keepdims=True))
            e_inter = jnp.exp(a - m_t)
            s_qk = _dot_nt(qb, kb) * jnp.exp(log_d - m_t)
            num = e_inter * _dot(qb, c0.astype(BF16)) + _dot(s_qk.astype(BF16), v)
            den = e_inter * jnp.sum(q * n0, axis=1, keepdims=True) + jnp.sum(s_qk, axis=1, keepdims=True)
            out_ref[0, :, hs] = num / jnp.maximum(jnp.abs(den), jnp.exp(-m_t))
            ml = m_loc[:, h:h + 1]
            ks = k * jnp.exp(w_end[:, h:h + 1] - ml)
            c_loc = _dot(ks.T.astype(BF16), v)
            n_loc = jnp.sum(ks, axis=0, keepdims=True)
            fe = f_end[:, h:h + 1]
            m_new = jnp.maximum(fe + m0, ml)
            ca = jnp.exp(fe + m0 - m_new)
            cb = jnp.exp(ml - m_new)
            c_ref[u] = ca * c0 + cb * c_loc
            n_ref[u] = ca * n0 + cb * n_loc
            m_ref[u] = jnp.broadcast_to(m_new, (1, LANES))

    qk_f = conv_silu(xf_ref, xfp_ref, xfn_ref, c == 0, c == nc - 1)
    direction(0, qk_f, vf_ref, gf_ref, hf_ref)
    qk_b = conv_silu(xb_ref, xbp_ref, xbn_ref, c == nc - 1, c == 0)
    direction(1, qk_b, vb_ref, gb_ref, hb_ref)


def _mlstm(mqk, mv, gates_t, conv_w, conv_b):
    B, S, _ = mqk.shape
    nc = S // BLK
    width = ML_HEADS * ML_DH
    per = BLK // HALO
    nh = S // HALO
    fwd = lambda c: c
    bwd = lambda c: nc - 1 - c

    def specs(ci):
        own = pl.BlockSpec((1, BLK, 2 * width), lambda b, c: (b, ci(c), 0))
        prev = pl.BlockSpec((1, HALO, 2 * width), lambda b, c: (b, jnp.maximum(ci(c) * per - 1, 0), 0))
        nxt = pl.BlockSpec((1, HALO, 2 * width), lambda b, c: (b, jnp.minimum((ci(c) + 1) * per, nh - 1), 0))
        return [own, prev, nxt]

    vspec = lambda ci: pl.BlockSpec((1, BLK, width), lambda b, c: (b, ci(c), 0))
    gspec = lambda ci: pl.BlockSpec((1, 16, BLK), lambda b, c: (b, 0, ci(c)))
    full = lambda *s: pl.BlockSpec(s, lambda b, c: (0,) * len(s))
    units = 2 * ML_HEADS
    return pl.pallas_call(
        _mlstm_kernel,
        grid=(B, nc),
        in_specs=specs(fwd) + specs(bwd) + [vspec(fwd), vspec(bwd), gspec(fwd), gspec(bwd),
                                            full(3, 2 * width), full(1, 2 * width)],
        out_specs=[vspec(fwd), vspec(bwd)],
        out_shape=[jax.ShapeDtypeStruct((B, S, width), F32)] * 2,
        scratch_shapes=[pltpu.VMEM((units, ML_DH, ML_DH), F32), pltpu.VMEM((units, 1, ML_DH), F32),
                        pltpu.VMEM((units, 1, LANES), F32)],
        compiler_params=pltpu.CompilerParams(dimension_semantics=("arbitrary", "arbitrary"),
                                             vmem_limit_bytes=VMEM_LIMIT),
        name="mlstm",
    )(mqk, mqk, mqk, mqk, mqk, mqk, mv, mv, gates_t, gates_t, conv_w, conv_b)


def _mix_kernel(x_ref, att_ref, hf_ref, hb_ref, mo_ref, og_ref, wo_ref,
                gm_ref, wq_ref, mqg_ref, k_ref, v_ref, wmo_ref,
                gf_ref, wrt_ref, brt_ref,
                y_ref, hn_ref, afft_ref):
    x = x_ref[0]
    width = ML_HEADS * ML_DH
    ml = hf_ref[0] + hb_ref[0]
    mo = mo_ref[0].astype(F32)
    parts = []
    for h in range(ML_HEADS):
        sl = slice(h * ML_DH, (h + 1) * ML_DH)
        parts.append((_sigmoid(mo[:, sl]) * _rms(ml[:, sl], og_ref[:, sl])).astype(BF16))
    ml_out = jnp.concatenate(parts, axis=1)
    y1 = x + _dot(att_ref[0], wo_ref[0:width, :]) + _dot(ml_out, wo_ref[width:2 * width, :])
    h2 = _rms(y1, gm_ref[...]).astype(BF16)
    qm = _dot(h2, wq_ref[...])
    k = k_ref[0]
    v = v_ref[0]
    outs = []
    for h in range(MEM_HEADS):
        sl = slice(h * MEM_DH, (h + 1) * MEM_DH)
        qh = (_rms(qm[:, sl], mqg_ref[...]) * (MEM_DH ** -0.5)).astype(BF16)
        s = _dot_nt(qh, k[:, sl])
        p = jnp.exp(s - jnp.max(s, axis=-1, keepdims=True))
        o = _dot(p.astype(BF16), v[:, sl]) / jnp.sum(p, axis=-1, keepdims=True)
        outs.append(o.astype(BF16))
    y2 = y1 + _dot(jnp.concatenate(outs, axis=1), wmo_ref[...])
    h3 = _rms(y2, gf_ref[...])
    h_hi = h3.astype(BF16)
    h_lo = (h3 - h_hi.astype(F32)).astype(BF16)
    wrt = wrt_ref[...]
    wt_hi = wrt.astype(BF16)
    wt_lo = (wrt - wt_hi.astype(F32)).astype(BF16)
    logits_t = _dot_nt(wt_hi, h_hi) + _dot_nt(wt_hi, h_lo) + _dot_nt(wt_lo, h_hi) + brt_ref[...]
    pt = jnp.exp(logits_t - jnp.max(logits_t, axis=0, keepdims=True))
    afft_ref[0] = pt / jnp.sum(pt, axis=0, keepdims=True)
    tm, chunks = y2.shape[0], y2.shape[1] // LANES
    for c in range(chunks):
        y_ref[pl.ds(c, tm, stride=chunks), :] = y2[:, c * LANES:(c + 1) * LANES]
        hn_ref[pl.ds(c, tm, stride=chunks), :] = h3[:, c * LANES:(c + 1) * LANES]


def _mix(x, att, hf, hb, mo, out_g, w_out, g_mem, w_q, mq_g, mem_k, mem_v, w_mo, g_ffn, w_r, b_r, tm):
    B, S, D = x.shape
    M = mem_k.shape[1]
    width = ML_HEADS * ML_DH
    mw = MEM_HEADS * MEM_DH
    full = lambda *s: pl.BlockSpec(s, lambda b, i: (0,) * len(s))
    tok = lambda w: pl.BlockSpec((1, tm, w), lambda b, i: (b, i, 0))
    chunks = D // LANES
    tiled = pl.BlockSpec((tm * chunks, LANES), lambda b, i: (b * (S // tm) + i, 0))
    memspec = pl.BlockSpec((1, M, mw), lambda b, i: (b, 0, 0))
    tiled_shape = jax.ShapeDtypeStruct((B * S * chunks, LANES), F32)
    return pl.pallas_call(
        _mix_kernel,
        grid=(B, S // tm),
        in_specs=[tok(D), tok(width), tok(width), tok(width), tok(width), full(1, width), full(2 * width, D),
                  full(1, D), full(D, mw), full(1, MEM_DH), memspec, memspec, full(mw, D),
                  full(1, D), full(N_EXPERTS, D), full(N_EXPERTS, 1)],
        out_specs=[tiled, tiled, pl.BlockSpec((1, N_EXPERTS, tm), lambda b, i: (b, 0, i))],
        out_shape=[tiled_shape, tiled_shape, jax.ShapeDtypeStruct((B, N_EXPERTS, S), F32)],
        compiler_params=pltpu.CompilerParams(dimension_semantics=("arbitrary", "arbitrary"),
                                             vmem_limit_bytes=VMEM_LIMIT),
        name="mix_mem_router",
    )(x, att, hf, hb, mo, out_g, w_out, g_mem, w_q, mq_g, mem_k, mem_v, w_mo, g_ffn, w_r.T, b_r.T)


def _topc_kernel(aff2_ref, aff3_ref, idx_ref, gval_ref, *, cap, seq, row_pitch):
    a2 = aff2_ref[0]
    bits2 = pltpu.bitcast(a2, I32)
    capf = float(cap)

    def bisect(i, lo):
        cand = lo | jnp.left_shift(jnp.int32(1), 30 - i)
        cnt = jnp.sum((bits2 >= cand).astype(F32), axis=1, keepdims=True)
        return jnp.where(cnt >= capf, cand, lo)

    thr_all = lax.fori_loop(0, 31, bisect, jnp.zeros((N_EXPERTS, 1), I32))
    need_all = capf - jnp.sum((bits2 > thr_all).astype(F32), axis=1, keepdims=True)

    T = aff3_ref.shape[2]
    tri_u = (_iota((LANES, LANES), 0) <= _iota((LANES, LANES), 1)).astype(BF16)
    tri_l = (_iota((LANES, LANES), 1) <= _iota((LANES, LANES), 0)).astype(BF16)
    ones8 = jnp.ones((8, LANES), BF16)
    before = _iota((T, T), 1) < _iota((T, T), 0)
    kcol = _iota((T, 1), 0).astype(F32)
    j = _iota((1, cap), 1).astype(F32)
    eye = (_iota((LANES, LANES), 0) == _iota((LANES, LANES), 1)).astype(BF16)
    lane_pos = _iota((LANES, cap), 0).astype(F32)

    def tile_starts(maskb):
        tot_row = _dot_nt(ones8, maskb)[0:1, :]
        return jnp.sum(jnp.where(before, tot_row, 0.0), axis=1, keepdims=True)

    for e in range(N_EXPERTS):
        bits = pltpu.bitcast(aff3_ref[0, e], I32)
        thr = thr_all[e:e + 1, :]
        need = need_all[e:e + 1, :]
        gt = bits > thr
        eq = bits == thr
        eqb = eq.astype(BF16)
        eq_rank = _dot(eqb, tri_u) + tile_starts(eqb) - eq.astype(F32)
        sel = gt | (eq & (eq_rank < need))
        selb = sel.astype(BF16)
        cs = _dot(selb, tri_u)
        start = tile_starts(selb)
        end = start + cs[:, LANES - 1:LANES]
        onehot = (start <= j) & (j < end)
        ohf = onehot.astype(F32)
        tile_of = jnp.sum(ohf * kcol, axis=0, keepdims=True)
        j_loc = j - jnp.sum(ohf * start, axis=0, keepdims=True)
        cs_t = _dot_nt(tri_l, selb)
        ohb = onehot.astype(BF16)
        r_t = _dot(cs_t.astype(BF16), ohb)
        local = jnp.sum((r_t <= j_loc).astype(F32), axis=0, keepdims=True)
        idx_ref[0, e:e + 1, :] = ((tile_of * LANES + local).astype(I32) + pl.program_id(0) * seq) * row_pitch
        a_t = _dot01_nt(eye, aff3_ref[0, e])
        a_tile = sum(_dot(term, ohb) for term in _split3(a_t))
        gval_ref[0, e:e + 1, :] = jnp.sum(jnp.where(lane_pos == local, a_tile, 0.0), axis=0, keepdims=True)


def _topc(aff_t, cap, row_pitch):
    B, E, S = aff_t.shape
    T = S // LANES
    aff3 = aff_t.reshape(B, E, T, LANES)
    return pl.pallas_call(
        functools.partial(_topc_kernel, cap=cap, seq=S, row_pitch=row_pitch),
        grid=(B,),
        in_specs=[pl.BlockSpec((1, E, S), lambda b: (b, 0, 0)), pl.BlockSpec((1, E, T, LANES), lambda b: (b, 0, 0, 0))],
        out_specs=[pl.BlockSpec((1, E, cap), lambda b: (b, 0, 0))] * 2,
        out_shape=[jax.ShapeDtypeStruct((B, E, cap), I32), jax.ShapeDtypeStruct((B, E, cap), F32)],
        compiler_params=pltpu.CompilerParams(dimension_semantics=("arbitrary",), vmem_limit_bytes=VMEM_LIMIT),
        name="topc",
    )(aff_t, aff3)


FFN_ROWS = 256
FFN_COLS = 256


def _ffn_kernel(idxp_ref, idxn_ref, gv_ref, hn_hbm, acc_in_hbm, wg_ref, wu_ref, wd_ref, acc_hbm,
                xe_ref, ab_ref, sems, *, nb, cap, d_model, ff):
    del acc_in_hbm
    s = pl.program_id(0)
    last = pl.num_programs(0) - 1
    slot = s % 2
    a_cur = s % 3
    a_nxt = (s + 1) % 3
    a_prv = (s + 2) % 3
    chunks = d_model // LANES

    def hbm_tile(ref, r):
        return ref.at[pl.ds(pl.multiple_of(r, chunks), chunks)]

    def vmem_tile(ref, sl, j):
        return ref.at[sl, pl.ds(pl.multiple_of(j * chunks, chunks), chunks)]

    def gather_x(r, j, sl):
        return pltpu.make_async_copy(hbm_tile(hn_hbm, r), vmem_tile(xe_ref, sl, j), sems.at[0])

    def gather_a(r, j, sl):
        return pltpu.make_async_copy(hbm_tile(acc_hbm, r), vmem_tile(ab_ref, sl, j), sems.at[1])

    def scatter(r, j, sl):
        return pltpu.make_async_copy(vmem_tile(ab_ref, sl, j), hbm_tile(acc_hbm, r), sems.at[2])

    def wait_x(sl):
        pltpu.make_async_copy(hn_hbm.at[pl.ds(0, cap * chunks)], xe_ref.at[sl], sems.at[0]).wait()

    def wait_a(sl):
        pltpu.make_async_copy(acc_hbm.at[pl.ds(0, cap * chunks)], ab_ref.at[sl], sems.at[1]).wait()

    def wait_scatter(sl):
        pltpu.make_async_copy(ab_ref.at[sl], acc_hbm.at[pl.ds(0, cap * chunks)], sems.at[2]).wait()

    @pl.when(s == 0)
    def _():
        @pl.loop(0, cap)
        def _(j):
            r = idxp_ref[0, 0, j]
            gather_x(r, j, 0).start()
            gather_a(r, j, 0).start()
            gather_a(r, j, 2).start()
        wait_a(2)

    wait_x(slot)
    wait_a(a_cur)

    n_dots = 2 * (ff // FFN_COLS) + d_model // FFN_COLS
    per_dot = -(-FFN_ROWS // n_dots)
    halves = FFN_COLS // LANES

    @pl.loop(0, cap // FFN_ROWS)
    def _(rb):
        r0 = pl.multiple_of(rb * FFN_ROWS, FFN_ROWS)
        issued = [0]

        def issue_some():
            lo, hi = issued[0], min(issued[0] + per_dot, FFN_ROWS)
            issued[0] = hi
            for jj in range(lo, hi):
                j = r0 + jj
                rn = idxn_ref[0, 0, j]
                gather_x(rn, j, 1 - slot).start()
                gather_a(rn, j, a_nxt).start()
                scatter(idxp_ref[0, 0, j], j, a_prv).start()

        chunk_rows = lambda c: pl.ds(r0 * chunks + c, FFN_ROWS, stride=chunks)
        xb = jnp.concatenate([xe_ref[slot, chunk_rows(c), :] for c in range(chunks)], axis=1).astype(BF16)
        g_row = gv_ref[0, rb]
        gval = jnp.concatenate(
            [jnp.broadcast_to(g_row[:, i * LANES:(i + 1) * LANES], (LANES, LANES)).T for i in range(FFN_ROWS // LANES)],
            axis=0)
        hid = []
        for ct in range(ff // FFN_COLS):
            cs = slice(ct * FFN_COLS, (ct + 1) * FFN_COLS)
            hg = _dot(xb, wg_ref[0, :, cs])
            issue_some()
            hu = _dot(xb, wu_ref[0, :, cs])
            issue_some()
            hid.append((hg * _sigmoid(hg) * hu).astype(BF16))
        hid = jnp.concatenate(hid, axis=1)
        for ct in range(d_model // FFN_COLS):
            ye = _dot(hid, wd_ref[0, :, ct * FFN_COLS:(ct + 1) * FFN_COLS])
            for i in range(halves):
                ab_ref[a_cur, chunk_rows(ct * halves + i), :] += ye[:, i * LANES:(i + 1) * LANES] * gval
            issue_some()
        assert issued[0] == FFN_ROWS

    wait_scatter(a_prv)

    @pl.when(s == last)
    def _():
        wait_x(1 - slot)
        wait_a(a_nxt)

        @pl.loop(0, cap)
        def _(j):
            scatter(idxn_ref[0, 0, j], j, a_cur).start()
        wait_scatter(a_cur)


def _expert_ffn(idx, gval, hn3, acc3, wg, wu, wd):
    B, E, cap = idx.shape
    d_model, ff = wg.shape[1], wg.shape[2]
    chunks = d_model // LANES
    assert B >= 3 and cap % FFN_ROWS == 0 and ff % FFN_COLS == 0 and d_model % FFN_COLS == 0
    ns = E * B
    nblk = cap // FFN_ROWS
    idx3 = idx.reshape(B * E, 1, cap)
    gv4 = gval.reshape(B * E, nblk, 1, FFN_ROWS)
    any_spec = pl.BlockSpec(memory_space=pl.ANY)
    wspec = lambda r, c: pl.BlockSpec((1, r, c), lambda s: (s // B, 0, 0))
    blk = lambda s: (s % B) * E + s // B
    ispec = lambda f: pl.BlockSpec((1, 1, cap), lambda s: (blk(f(s)), 0, 0), memory_space=pltpu.SMEM)
    kern = functools.partial(_ffn_kernel, nb=B, cap=cap, d_model=d_model, ff=ff)
    return pl.pallas_call(
        kern,
        grid=(ns,),
        in_specs=[ispec(lambda s: jnp.maximum(s - 1, 0)), ispec(lambda s: jnp.minimum(s + 1, ns - 1)),
                  pl.BlockSpec((1, nblk, 1, FFN_ROWS), lambda s: (blk(s), 0, 0, 0)),
                  any_spec, any_spec, wspec(d_model, ff), wspec(d_model, ff), wspec(ff, d_model)],
        out_specs=any_spec,
        out_shape=jax.ShapeDtypeStruct(acc3.shape, F32),
        scratch_shapes=[pltpu.VMEM((2, cap * chunks, LANES), F32), pltpu.VMEM((3, cap * chunks, LANES), F32),
                        pltpu.SemaphoreType.DMA((3,))],
        input_output_aliases={4: 0},
        compiler_params=pltpu.CompilerParams(dimension_semantics=("arbitrary",), vmem_limit_bytes=VMEM_LIMIT),
        name="expert_ffn",
    )(idx3, idx3, gv4, hn3, acc3, wg, wu, wd)


def _untile_kernel(a_ref, o_ref):
    tm, chunks = o_ref.shape[0], o_ref.shape[1] // LANES
    for c in range(chunks):
        o_ref[:, c * LANES:(c + 1) * LANES] = a_ref[pl.ds(c, tm, stride=chunks), :]


def _untile(a3, chunks, tm):
    N = a3.shape[0] // chunks
    return pl.pallas_call(
        _untile_kernel,
        grid=(N // tm,),
        in_specs=[pl.BlockSpec((tm * chunks, LANES), lambda i: (i, 0))],
        out_specs=pl.BlockSpec((tm, chunks * LANES), lambda i: (i, 0)),
        out_shape=jax.ShapeDtypeStruct((N, chunks * LANES), F32),
        compiler_params=pltpu.CompilerParams(dimension_semantics=("arbitrary",), vmem_limit_bytes=VMEM_LIMIT),
        name="untile",
    )(a3)


def _layer(x, mem, norm_mix_g, w_in, att_q_norm_g, att_k_norm_g, att_sink, ml_conv_w, ml_conv_b,
           ml_gate_b, ml_out_norm_g, w_out, norm_mem_g, mem_kv_norm_g, w_mem_q, w_mem_kv,
           mem_q_norm_g, mem_k_norm_g, w_mem_o, norm_ffn_g, w_router, b_router,
           w_exp_gate, w_exp_up, w_exp_down):
    B, S, D = x.shape
    row = lambda v: v.reshape(1, -1).astype(F32)
    tm_in = min(512, S)
    tm_mix = min(256, S)
    cap = CAPACITY_FACTOR * S // N_EXPERTS

    mem_k, mem_v = _mem_kv(mem, row(mem_kv_norm_g), w_mem_kv.astype(BF16), row(mem_k_norm_g))

    w_main = w_in[:, :_GATES[0]].astype(BF16)
    wg_t = w_in[:, _GATES[0]:_GATES[1]].T.astype(BF16)
    q_g = jnp.tile(row(att_q_norm_g), (1, ATT_HEADS)) * (ATT_DH ** -0.5)
    k_g = jnp.tile(row(att_k_norm_g), (1, ATT_KV))
    aq, ak, av, mqk, mv, mo, gates_t = _inproj(x, row(norm_mix_g), w_main, wg_t, ml_gate_b.reshape(-1, 1).astype(F32),
                                               q_g, k_g, tm_in)

    att = _attention(att_sink.astype(F32), aq, ak, av)
    hf, hb = _mlstm(mqk, mv, gates_t, ml_conv_w.astype(F32), row(ml_conv_b))

    y3, hn3, aff_t = _mix(x, att, hf, hb, mo, row(ml_out_norm_g), w_out.astype(BF16), row(norm_mem_g),
                             w_mem_q.astype(BF16), row(mem_q_norm_g), mem_k, mem_v, w_mem_o.astype(BF16),
                             row(norm_ffn_g), w_router.astype(F32), row(b_router), tm_mix)

    idx, gval = _topc(aff_t, cap, D // LANES)
    out3 = _expert_ffn(idx, gval, hn3, y3, w_exp_gate.astype(BF16), w_exp_up.astype(BF16), w_exp_down.astype(BF16))
    return _untile(out3, D // LANES, tm_in).reshape(B, S, D)


def kernel(x, mem, norm_mix_g, w_in, att_q_norm_g, att_k_norm_g, att_sink, ml_conv_w, ml_conv_b, ml_gate_b,
           ml_out_norm_g, w_out, norm_mem_g, mem_kv_norm_g, w_mem_q, w_mem_kv, mem_q_norm_g, mem_k_norm_g,
           w_mem_o, norm_ffn_g, w_router, b_router, w_exp_gate, w_exp_up, w_exp_down):
    params = (norm_mix_g, w_in, att_q_norm_g, att_k_norm_g, att_sink, ml_conv_w, ml_conv_b, ml_gate_b,
              ml_out_norm_g, w_out, norm_mem_g, mem_kv_norm_g, w_mem_q, w_mem_kv, mem_q_norm_g, mem_k_norm_g,
              w_mem_o, norm_ffn_g, w_router, b_router, w_exp_gate, w_exp_up, w_exp_down)
    depth = norm_mix_g.shape[0]
    for l in range(depth):
        x = _layer(x, mem, *[p[l] for p in params])
    return x
```

```python
import functools

import jax
import jax.numpy as jnp
from jax import lax
from jax.experimental import pallas as pl
from jax.experimental.pallas import tpu as pltpu

F32 = jnp.float32
BF16 = jnp.bfloat16
I32 = jnp.int32

EPS = 1e-6
LANES = 128
BLK = 128
ATT_HEADS, ATT_KV, ATT_DH = 8, 2, 64
ML_HEADS, ML_DH = 4, 128
MEM_HEADS, MEM_DH = 4, 128
N_EXPERTS = 16
CAPACITY_FACTOR = 2
NEG = -1e30
HALO = 16
VMEM_LIMIT = 48 * 1024 * 1024

_NT = (((1,), (1,)), ((), ()))


def _dot(a, b):
    return jnp.dot(a, b, preferred_element_type=F32)


def _dot_nt(a, b):
    return lax.dot_general(a, b, _NT, preferred_element_type=F32)


def _split3(x):
    hi = x.astype(BF16)
    r1 = x - hi.astype(F32)
    mid = r1.astype(BF16)
    lo = (r1 - mid.astype(F32)).astype(BF16)
    return hi, mid, lo


def _dot01_nt(m01, x):
    hi, mid, lo = _split3(x)
    return _dot_nt(m01, hi) + _dot_nt(m01, mid) + _dot_nt(m01, lo)


def _dot01(x, m01):
    hi, mid, lo = _split3(x)
    return _dot(hi, m01) + _dot(mid, m01) + _dot(lo, m01)


def _rms(x, g):
    ms = jnp.mean(x * x, axis=-1, keepdims=True)
    return x * lax.rsqrt(ms + EPS) * g


def _sigmoid(x):
    return 1.0 / (1.0 + jnp.exp(-x))


def _iota(shape, dim):
    return lax.broadcasted_iota(I32, shape, dim)


def _mem_kv_kernel(mem_ref, g_ref, w_ref, kg_ref, k_ref, v_ref):
    mn = _rms(mem_ref[0], g_ref[...]).astype(BF16)
    kv = _dot(mn, w_ref[...])
    width = MEM_HEADS * MEM_DH
    for h in range(MEM_HEADS):
        sl = slice(h * MEM_DH, (h + 1) * MEM_DH)
        k_ref[0, :, sl] = _rms(kv[:, sl], kg_ref[...]).astype(BF16)
    v_ref[0] = kv[:, width:].astype(BF16)


def _mem_kv(mem, g, w_kv, k_g):
    B, M, D = mem.shape
    width = MEM_HEADS * MEM_DH
    full = lambda *s: pl.BlockSpec(s, lambda b: (0,) * len(s))
    return pl.pallas_call(
        _mem_kv_kernel,
        grid=(B,),
        in_specs=[pl.BlockSpec((1, M, D), lambda b: (b, 0, 0)), full(1, D), full(D, 2 * width), full(1, MEM_DH)],
        out_specs=[pl.BlockSpec((1, M, width), lambda b: (b, 0, 0))] * 2,
        out_shape=[jax.ShapeDtypeStruct((B, M, width), BF16)] * 2,
        compiler_params=pltpu.CompilerParams(dimension_semantics=("arbitrary",), vmem_limit_bytes=VMEM_LIMIT),
        name="mem_kv",
    )(mem, g, w_kv, k_g)


_AQ = (0, 512)
_AK = (512, 640)
_AV = (640, 768)
_MQK = (768, 1792)
_MV = (1792, 2304)
_MO = (2304, 2816)
_GATES = (2816, 2832)


GC_F, GC_GMAX, GC_EEND, GC_MLOC = 0, 1, 2, 3


def _inproj_kernel(x_ref, g_ref, w_ref, wgt_ref, gb_ref, qg_ref, kg_ref, bdq_ref, bdk_ref, trif_ref, trib_ref,
                   blk_ref, aq_ref, ak_ref, av_ref, mqk_ref, mv_ref, mo_ref, grow_ref, gcol_ref):
    h = _rms(x_ref[0], g_ref[...]).astype(BF16)
    gt = _dot_nt(wgt_ref[...], h) + gb_ref[...]
    logsig = jnp.minimum(gt, 0.0) - jnp.log(1.0 + jnp.exp(-jnp.abs(gt)))
    tm = gt.shape[1]
    H = ML_HEADS
    pos = _iota((H, tm), 1) % BLK

    def chunk_scan(li, lf, tri_ref, fwd):
        f = _dot01(lf, tri_ref[...])
        g = li - f
        gmax = g
        sh = 1
        while sh < BLK:
            if fwd:
                gmax = jnp.where(pos >= sh, jnp.maximum(gmax, pltpu.roll(gmax, sh, axis=1)), gmax)
            else:
                gmax = jnp.where(pos < BLK - sh, jnp.maximum(gmax, pltpu.roll(gmax, tm - sh, axis=1)), gmax)
            sh *= 2
        w_end = _dot01(lf, blk_ref[...]) + g
        m_loc = jnp.concatenate(
            [jnp.broadcast_to(jnp.max(w_end[:, c0:c0 + BLK], axis=1, keepdims=True), (H, BLK))
             for c0 in range(0, tm, BLK)], axis=1)
        return f, g, gmax, jnp.exp(w_end - m_loc), m_loc

    f_f, g_f, gm_f, ee_f, ml_f = chunk_scan(gt[0:H], logsig[H:2 * H], trif_ref, True)
    f_b, g_b, gm_b, ee_b, ml_b = chunk_scan(gt[2 * H:3 * H], logsig[3 * H:4 * H], trib_ref, False)
    grow_ref[0] = jnp.concatenate([g_f, g_b], axis=0)
    kinds = [None] * 4
    kinds[GC_F], kinds[GC_GMAX], kinds[GC_EEND], kinds[GC_MLOC] = (f_f, f_b), (gm_f, gm_b), (ee_f, ee_b), (ml_f, ml_b)
    rows = jnp.concatenate([r for pair in kinds for r in pair], axis=0)
    pad = jnp.zeros((LANES - rows.shape[0], BLK), F32)
    for c0 in range(0, tm, BLK):
        gcol_ref[0, c0:c0 + BLK, :] = jnp.concatenate([rows[:, c0:c0 + BLK], pad], axis=0).T
    sec = lambda s: _dot(h, w_ref[:, s[0]:s[1]])
    aq = sec(_AQ)
    ssq = _dot((aq * aq).astype(BF16), bdq_ref[...])
    aq_ref[0] = (aq * lax.rsqrt(ssq * (1.0 / ATT_DH) + EPS) * qg_ref[...]).astype(BF16)
    ak = sec(_AK)
    ssk = _dot((ak * ak).astype(BF16), bdk_ref[...])
    ak_ref[0] = (ak * lax.rsqrt(ssk * (1.0 / ATT_DH) + EPS) * kg_ref[...]).astype(BF16)
    av_ref[0] = sec(_AV).astype(BF16)
    mqk_ref[0] = sec(_MQK).astype(BF16)
    mv_ref[0] = sec(_MV).astype(BF16)
    mo_ref[0] = sec(_MO).astype(BF16)


def _inproj(x, g, w_main, wg_t, gate_b, q_g, k_g, tm):
    B, S, D = x.shape
    bdq = (jnp.arange(512)[:, None] // ATT_DH == jnp.arange(512)[None, :] // ATT_DH).astype(BF16)
    bdk = bdq[:128, :128]
    s_from, s_to = jnp.arange(tm)[:, None], jnp.arange(tm)[None, :]
    same = s_from // BLK == s_to // BLK
    tri_f = (same & (s_from <= s_to)).astype(BF16)
    tri_b = (same & (s_from >= s_to)).astype(BF16)
    full = lambda *s: pl.BlockSpec(s, lambda b, i: (0,) * len(s))
    tok = lambda w: pl.BlockSpec((1, tm, w), lambda b, i: (b, i, 0))
    widths = (512, 128, 128, 1024, 512, 512)
    return pl.pallas_call(
        _inproj_kernel,
        grid=(B, S // tm),
        in_specs=[tok(D), full(1, D), full(D, w_main.shape[1]), full(16, D), full(16, 1),
                  full(1, 512), full(1, 128), full(512, 512), full(128, 128), full(tm, tm), full(tm, tm), full(tm, tm)],
        out_specs=[tok(w) for w in widths] + [pl.BlockSpec((1, 2 * ML_HEADS, tm), lambda b, i: (b, 0, i)), tok(LANES)],
        out_shape=[jax.ShapeDtypeStruct((B, S, w), BF16) for w in widths]
        + [jax.ShapeDtypeStruct((B, 2 * ML_HEADS, S), F32), jax.ShapeDtypeStruct((B, S, LANES), F32)],
        compiler_params=pltpu.CompilerParams(dimension_semantics=("arbitrary", "arbitrary"),
                                             vmem_limit_bytes=VMEM_LIMIT),
        name="inproj",
    )(x, g, w_main, wg_t, gate_b, q_g, k_g, bdq, bdk, tri_f, tri_b, same.astype(BF16))


def _attn_kernel(sink_ref, q_ref, kp_ref, kc_ref, kn_ref, vp_ref, vc_ref, vn_ref, o_ref):
    n = pl.program_id(1)
    nb = pl.num_programs(1)
    shape = (BLK, 3 * BLK)
    qi = _iota(shape, 0)
    si = _iota(shape, 1)
    adist = jnp.abs(qi + BLK - si)
    valid = adist <= BLK
    valid = valid & ((si >= BLK) | (n > 0))
    valid = valid & ((si < 2 * BLK) | (n < nb - 1))
    adist = adist.astype(F32)
    q = q_ref[0]
    outs = []
    for kv in range(ATT_KV):
        sl = slice(kv * ATT_DH, (kv + 1) * ATT_DH)
        kband = jnp.concatenate([kp_ref[0][:, sl], kc_ref[0][:, sl], kn_ref[0][:, sl]], axis=0)
        vband = jnp.concatenate([vp_ref[0][:, sl], vc_ref[0][:, sl], vn_ref[0][:, sl]], axis=0)
        for g in range(ATT_HEADS // ATT_KV):
            h = kv * (ATT_HEADS // ATT_KV) + g
            qh = q[:, h * ATT_DH:(h + 1) * ATT_DH]
            s = _dot_nt(qh, kband)
            slope = 2.0 ** (-8.0 * (h + 1) / ATT_HEADS)
            s = jnp.where(valid, s - slope * adist, NEG)
            sink = sink_ref[h]
            m = jnp.maximum(jnp.max(s, axis=-1, keepdims=True), sink)
            p = jnp.exp(s - m)
            denom = jnp.sum(p, axis=-1, keepdims=True) + jnp.exp(sink - m)
            outs.append(_dot(p.astype(BF16), vband) / denom)
    o_ref[0] = jnp.concatenate(outs, axis=1).astype(BF16)


def _attention(sink, aq, ak, av):
    B, S, _ = aq.shape
    nb = S // BLK
    kvw = ATT_KV * ATT_DH
    prev = pl.BlockSpec((1, BLK, kvw), lambda b, n, *_: (b, jnp.maximum(n - 1, 0), 0))
    cur = pl.BlockSpec((1, BLK, kvw), lambda b, n, *_: (b, n, 0))
    nxt = pl.BlockSpec((1, BLK, kvw), lambda b, n, *_: (b, jnp.minimum(n + 1, nb - 1), 0))
    qspec = pl.BlockSpec((1, BLK, ATT_HEADS * ATT_DH), lambda b, n, *_: (b, n, 0))
    return pl.pallas_call(
        _attn_kernel,
        grid_spec=pltpu.PrefetchScalarGridSpec(
            num_scalar_prefetch=1, grid=(B, nb),
            in_specs=[qspec, prev, cur, nxt, prev, cur, nxt], out_specs=qspec),
        out_shape=jax.ShapeDtypeStruct(aq.shape, BF16),
        compiler_params=pltpu.CompilerParams(dimension_semantics=("arbitrary", "arbitrary"),
                                             vmem_limit_bytes=VMEM_LIMIT),
        name="win_attn",
    )(sink, aq, ak, ak, ak, av, av, av)


def _mlstm_kernel(xf_ref, xfp_ref, xfn_ref, xb_ref, xbp_ref, xbn_ref, vf_ref, vb_ref, grf_ref, grb_ref,
                  gcf_ref, gcb_ref, cw_ref, cb_ref, hf_ref, hb_ref, cn_ref, m_ref):
    c = pl.program_id(1)
    nc = pl.num_programs(1)
    L = BLK
    width = ML_HEADS * ML_DH

    @pl.when(c == 0)
    def _():
        cn_ref[...] = jnp.zeros_like(cn_ref)
        m_ref[...] = jnp.zeros_like(m_ref)

    row = _iota((L, 1), 0)
    tt = _iota((L, L), 0)
    ss = _iota((L, L), 1)
    ones_v = jnp.ones((L, ML_DH), BF16)

    def conv_silu(x_ref, p_ref, nx_ref, first, last):
        x = x_ref[0].astype(F32)
        prow = jnp.where(first, 0.0, p_ref[0, HALO - 1:HALO, :].astype(F32))
        nrow = jnp.where(last, 0.0, nx_ref[0, 0:1, :].astype(F32))
        xp = jnp.where(row == 0, prow, pltpu.roll(x, 1, axis=0))
        xn = jnp.where(row == L - 1, nrow, pltpu.roll(x, L - 1, axis=0))
        y = cw_ref[0:1, :] * xp + cw_ref[1:2, :] * x + cw_ref[2:3, :] * xn + cb_ref[...]
        return y * _sigmoid(y)

    def direction(d, qk, v_ref, grow_ref, gcol_ref, out_ref):
        H = ML_HEADS
        keep = (ss <= tt) if d == 0 else (ss >= tt)
        g_row = grow_ref[0][H * d:H * (d + 1), :]
        gc = gcol_ref[0]
        kind = lambda i: gc[:, 2 * H * i + H * d:2 * H * i + H * (d + 1)]
        f, g_max, e_end, m_loc = kind(GC_F), kind(GC_GMAX), kind(GC_EEND), kind(GC_MLOC)[0:1, :]
        end = L - 1 if d == 0 else 0
        f_end = f[end:end + 1, :]
        m0 = m_ref[d, 0:1, 0:H]
        mm = jnp.maximum(m0, g_max)
        e_inter = jnp.exp(m0 - mm)
        floor = jnp.exp(-(f + mm))
        m_new = jnp.maximum(f_end + m0, m_loc)
        ca = jnp.exp(f_end + m0 - m_new)
        cb = jnp.exp(m_loc - m_new)
        m_ref[d, 0:1, 0:H] = m_new
        col = lambda a, h, w=ML_DH: jnp.broadcast_to(a[:, h:h + 1], (L, w))
        v_all = v_ref[0]
        for h in range(H):
            u = H * d + h
            hs = slice(h * ML_DH, (h + 1) * ML_DH)
            q = qk[:, hs]
            k = qk[:, width + h * ML_DH: width + (h + 1) * ML_DH] * (ML_DH ** -0.5)
            v1 = jnp.concatenate([v_all[:, hs], ones_v], axis=1)
            qb = q.astype(BF16)
            kb = k.astype(BF16)
            cn0 = cn_ref[u]
            dec = jnp.where(keep, jnp.exp(g_row[h:h + 1, :] - col(mm, h)), 0.0)
            s_qk = (_dot_nt(qb, kb) * dec).astype(BF16)
            tot = col(e_inter, h, 2 * ML_DH) * _dot(qb, cn0.astype(BF16)) + _dot(s_qk, v1)
            out_ref[0, :, hs] = tot[:, :ML_DH] / jnp.maximum(jnp.abs(tot[:, ML_DH:]), col(floor, h))
            ks = k * col(e_end, h)
            cn_ref[u] = ca[:, h:h + 1] * cn0 + cb[:, h:h + 1] * _dot(ks.T.astype(BF16), v1)

    qk_f = conv_silu(xf_ref, xfp_ref, xfn_ref, c == 0, c == nc - 1)
    direction(0, qk_f, vf_ref, grf_ref, gcf_ref, hf_ref)
    qk_b = conv_silu(xb_ref, xbp_ref, xbn_ref, c == nc - 1, c == 0)
    direction(1, qk_b, vb_ref, grb_ref, gcb_ref, hb_ref)


def _mlstm(mqk, mv, g_rows, g_cols, conv_w, conv_b):
    B, S, _ = mqk.shape
    nc = S // BLK
    width = ML_HEADS * ML_DH
    per = BLK // HALO
    nh = S // HALO
    fwd = lambda c: c
    bwd = lambda c: nc - 1 - c

    def specs(ci):
        own = pl.BlockSpec((1, BLK, 2 * width), lambda b, c: (b, ci(c), 0))
        prev = pl.BlockSpec((1, HALO, 2 * width), lambda b, c: (b, jnp.maximum(ci(c) * per - 1, 0), 0))
        nxt = pl.BlockSpec((1, HALO, 2 * width), lambda b, c: (b, jnp.minimum((ci(c) + 1) * per, nh - 1), 0))
        return [own, prev, nxt]

    vspec = lambda ci: pl.BlockSpec((1, BLK, width), lambda b, c: (b, ci(c), 0))
    gspec = lambda ci: pl.BlockSpec((1, 2 * ML_HEADS, BLK), lambda b, c: (b, 0, ci(c)))
    cspec = lambda ci: pl.BlockSpec((1, BLK, LANES), lambda b, c: (b, ci(c), 0))
    full = lambda *s: pl.BlockSpec(s, lambda b, c: (0,) * len(s))
    units = 2 * ML_HEADS
    return pl.pallas_call(
        _mlstm_kernel,
        grid=(B, nc),
        in_specs=specs(fwd) + specs(bwd) + [vspec(fwd), vspec(bwd), gspec(fwd), gspec(bwd), cspec(fwd), cspec(bwd),
                                            full(3, 2 * width), full(1, 2 * width)],
        out_specs=[vspec(fwd), vspec(bwd)],
        out_shape=[jax.ShapeDtypeStruct((B, S, width), F32)] * 2,
        scratch_shapes=[pltpu.VMEM((units, ML_DH, 2 * ML_DH), F32), pltpu.VMEM((2, 8, LANES), F32)],
        compiler_params=pltpu.CompilerParams(dimension_semantics=("arbitrary", "arbitrary"),
                                             vmem_limit_bytes=VMEM_LIMIT),
        name="mlstm",
    )(mqk, mqk, mqk, mqk, mqk, mqk, mv, mv, g_rows, g_rows, g_cols, g_cols, conv_w, conv_b)


def _mix_kernel(x_ref, att_ref, hf_ref, hb_ref, mo_ref, og_ref, wo_ref,
                gm_ref, wq_ref, mqg_ref, k_ref, v_ref, wmo_ref,
                gf_ref, wrt_ref, brt_ref,
                y_ref, hn_ref, afft_ref):
    x = x_ref[0]
    width = ML_HEADS * ML_DH
    ml = hf_ref[0] + hb_ref[0]
    mo = mo_ref[0].astype(F32)
    parts = []
    for h in range(ML_HEADS):
        sl = slice(h * ML_DH, (h + 1) * ML_DH)
        parts.append((_sigmoid(mo[:, sl]) * _rms(ml[:, sl], og_ref[:, sl])).astype(BF16))
    ml_out = jnp.concatenate(parts, axis=1)
    y1 = x + _dot(att_ref[0], wo_ref[0:width, :]) + _dot(ml_out, wo_ref[width:2 * width, :])
    h2 = _rms(y1, gm_ref[...]).astype(BF16)
    qm = _dot(h2, wq_ref[...])
    k = k_ref[0]
    v = v_ref[0]
    outs = []
    for h in range(MEM_HEADS):
        sl = slice(h * MEM_DH, (h + 1) * MEM_DH)
        qh = (_rms(qm[:, sl], mqg_ref[...]) * (MEM_DH ** -0.5)).astype(BF16)
        s = _dot_nt(qh, k[:, sl])
        p = jnp.exp(s - jnp.max(s, axis=-1, keepdims=True))
        o = _dot(p.astype(BF16), v[:, sl]) / jnp.sum(p, axis=-1, keepdims=True)
        outs.append(o.astype(BF16))
    y2 = y1 + _dot(jnp.concatenate(outs, axis=1), wmo_ref[...])
    h3 = _rms(y2, gf_ref[...])
    h_hi = h3.astype(BF16)
    h_lo = (h3 - h_hi.astype(F32)).astype(BF16)
    wrt = wrt_ref[...]
    wt_hi = wrt.astype(BF16)
    wt_lo = (wrt - wt_hi.astype(F32)).astype(BF16)
    logits_t = _dot_nt(wt_hi, h_hi) + _dot_nt(wt_hi, h_lo) + _dot_nt(wt_lo, h_hi) + brt_ref[...]
    pt = jnp.exp(logits_t - jnp.max(logits_t, axis=0, keepdims=True))
    afft_ref[0] = pt / jnp.sum(pt, axis=0, keepdims=True)
    tm, chunks = y2.shape[0], y2.shape[1] // LANES
    for c in range(chunks):
        y_ref[pl.ds(c, tm, stride=chunks), :] = y2[:, c * LANES:(c + 1) * LANES]
        hn_ref[pl.ds(c, tm, stride=chunks), :] = h3[:, c * LANES:(c + 1) * LANES]


def _mix(x, att, hf, hb, mo, out_g, w_out, g_mem, w_q, mq_g, mem_k, mem_v, w_mo, g_ffn, w_r, b_r, tm):
    B, S, D = x.shape
    M = mem_k.shape[1]
    width = ML_HEADS * ML_DH
    mw = MEM_HEADS * MEM_DH
    full = lambda *s: pl.BlockSpec(s, lambda b, i: (0,) * len(s))
    tok = lambda w: pl.BlockSpec((1, tm, w), lambda b, i: (b, i, 0))
    chunks = D // LANES
    tiled = pl.BlockSpec((tm * chunks, LANES), lambda b, i: (b * (S // tm) + i, 0))
    memspec = pl.BlockSpec((1, M, mw), lambda b, i: (b, 0, 0))
    tiled_shape = jax.ShapeDtypeStruct((B * S * chunks, LANES), F32)
    return pl.pallas_call(
        _mix_kernel,
        grid=(B, S // tm),
        in_specs=[tok(D), tok(width), tok(width), tok(width), tok(width), full(1, width), full(2 * width, D),
                  full(1, D), full(D, mw), full(1, MEM_DH), memspec, memspec, full(mw, D),
                  full(1, D), full(N_EXPERTS, D), full(N_EXPERTS, 1)],
        out_specs=[tiled, tiled, pl.BlockSpec((1, N_EXPERTS, tm), lambda b, i: (b, 0, i))],
        out_shape=[tiled_shape, tiled_shape, jax.ShapeDtypeStruct((B, N_EXPERTS, S), F32)],
        compiler_params=pltpu.CompilerParams(dimension_semantics=("arbitrary", "arbitrary"),
                                             vmem_limit_bytes=VMEM_LIMIT),
        name="mix_mem_router",
    )(x, att, hf, hb, mo, out_g, w_out, g_mem, w_q, mq_g, mem_k, mem_v, w_mo, g_ffn, w_r.T, b_r.T)


def _topc_kernel(aff2_ref, aff3_ref, idx_ref, gval_ref, *, cap, seq, row_pitch):
    a2 = aff2_ref[0]
    bits2 = pltpu.bitcast(a2, I32)
    capf = float(cap)

    def bisect(i, lo):
        cand = lo | jnp.left_shift(jnp.int32(1), 30 - i)
        cnt = jnp.sum((bits2 >= cand).astype(F32), axis=1, keepdims=True)
        return jnp.where(cnt >= capf, cand, lo)

    thr_all = lax.fori_loop(0, 31, bisect, jnp.zeros((N_EXPERTS, 1), I32))
    need_all = capf - jnp.sum((bits2 > thr_all).astype(F32), axis=1, keepdims=True)

    T = aff3_ref.shape[2]
    tri_u = (_iota((LANES, LANES), 0) <= _iota((LANES, LANES), 1)).astype(BF16)
    tri_l = (_iota((LANES, LANES), 1) <= _iota((LANES, LANES), 0)).astype(BF16)
    ones8 = jnp.ones((8, LANES), BF16)
    before = _iota((T, T), 1) < _iota((T, T), 0)
    kcol = _iota((T, 1), 0).astype(F32)
    j = _iota((1, cap), 1).astype(F32)
    eye = (_iota((LANES, LANES), 0) == _iota((LANES, LANES), 1)).astype(BF16)
    lane_pos = _iota((LANES, cap), 0).astype(F32)

    def tile_starts(maskb):
        tot_row = _dot_nt(ones8, maskb)[0:1, :]
        return jnp.sum(jnp.where(before, tot_row, 0.0), axis=1, keepdims=True)

    for e in range(N_EXPERTS):
        bits = pltpu.bitcast(aff3_ref[0, e], I32)
        thr = thr_all[e:e + 1, :]
        need = need_all[e:e + 1, :]
        gt = bits > thr
        eq = bits == thr
        eqb = eq.astype(BF16)
        eq_rank = _dot(eqb, tri_u) + tile_starts(eqb) - eq.astype(F32)
        sel = gt | (eq & (eq_rank < need))
        selb = sel.astype(BF16)
        cs = _dot(selb, tri_u)
        start = tile_starts(selb)
        end = start + cs[:, LANES - 1:LANES]
        onehot = (start <= j) & (j < end)
        ohf = onehot.astype(F32)
        tile_of = jnp.sum(ohf * kcol, axis=0, keepdims=True)
        j_loc = j - jnp.sum(ohf * start, axis=0, keepdims=True)
        cs_t = _dot_nt(tri_l, selb)
        ohb = onehot.astype(BF16)
        r_t = _dot(cs_t.astype(BF16), ohb)
        local = jnp.sum((r_t <= j_loc).astype(F32), axis=0, keepdims=True)
        idx_ref[0, e:e + 1, :] = ((tile_of * LANES + local).astype(I32) + pl.program_id(0) * seq) * row_pitch
        a_t = _dot01_nt(eye, aff3_ref[0, e])
        a_tile = sum(_dot(term, ohb) for term in _split3(a_t))
        gval_ref[0, e:e + 1, :] = jnp.sum(jnp.where(lane_pos == local, a_tile, 0.0), axis=0, keepdims=True)


def _topc(aff_t, cap, row_pitch):
    B, E, S = aff_t.shape
    T = S // LANES
    aff3 = aff_t.reshape(B, E, T, LANES)
    return pl.pallas_call(
        functools.partial(_topc_kernel, cap=cap, seq=S, row_pitch=row_pitch),
        grid=(B,),
        in_specs=[pl.BlockSpec((1, E, S), lambda b: (b, 0, 0)), pl.BlockSpec((1, E, T, LANES), lambda b: (b, 0, 0, 0))],
        out_specs=[pl.BlockSpec((1, E, cap), lambda b: (b, 0, 0))] * 2,
        out_shape=[jax.ShapeDtypeStruct((B, E, cap), I32), jax.ShapeDtypeStruct((B, E, cap), F32)],
        compiler_params=pltpu.CompilerParams(dimension_semantics=("arbitrary",), vmem_limit_bytes=VMEM_LIMIT),
        name="topc",
    )(aff_t, aff3)


FFN_ROWS = 256
FFN_COLS = 256


def _ffn_kernel(idxp_ref, idxn_ref, gv_ref, hn_hbm, acc_in_hbm, wg_ref, wu_ref, wd_ref, acc_hbm,
                xe_ref, ab_ref, sems, *, nb, cap, d_model, ff):
    del acc_in_hbm
    s = pl.program_id(0)
    last = pl.num_programs(0) - 1
    slot = s % 2
    a_cur = s % 3
    a_nxt = (s + 1) % 3
    a_prv = (s + 2) % 3
    chunks = d_model // LANES

    def hbm_tile(ref, r):
        return ref.at[pl.ds(pl.multiple_of(r, chunks), chunks)]

    def vmem_tile(ref, sl, j):
        return ref.at[sl, pl.ds(pl.multiple_of(j * chunks, chunks), chunks)]

    def gather_x(r, j, sl):
        return pltpu.make_async_copy(hbm_tile(hn_hbm, r), vmem_tile(xe_ref, sl, j), sems.at[0])

    def gather_a(r, j, sl):
        return pltpu.make_async_copy(hbm_tile(acc_hbm, r), vmem_tile(ab_ref, sl, j), sems.at[1])

    def scatter(r, j, sl):
        return pltpu.make_async_copy(vmem_tile(ab_ref, sl, j), hbm_tile(acc_hbm, r), sems.at[2])

    def wait_x(sl):
        pltpu.make_async_copy(hn_hbm.at[pl.ds(0, cap * chunks)], xe_ref.at[sl], sems.at[0]).wait()

    def wait_a(sl):
        pltpu.make_async_copy(acc_hbm.at[pl.ds(0, cap * chunks)], ab_ref.at[sl], sems.at[1]).wait()

    def wait_scatter(sl):
        pltpu.make_async_copy(ab_ref.at[sl], acc_hbm.at[pl.ds(0, cap * chunks)], sems.at[2]).wait()

    @pl.when(s == 0)
    def _():
        @pl.loop(0, cap)
        def _(j):
            r = idxp_ref[0, 0, j]
            gather_x(r, j, 0).start()
            gather_a(r, j, 0).start()
            gather_a(r, j, 2).start()
        wait_a(2)

    wait_x(slot)
    wait_a(a_cur)

    halves = FFN_COLS // LANES
    nblk = cap // FFN_ROWS
    n_ct = ff // FFN_COLS
    assert d_model // FFN_COLS == n_ct

    n_updates = nblk * n_ct
    tail = n_ct
    tail_rows = 16 if nblk > 1 else -(-cap // tail)
    body_rows = -(-(cap - tail * tail_rows) // (n_updates - tail)) if nblk > 1 else 0
    next_row = [0]

    def issue_rows(n):
        lo, hi = next_row[0], min(next_row[0] + n, cap)
        next_row[0] = hi
        for j in range(lo, hi):
            rn = idxn_ref[0, 0, j]
            gather_x(rn, j, 1 - slot).start()
            gather_a(rn, j, a_nxt).start()
            scatter(idxp_ref[0, 0, j], j, a_prv).start()

    def chunk_rows(rb, c):
        return pl.ds(rb * FFN_ROWS * chunks + c, FFN_ROWS, stride=chunks)

    def gate_rows(rb):
        g_row = gv_ref[0, rb]
        return jnp.concatenate(
            [jnp.broadcast_to(g_row[:, i * LANES:(i + 1) * LANES], (LANES, LANES)).T for i in range(FFN_ROWS // LANES)],
            axis=0)

    hid_prev = None
    for stage in range(nblk + 1):
        up, dn = stage < nblk, stage >= 1
        if up:
            xb = jnp.concatenate([xe_ref[slot, chunk_rows(stage, c), :] for c in range(chunks)], axis=1).astype(BF16)
        if dn:
            gval = gate_rows(stage - 1)
        hid = []
        for ct in range(n_ct):
            cs = slice(ct * FFN_COLS, (ct + 1) * FFN_COLS)
            if up:
                hg = _dot(xb, wg_ref[0, :, cs])
                hu = _dot(xb, wu_ref[0, :, cs])
                hid.append((hg * _sigmoid(hg) * hu).astype(BF16))
            if dn:
                ye = _dot(hid_prev, wd_ref[0, :, cs])
                for i in range(halves):
                    ab_ref[a_cur, chunk_rows(stage - 1, ct * halves + i), :] += ye[:, i * LANES:(i + 1) * LANES] * gval
                issue_rows(body_rows if up else tail_rows)
        hid_prev = jnp.concatenate(hid, axis=1) if up else None
    issue_rows(cap)
    assert next_row[0] == cap

    wait_scatter(a_prv)

    @pl.when(s == last)
    def _():
        wait_x(1 - slot)
        wait_a(a_nxt)

        @pl.loop(0, cap)
        def _(j):
            scatter(idxn_ref[0, 0, j], j, a_cur).start()
        wait_scatter(a_cur)


def _expert_ffn(idx, gval, hn3, acc3, wg, wu, wd):
    B, E, cap = idx.shape
    d_model, ff = wg.shape[1], wg.shape[2]
    chunks = d_model // LANES
    assert B >= 3 and cap % FFN_ROWS == 0 and ff % FFN_COLS == 0 and d_model % FFN_COLS == 0
    ns = E * B
    nblk = cap // FFN_ROWS
    idx3 = idx.reshape(B * E, 1, cap)
    gv4 = gval.reshape(B * E, nblk, 1, FFN_ROWS)
    any_spec = pl.BlockSpec(memory_space=pl.ANY)
    wspec = lambda r, c: pl.BlockSpec((1, r, c), lambda s: (s // B, 0, 0))
    blk = lambda s: (s % B) * E + s // B
    ispec = lambda f: pl.BlockSpec((1, 1, cap), lambda s: (blk(f(s)), 0, 0), memory_space=pltpu.SMEM)
    kern = functools.partial(_ffn_kernel, nb=B, cap=cap, d_model=d_model, ff=ff)
    return pl.pallas_call(
        kern,
        grid=(ns,),
        in_specs=[ispec(lambda s: jnp.maximum(s - 1, 0)), ispec(lambda s: jnp.minimum(s + 1, ns - 1)),
                  pl.BlockSpec((1, nblk, 1, FFN_ROWS), lambda s: (blk(s), 0, 0, 0)),
                  any_spec, any_spec, wspec(d_model, ff), wspec(d_model, ff), wspec(ff, d_model)],
        out_specs=any_spec,
        out_shape=jax.ShapeDtypeStruct(acc3.shape, F32),
        scratch_shapes=[pltpu.VMEM((2, cap * chunks, LANES), F32), pltpu.VMEM((3, cap * chunks, LANES), F32),
                        pltpu.SemaphoreType.DMA((3,))],
        input_output_aliases={4: 0},
        compiler_params=pltpu.CompilerParams(dimension_semantics=("arbitrary",), vmem_limit_bytes=VMEM_LIMIT),
        name="expert_ffn",
    )(idx3, idx3, gv4, hn3, acc3, wg, wu, wd)


def _untile_kernel(a_ref, o_ref):
    tm, chunks = o_ref.shape[0], o_ref.shape[1] // LANES
    for c in range(chunks):
        o_ref[:, c * LANES:(c + 1) * LANES] = a_ref[pl.ds(c, tm, stride=chunks), :]


def _untile(a3, chunks, tm):
    N = a3.shape[0] // chunks
    return pl.pallas_call(
        _untile_kernel,
        grid=(N // tm,),
        in_specs=[pl.BlockSpec((tm * chunks, LANES), lambda i: (i, 0))],
        out_specs=pl.BlockSpec((tm, chunks * LANES), lambda i: (i, 0)),
        out_shape=jax.ShapeDtypeStruct((N, chunks * LANES), F32),
        compiler_params=pltpu.CompilerParams(dimension_semantics=("arbitrary",), vmem_limit_bytes=VMEM_LIMIT),
        name="untile",
    )(a3)


def _layer(x, mem, norm_mix_g, w_in, att_q_norm_g, att_k_norm_g, att_sink, ml_conv_w, ml_conv_b,
           ml_gate_b, ml_out_norm_g, w_out, norm_mem_g, mem_kv_norm_g, w_mem_q, w_mem_kv,
           mem_q_norm_g, mem_k_norm_g, w_mem_o, norm_ffn_g, w_router, b_router,
           w_exp_gate, w_exp_up, w_exp_down):
    B, S, D = x.shape
    row = lambda v: v.reshape(1, -1).astype(F32)
    tm_in = min(512, S)
    tm_mix = min(256, S)
    cap = CAPACITY_FACTOR * S // N_EXPERTS

    mem_k, mem_v = _mem_kv(mem, row(mem_kv_norm_g), w_mem_kv.astype(BF16), row(mem_k_norm_g))

    w_main = w_in[:, :_GATES[0]].astype(BF16)
    wg_t = w_in[:, _GATES[0]:_GATES[1]].T.astype(BF16)
    q_g = jnp.tile(row(att_q_norm_g), (1, ATT_HEADS)) * (ATT_DH ** -0.5)
    k_g = jnp.tile(row(att_k_norm_g), (1, ATT_KV))
    aq, ak, av, mqk, mv, mo, g_rows, g_cols = _inproj(x, row(norm_mix_g), w_main, wg_t, ml_gate_b.reshape(-1, 1).astype(F32),
                                               q_g, k_g, tm_in)

    att = _attention(att_sink.astype(F32), aq, ak, av)
    hf, hb = _mlstm(mqk, mv, g_rows, g_cols, ml_conv_w.astype(F32), row(ml_conv_b))

    y3, hn3, aff_t = _mix(x, att, hf, hb, mo, row(ml_out_norm_g), w_out.astype(BF16), row(norm_mem_g),
                             w_mem_q.astype(BF16), row(mem_q_norm_g), mem_k, mem_v, w_mem_o.astype(BF16),
                             row(norm_ffn_g), w_router.astype(F32), row(b_router), tm_mix)

    idx, gval = _topc(aff_t, cap, D // LANES)
    out3 = _expert_ffn(idx, gval, hn3, y3, w_exp_gate.astype(BF16), w_exp_up.astype(BF16), w_exp_down.astype(BF16))
    return _untile(out3, D // LANES, tm_in).reshape(B, S, D)


def kernel(x, mem, norm_mix_g, w_in, att_q_norm_g, att_k_norm_g, att_sink, ml_conv_w, ml_conv_b, ml_gate_b,
           ml_out_norm_g, w_out, norm_mem_g, mem_kv_norm_g, w_mem_q, w_mem_kv, mem_q_norm_g, mem_k_norm_g,
           w_mem_o, norm_ffn_g, w_router, b_router, w_exp_gate, w_exp_up, w_exp_down):
    params = (norm_mix_g, w_in, att_q_norm_g, att_k_norm_g, att_sink, ml_conv_w, ml_conv_b, ml_gate_b,
              ml_out_norm_g, w_out, norm_mem_g, mem_kv_norm_g, w_mem_q, w_mem_kv, mem_q_norm_g, mem_k_norm_g,
              w_mem_o, norm_ffn_g, w_router, b_router, w_exp_gate, w_exp_up, w_exp_down)
    depth = norm_mix_g.shape[0]
    for l in range(depth):
        x = _layer(x, mem, *[p[l] for p in params])
    return x
```

```python
import functools

import jax
import jax.numpy as jnp
from jax import lax
from jax.experimental import pallas as pl
from jax.experimental.pallas import tpu as pltpu

F32 = jnp.float32
BF16 = jnp.bfloat16
I32 = jnp.int32

EPS = 1e-6
LANES = 128
BLK = 128
ATT_HEADS, ATT_KV, ATT_DH = 8, 2, 64
ML_HEADS, ML_DH = 4, 128
MEM_HEADS, MEM_DH = 4, 128
N_EXPERTS = 16
CAPACITY_FACTOR = 2
NEG = -1e30
HALO = 16
VMEM_LIMIT = 48 * 1024 * 1024

_NT = (((1,), (1,)), ((), ()))


def _dot(a, b):
    return jnp.dot(a, b, preferred_element_type=F32)


def _dot_nt(a, b):
    return lax.dot_general(a, b, _NT, preferred_element_type=F32)


def _split3(x):
    hi = x.astype(BF16)
    r1 = x - hi.astype(F32)
    mid = r1.astype(BF16)
    lo = (r1 - mid.astype(F32)).astype(BF16)
    return hi, mid, lo


def _dot01_nt(m01, x):
    hi, mid, lo = _split3(x)
    return _dot_nt(m01, hi) + _dot_nt(m01, mid) + _dot_nt(m01, lo)


def _dot01(x, m01):
    hi, mid, lo = _split3(x)
    return _dot(hi, m01) + _dot(mid, m01) + _dot(lo, m01)


def _rms(x, g):
    ms = jnp.mean(x * x, axis=-1, keepdims=True)
    return x * lax.rsqrt(ms + EPS) * g


def _sigmoid(x):
    return 1.0 / (1.0 + jnp.exp(-x))


def _iota(shape, dim):
    return lax.broadcasted_iota(I32, shape, dim)


def _mem_kv_kernel(mem_ref, g_ref, w_ref, kg_ref, k_ref, v_ref):
    mn = _rms(mem_ref[0], g_ref[...]).astype(BF16)
    kv = _dot(mn, w_ref[...])
    width = MEM_HEADS * MEM_DH
    for h in range(MEM_HEADS):
        sl = slice(h * MEM_DH, (h + 1) * MEM_DH)
        k_ref[0, :, sl] = _rms(kv[:, sl], kg_ref[...]).astype(BF16)
    v_ref[0] = kv[:, width:].astype(BF16)


def _mem_kv(mem, g, w_kv, k_g):
    B, M, D = mem.shape
    width = MEM_HEADS * MEM_DH
    full = lambda *s: pl.BlockSpec(s, lambda b: (0,) * len(s))
    return pl.pallas_call(
        _mem_kv_kernel,
        grid=(B,),
        in_specs=[pl.BlockSpec((1, M, D), lambda b: (b, 0, 0)), full(1, D), full(D, 2 * width), full(1, MEM_DH)],
        out_specs=[pl.BlockSpec((1, M, width), lambda b: (b, 0, 0))] * 2,
        out_shape=[jax.ShapeDtypeStruct((B, M, width), BF16)] * 2,
        compiler_params=pltpu.CompilerParams(dimension_semantics=("arbitrary",), vmem_limit_bytes=VMEM_LIMIT),
        name="mem_kv",
    )(mem, g, w_kv, k_g)


_AQ = (0, 512)
_AK = (512, 640)
_AV = (640, 768)
_MQK = (768, 1792)
_MV = (1792, 2304)
_MO = (2304, 2816)
_GATES = (2816, 2832)


GC_F, GC_GMAX, GC_EEND, GC_MLOC = 0, 1, 2, 3


def _inproj_kernel(x_ref, g_ref, w_ref, wgt_ref, gb_ref, qg_ref, kg_ref, bdq_ref, bdk_ref, trif_ref, trib_ref,
                   blk_ref, aq_ref, ak_ref, av_ref, mqk_ref, mv_ref, mo_ref, grow_ref, gcol_ref):
    h = _rms(x_ref[0], g_ref[...]).astype(BF16)
    gt = _dot_nt(wgt_ref[...], h) + gb_ref[...]
    logsig = jnp.minimum(gt, 0.0) - jnp.log(1.0 + jnp.exp(-jnp.abs(gt)))
    tm = gt.shape[1]
    H = ML_HEADS
    pos = _iota((H, tm), 1) % BLK

    def chunk_scan(li, lf, tri_ref, fwd):
        f = _dot01(lf, tri_ref[...])
        g = li - f
        gmax = g
        sh = 1
        while sh < BLK:
            if fwd:
                gmax = jnp.where(pos >= sh, jnp.maximum(gmax, pltpu.roll(gmax, sh, axis=1)), gmax)
            else:
                gmax = jnp.where(pos < BLK - sh, jnp.maximum(gmax, pltpu.roll(gmax, tm - sh, axis=1)), gmax)
            sh *= 2
        w_end = _dot01(lf, blk_ref[...]) + g
        m_loc = jnp.concatenate(
            [jnp.broadcast_to(jnp.max(w_end[:, c0:c0 + BLK], axis=1, keepdims=True), (H, BLK))
             for c0 in range(0, tm, BLK)], axis=1)
        return f, g, gmax, jnp.exp(w_end - m_loc), m_loc

    f_f, g_f, gm_f, ee_f, ml_f = chunk_scan(gt[0:H], logsig[H:2 * H], trif_ref, True)
    f_b, g_b, gm_b, ee_b, ml_b = chunk_scan(gt[2 * H:3 * H], logsig[3 * H:4 * H], trib_ref, False)
    grow_ref[0] = jnp.concatenate([g_f, g_b], axis=0)
    kinds = [None] * 4
    kinds[GC_F], kinds[GC_GMAX], kinds[GC_EEND], kinds[GC_MLOC] = (f_f, f_b), (gm_f, gm_b), (ee_f, ee_b), (ml_f, ml_b)
    rows = jnp.concatenate([r for pair in kinds for r in pair], axis=0)
    pad = jnp.zeros((LANES - rows.shape[0], BLK), F32)
    for c0 in range(0, tm, BLK):
        gcol_ref[0, c0:c0 + BLK, :] = jnp.concatenate([rows[:, c0:c0 + BLK], pad], axis=0).T
    sec = lambda s: _dot(h, w_ref[:, s[0]:s[1]])
    aq = sec(_AQ)
    ssq = _dot((aq * aq).astype(BF16), bdq_ref[...])
    aq_ref[0] = (aq * lax.rsqrt(ssq * (1.0 / ATT_DH) + EPS) * qg_ref[...]).astype(BF16)
    ak = sec(_AK)
    ssk = _dot((ak * ak).astype(BF16), bdk_ref[...])
    ak_ref[0] = (ak * lax.rsqrt(ssk * (1.0 / ATT_DH) + EPS) * kg_ref[...]).astype(BF16)
    av_ref[0] = sec(_AV).astype(BF16)
    mqk_ref[0] = sec(_MQK).astype(BF16)
    mv_ref[0] = sec(_MV).astype(BF16)
    mo_ref[0] = sec(_MO).astype(BF16)


def _inproj(x, g, w_main, wg_t, gate_b, q_g, k_g, tm):
    B, S, D = x.shape
    bdq = (jnp.arange(512)[:, None] // ATT_DH == jnp.arange(512)[None, :] // ATT_DH).astype(BF16)
    bdk = bdq[:128, :128]
    s_from, s_to = jnp.arange(tm)[:, None], jnp.arange(tm)[None, :]
    same = s_from // BLK == s_to // BLK
    tri_f = (same & (s_from <= s_to)).astype(BF16)
    tri_b = (same & (s_from >= s_to)).astype(BF16)
    full = lambda *s: pl.BlockSpec(s, lambda b, i: (0,) * len(s))
    tok = lambda w: pl.BlockSpec((1, tm, w), lambda b, i: (b, i, 0))
    widths = (512, 128, 128, 1024, 512, 512)
    return pl.pallas_call(
        _inproj_kernel,
        grid=(B, S // tm),
        in_specs=[tok(D), full(1, D), full(D, w_main.shape[1]), full(16, D), full(16, 1),
                  full(1, 512), full(1, 128), full(512, 512), full(128, 128), full(tm, tm), full(tm, tm), full(tm, tm)],
        out_specs=[tok(w) for w in widths] + [pl.BlockSpec((1, 2 * ML_HEADS, tm), lambda b, i: (b, 0, i)), tok(LANES)],
        out_shape=[jax.ShapeDtypeStruct((B, S, w), BF16) for w in widths]
        + [jax.ShapeDtypeStruct((B, 2 * ML_HEADS, S), F32), jax.ShapeDtypeStruct((B, S, LANES), F32)],
        compiler_params=pltpu.CompilerParams(dimension_semantics=("arbitrary", "arbitrary"),
                                             vmem_limit_bytes=VMEM_LIMIT),
        name="inproj",
    )(x, g, w_main, wg_t, gate_b, q_g, k_g, bdq, bdk, tri_f, tri_b, same.astype(BF16))


ATT_GROUP = ATT_HEADS // ATT_KV
ATT_HEAD_ORDER = tuple(kv * ATT_GROUP + g for g in range(ATT_GROUP) for kv in range(ATT_KV))


def _attn_kernel(sink_ref, q_ref, kp_ref, kc_ref, kn_ref, vp_ref, vc_ref, vn_ref, bias_ref, o_ref):
    n = pl.program_id(1)
    nb = pl.num_programs(1)
    G = ATT_GROUP
    kband = jnp.concatenate([kp_ref[0], kc_ref[0], kn_ref[0]], axis=0)
    vband = jnp.concatenate([vp_ref[0], vc_ref[0], vn_ref[0]], axis=0)
    first = _iota(kband.shape, 1) < ATT_DH
    kband, vband = kband.astype(F32), vband.astype(F32)
    keep_lanes = lambda a, mine: jnp.where(mine, a, 0.0).astype(BF16)
    q = q_ref[0]
    qs = jnp.concatenate([q[:, g * LANES:(g + 1) * LANES] for g in range(G)], axis=0)
    si = _iota((1, 3 * BLK), 1)
    edge = jnp.where(((si < BLK) & (n == 0)) | ((si >= 2 * BLK) & (n == nb - 1)), NEG, 0.0)
    tot = None
    sink_terms = []
    for kv in range(ATT_KV):
        mine = first if kv == 0 else ~first
        s = _dot_nt(qs, keep_lanes(kband, mine))
        vaug = jnp.concatenate([keep_lanes(vband, mine), mine.astype(F32).astype(BF16)], axis=1)
        ps, st = [], []
        for g in range(G):
            h = kv * G + g
            sg = s[g * BLK:(g + 1) * BLK] + bias_ref[h] + edge
            sink = sink_ref[h]
            m = jnp.maximum(jnp.max(sg, axis=-1, keepdims=True), sink)
            ps.append(jnp.exp(sg - m).astype(BF16))
            st.append(jnp.exp(sink - m))
        part = _dot(jnp.concatenate(ps, axis=0), vaug)
        tot = part if tot is None else tot + part
        sink_terms.append(jnp.concatenate(st, axis=0))
    lane_first = _iota((1, LANES), 1) < ATT_DH
    den = tot[:, LANES:] + jnp.where(lane_first, sink_terms[0], sink_terms[1])
    out = (tot[:, :LANES] / den).astype(BF16)
    o_ref[0] = jnp.concatenate([out[g * BLK:(g + 1) * BLK] for g in range(G)], axis=1)


def _attention(sink, aq, ak, av):
    B, S, _ = aq.shape
    nb = S // BLK
    kvw = ATT_KV * ATT_DH
    assert kvw == LANES
    dist = jnp.abs(jnp.arange(BLK)[:, None] + BLK - jnp.arange(3 * BLK)[None, :]).astype(F32)
    slopes = jnp.exp2(-8.0 * jnp.arange(1, ATT_HEADS + 1, dtype=F32) / ATT_HEADS)
    bias = jnp.where(dist <= BLK, -slopes[:, None, None] * dist, NEG)
    prev = pl.BlockSpec((1, BLK, kvw), lambda b, n, *_: (b, jnp.maximum(n - 1, 0), 0))
    cur = pl.BlockSpec((1, BLK, kvw), lambda b, n, *_: (b, n, 0))
    nxt = pl.BlockSpec((1, BLK, kvw), lambda b, n, *_: (b, jnp.minimum(n + 1, nb - 1), 0))
    qspec = pl.BlockSpec((1, BLK, ATT_HEADS * ATT_DH), lambda b, n, *_: (b, n, 0))
    bspec = pl.BlockSpec((ATT_HEADS, BLK, 3 * BLK), lambda b, n, *_: (0, 0, 0))
    return pl.pallas_call(
        _attn_kernel,
        grid_spec=pltpu.PrefetchScalarGridSpec(
            num_scalar_prefetch=1, grid=(B, nb),
            in_specs=[qspec, prev, cur, nxt, prev, cur, nxt, bspec], out_specs=qspec),
        out_shape=jax.ShapeDtypeStruct(aq.shape, BF16),
        compiler_params=pltpu.CompilerParams(dimension_semantics=("arbitrary", "arbitrary"),
                                             vmem_limit_bytes=VMEM_LIMIT),
        name="win_attn",
    )(sink, aq, ak, ak, ak, av, av, av, bias)


def _mlstm_kernel(xf_ref, xfp_ref, xfn_ref, xb_ref, xbp_ref, xbn_ref, vf_ref, vb_ref, grf_ref, grb_ref,
                  gcf_ref, gcb_ref, cw_ref, cb_ref, hf_ref, hb_ref, cn_ref, m_ref):
    c = pl.program_id(1)
    nc = pl.num_programs(1)
    L = BLK
    width = ML_HEADS * ML_DH

    @pl.when(c == 0)
    def _():
        cn_ref[...] = jnp.zeros_like(cn_ref)
        m_ref[...] = jnp.zeros_like(m_ref)

    row = _iota((L, 1), 0)
    tt = _iota((L, L), 0)
    ss = _iota((L, L), 1)
    ones_v = jnp.ones((L, ML_DH), BF16)

    def conv_silu(x_ref, p_ref, nx_ref, first, last):
        x = x_ref[0].astype(F32)
        prow = jnp.where(first, 0.0, p_ref[0, HALO - 1:HALO, :].astype(F32))
        nrow = jnp.where(last, 0.0, nx_ref[0, 0:1, :].astype(F32))
        xp = jnp.where(row == 0, prow, pltpu.roll(x, 1, axis=0))
        xn = jnp.where(row == L - 1, nrow, pltpu.roll(x, L - 1, axis=0))
        y = cw_ref[0:1, :] * xp + cw_ref[1:2, :] * x + cw_ref[2:3, :] * xn + cb_ref[...]
        return y * _sigmoid(y)

    def direction(d, qk, v_ref, grow_ref, gcol_ref, out_ref):
        H = ML_HEADS
        keep = (ss <= tt) if d == 0 else (ss >= tt)
        g_row = grow_ref[0][H * d:H * (d + 1), :]
        gc = gcol_ref[0]
        kind = lambda i: gc[:, 2 * H * i + H * d:2 * H * i + H * (d + 1)]
        f, g_max, e_end, m_loc = kind(GC_F), kind(GC_GMAX), kind(GC_EEND), kind(GC_MLOC)[0:1, :]
        end = L - 1 if d == 0 else 0
        f_end = f[end:end + 1, :]
        m0 = m_ref[d, 0:1, 0:H]
        mm = jnp.maximum(m0, g_max)
        e_inter = jnp.exp(m0 - mm)
        floor = jnp.exp(-(f + mm))
        m_new = jnp.maximum(f_end + m0, m_loc)
        ca = jnp.exp(f_end + m0 - m_new)
        cb = jnp.exp(m_loc - m_new)
        m_ref[d, 0:1, 0:H] = m_new
        col = lambda a, h, w=ML_DH: jnp.broadcast_to(a[:, h:h + 1], (L, w))
        v_all = v_ref[0]
        for h in range(H):
            u = H * d + h
            hs = slice(h * ML_DH, (h + 1) * ML_DH)
            q = qk[:, hs]
            k = qk[:, width + h * ML_DH: width + (h + 1) * ML_DH] * (ML_DH ** -0.5)
            v1 = jnp.concatenate([v_all[:, hs], ones_v], axis=1)
            qb = q.astype(BF16)
            kb = k.astype(BF16)
            cn0 = cn_ref[u]
            dec = jnp.where(keep, jnp.exp(g_row[h:h + 1, :] - col(mm, h)), 0.0)
            s_qk = (_dot_nt(qb, kb) * dec).astype(BF16)
            tot = col(e_inter, h, 2 * ML_DH) * _dot(qb, cn0.astype(BF16)) + _dot(s_qk, v1)
            out_ref[0, :, hs] = tot[:, :ML_DH] / jnp.maximum(jnp.abs(tot[:, ML_DH:]), col(floor, h))
            ks = k * col(e_end, h)
            cn_ref[u] = ca[:, h:h + 1] * cn0 + cb[:, h:h + 1] * _dot(ks.T.astype(BF16), v1)

    qk_f = conv_silu(xf_ref, xfp_ref, xfn_ref, c == 0, c == nc - 1)
    direction(0, qk_f, vf_ref, grf_ref, gcf_ref, hf_ref)
    qk_b = conv_silu(xb_ref, xbp_ref, xbn_ref, c == nc - 1, c == 0)
    direction(1, qk_b, vb_ref, grb_ref, gcb_ref, hb_ref)


def _mlstm(mqk, mv, g_rows, g_cols, conv_w, conv_b):
    B, S, _ = mqk.shape
    nc = S // BLK
    width = ML_HEADS * ML_DH
    per = BLK // HALO
    nh = S // HALO
    fwd = lambda c: c
    bwd = lambda c: nc - 1 - c

    def specs(ci):
        own = pl.BlockSpec((1, BLK, 2 * width), lambda b, c: (b, ci(c), 0))
        prev = pl.BlockSpec((1, HALO, 2 * width), lambda b, c: (b, jnp.maximum(ci(c) * per - 1, 0), 0))
        nxt = pl.BlockSpec((1, HALO, 2 * width), lambda b, c: (b, jnp.minimum((ci(c) + 1) * per, nh - 1), 0))
        return [own, prev, nxt]

    vspec = lambda ci: pl.BlockSpec((1, BLK, width), lambda b, c: (b, ci(c), 0))
    gspec = lambda ci: pl.BlockSpec((1, 2 * ML_HEADS, BLK), lambda b, c: (b, 0, ci(c)))
    cspec = lambda ci: pl.BlockSpec((1, BLK, LANES), lambda b, c: (b, ci(c), 0))
    full = lambda *s: pl.BlockSpec(s, lambda b, c: (0,) * len(s))
    units = 2 * ML_HEADS
    return pl.pallas_call(
        _mlstm_kernel,
        grid=(B, nc),
        in_specs=specs(fwd) + specs(bwd) + [vspec(fwd), vspec(bwd), gspec(fwd), gspec(bwd), cspec(fwd), cspec(bwd),
                                            full(3, 2 * width), full(1, 2 * width)],
        out_specs=[vspec(fwd), vspec(bwd)],
        out_shape=[jax.ShapeDtypeStruct((B, S, width), F32)] * 2,
        scratch_shapes=[pltpu.VMEM((units, ML_DH, 2 * ML_DH), F32), pltpu.VMEM((2, 8, LANES), F32)],
        compiler_params=pltpu.CompilerParams(dimension_semantics=("arbitrary", "arbitrary"),
                                             vmem_limit_bytes=VMEM_LIMIT),
        name="mlstm",
    )(mqk, mqk, mqk, mqk, mqk, mqk, mv, mv, g_rows, g_rows, g_cols, g_cols, conv_w, conv_b)


def _mix_kernel(x_ref, att_ref, hf_ref, hb_ref, mo_ref, og_ref, wo_ref,
                gm_ref, wq_ref, mqg_ref, k_ref, v_ref, wmo_ref,
                gf_ref, wr_ref, brt_ref,
                y_ref, hn_ref, afft_ref):
    x = x_ref[0]
    width = ML_HEADS * ML_DH
    ml = hf_ref[0] + hb_ref[0]
    mo = mo_ref[0].astype(F32)
    parts = []
    for h in range(ML_HEADS):
        sl = slice(h * ML_DH, (h + 1) * ML_DH)
        parts.append((_sigmoid(mo[:, sl]) * _rms(ml[:, sl], og_ref[:, sl])).astype(BF16))
    ml_out = jnp.concatenate(parts, axis=1)
    y1 = x + _dot(att_ref[0], wo_ref[0:width, :]) + _dot(ml_out, wo_ref[width:2 * width, :])
    h2 = _rms(y1, gm_ref[...]).astype(BF16)
    qm = _dot(h2, wq_ref[...])
    k = k_ref[0]
    v = v_ref[0]
    outs = []
    for h in range(MEM_HEADS):
        sl = slice(h * MEM_DH, (h + 1) * MEM_DH)
        qh = (_rms(qm[:, sl], mqg_ref[...]) * (MEM_DH ** -0.5)).astype(BF16)
        s = _dot_nt(qh, k[:, sl])
        p = jnp.exp(s - jnp.max(s, axis=-1, keepdims=True))
        o = _dot(p.astype(BF16), v[:, sl]) / jnp.sum(p, axis=-1, keepdims=True)
        outs.append(o.astype(BF16))
    y2 = y1 + _dot(jnp.concatenate(outs, axis=1), wmo_ref[...])
    h3 = _rms(y2, gf_ref[...])
    h_hi = h3.astype(BF16)
    h_lo = (h3 - h_hi.astype(F32)).astype(BF16)
    wr = wr_ref[...]
    w_hi = wr.astype(BF16)
    w_lo = (wr - w_hi.astype(F32)).astype(BF16)
    logits = _dot(h_hi, w_hi) + _dot(h_lo, w_hi) + _dot(h_hi, w_lo)
    logits_t = jnp.concatenate([logits[r0:r0 + LANES, :].T[:N_EXPERTS, :] for r0 in range(0, logits.shape[0], LANES)],
                               axis=1) + brt_ref[...]
    pt = jnp.exp(logits_t - jnp.max(logits_t, axis=0, keepdims=True))
    afft_ref[0] = pt / jnp.sum(pt, axis=0, keepdims=True)
    tm, chunks = y2.shape[0], y2.shape[1] // LANES
    for c in range(chunks):
        y_ref[pl.ds(c, tm, stride=chunks), :] = y2[:, c * LANES:(c + 1) * LANES]
        hn_ref[pl.ds(c, tm, stride=chunks), :] = h3[:, c * LANES:(c + 1) * LANES]


def _mix(x, att, hf, hb, mo, out_g, w_out, g_mem, w_q, mq_g, mem_k, mem_v, w_mo, g_ffn, w_r, b_r, tm):
    B, S, D = x.shape
    M = mem_k.shape[1]
    width = ML_HEADS * ML_DH
    mw = MEM_HEADS * MEM_DH
    full = lambda *s: pl.BlockSpec(s, lambda b, i: (0,) * len(s))
    tok = lambda w: pl.BlockSpec((1, tm, w), lambda b, i: (b, i, 0))
    chunks = D // LANES
    tiled = pl.BlockSpec((tm * chunks, LANES), lambda b, i: (b * (S // tm) + i, 0))
    memspec = pl.BlockSpec((1, M, mw), lambda b, i: (b, 0, 0))
    tiled_shape = jax.ShapeDtypeStruct((B * S * chunks, LANES), F32)
    return pl.pallas_call(
        _mix_kernel,
        grid=(B, S // tm),
        in_specs=[tok(D), tok(width), tok(width), tok(width), tok(width), full(1, width), full(2 * width, D),
                  full(1, D), full(D, mw), full(1, MEM_DH), memspec, memspec, full(mw, D),
                  full(1, D), full(D, LANES), full(N_EXPERTS, 1)],
        out_specs=[tiled, tiled, pl.BlockSpec((1, N_EXPERTS, tm), lambda b, i: (b, 0, i))],
        out_shape=[tiled_shape, tiled_shape, jax.ShapeDtypeStruct((B, N_EXPERTS, S), F32)],
        compiler_params=pltpu.CompilerParams(dimension_semantics=("arbitrary", "arbitrary"),
                                             vmem_limit_bytes=VMEM_LIMIT),
        name="mix_mem_router",
    )(x, att, hf, hb, mo, out_g, w_out, g_mem, w_q, mq_g, mem_k, mem_v, w_mo, g_ffn,
      jnp.pad(w_r, ((0, 0), (0, LANES - N_EXPERTS))), b_r.T)


def _topc_kernel(aff2_ref, aff3_ref, idx_ref, gval_ref, *, cap, seq, row_pitch):
    a2 = aff2_ref[0]
    bits2 = pltpu.bitcast(a2, I32)
    capf = float(cap)

    def bisect(i, lo):
        cand = lo | jnp.left_shift(jnp.int32(1), 30 - i)
        cnt = jnp.sum((bits2 >= cand).astype(F32), axis=1, keepdims=True)
        return jnp.where(cnt >= capf, cand, lo)

    thr_all = lax.fori_loop(0, 31, bisect, jnp.zeros((N_EXPERTS, 1), I32))
    need_all = capf - jnp.sum((bits2 > thr_all).astype(F32), axis=1, keepdims=True)

    T = aff3_ref.shape[2]
    tri_u = (_iota((LANES, LANES), 0) <= _iota((LANES, LANES), 1)).astype(BF16)
    tri_l = (_iota((LANES, LANES), 1) <= _iota((LANES, LANES), 0)).astype(BF16)
    ones8 = jnp.ones((8, LANES), BF16)
    before = _iota((T, T), 1) < _iota((T, T), 0)
    kcol = _iota((T, 1), 0).astype(F32)
    j = _iota((1, cap), 1).astype(F32)
    eye = (_iota((LANES, LANES), 0) == _iota((LANES, LANES), 1)).astype(BF16)
    lane_pos = _iota((LANES, cap), 0).astype(F32)

    def tile_starts(maskb):
        tot_row = _dot_nt(ones8, maskb)[0:1, :]
        return jnp.sum(jnp.where(before, tot_row, 0.0), axis=1, keepdims=True)

    for e in range(N_EXPERTS):
        bits = pltpu.bitcast(aff3_ref[0, e], I32)
        thr = thr_all[e:e + 1, :]
        need = need_all[e:e + 1, :]
        gt = bits > thr
        eq = bits == thr
        eqb = eq.astype(BF16)
        eq_rank = _dot(eqb, tri_u) + tile_starts(eqb) - eq.astype(F32)
        sel = gt | (eq & (eq_rank < need))
        selb = sel.astype(BF16)
        cs = _dot(selb, tri_u)
        start = tile_starts(selb)
        end = start + cs[:, LANES - 1:LANES]
        onehot = (start <= j) & (j < end)
        ohf = onehot.astype(F32)
        tile_of = jnp.sum(ohf * kcol, axis=0, keepdims=True)
        j_loc = j - jnp.sum(ohf * start, axis=0, keepdims=True)
        cs_t = _dot_nt(tri_l, selb)
        ohb = onehot.astype(BF16)
        r_t = _dot(cs_t.astype(BF16), ohb)
        local = jnp.sum((r_t <= j_loc).astype(F32), axis=0, keepdims=True)
        idx_ref[0, e:e + 1, :] = ((tile_of * LANES + local).astype(I32) + pl.program_id(0) * seq) * row_pitch
        a_t = _dot01_nt(eye, aff3_ref[0, e])
        a_tile = sum(_dot(term, ohb) for term in _split3(a_t))
        gval_ref[0, e:e + 1, :] = jnp.sum(jnp.where(lane_pos == local, a_tile, 0.0), axis=0, keepdims=True)


def _topc(aff_t, cap, row_pitch):
    B, E, S = aff_t.shape
    T = S // LANES
    aff3 = aff_t.reshape(B, E, T, LANES)
    return pl.pallas_call(
        functools.partial(_topc_kernel, cap=cap, seq=S, row_pitch=row_pitch),
        grid=(B,),
        in_specs=[pl.BlockSpec((1, E, S), lambda b: (b, 0, 0)), pl.BlockSpec((1, E, T, LANES), lambda b: (b, 0, 0, 0))],
        out_specs=[pl.BlockSpec((1, E, cap), lambda b: (b, 0, 0))] * 2,
        out_shape=[jax.ShapeDtypeStruct((B, E, cap), I32), jax.ShapeDtypeStruct((B, E, cap), F32)],
        compiler_params=pltpu.CompilerParams(dimension_semantics=("arbitrary",), vmem_limit_bytes=VMEM_LIMIT),
        name="topc",
    )(aff_t, aff3)


FFN_ROWS = 256
FFN_COLS = 256


def _ffn_kernel(idxp_ref, idxn_ref, gv_ref, hn_hbm, acc_in_hbm, wg_ref, wu_ref, wd_ref, acc_hbm,
                xe_ref, ab_ref, sems, *, nb, cap, d_model, ff):
    del acc_in_hbm
    s = pl.program_id(0)
    last = pl.num_programs(0) - 1
    slot = s % 2
    a_cur = s % 3
    a_nxt = (s + 1) % 3
    a_prv = (s + 2) % 3
    chunks = d_model // LANES

    def hbm_tile(ref, r):
        return ref.at[pl.ds(pl.multiple_of(r, chunks), chunks)]

    def vmem_tile(ref, sl, j):
        return ref.at[sl, pl.ds(pl.multiple_of(j * chunks, chunks), chunks)]

    def gather_x(r, j, sl):
        return pltpu.make_async_copy(hbm_tile(hn_hbm, r), vmem_tile(xe_ref, sl, j), sems.at[0])

    def gather_a(r, j, sl):
        return pltpu.make_async_copy(hbm_tile(acc_hbm, r), vmem_tile(ab_ref, sl, j), sems.at[1])

    def scatter(r, j, sl):
        return pltpu.make_async_copy(vmem_tile(ab_ref, sl, j), hbm_tile(acc_hbm, r), sems.at[2])

    def wait_x(sl):
        pltpu.make_async_copy(hn_hbm.at[pl.ds(0, cap * chunks)], xe_ref.at[sl], sems.at[0]).wait()

    def wait_a(sl):
        pltpu.make_async_copy(acc_hbm.at[pl.ds(0, cap * chunks)], ab_ref.at[sl], sems.at[1]).wait()

    def wait_scatter(sl):
        pltpu.make_async_copy(ab_ref.at[sl], acc_hbm.at[pl.ds(0, cap * chunks)], sems.at[2]).wait()

    @pl.when(s == 0)
    def _():
        @pl.loop(0, cap)
        def _(j):
            r = idxp_ref[0, 0, j]
            gather_x(r, j, 0).start()
            gather_a(r, j, 0).start()
            gather_a(r, j, 2).start()
        wait_a(2)

    wait_x(slot)
    wait_a(a_cur)

    halves = FFN_COLS // LANES
    nblk = cap // FFN_ROWS
    n_ct = ff // FFN_COLS
    assert d_model // FFN_COLS == n_ct

    n_updates = nblk * n_ct
    tail = n_ct
    tail_rows = 16 if nblk > 1 else -(-cap // tail)
    body_rows = -(-(cap - tail * tail_rows) // (n_updates - tail)) if nblk > 1 else 0
    next_row = [0]

    def issue_rows(n):
        lo, hi = next_row[0], min(next_row[0] + n, cap)
        next_row[0] = hi
        for j in range(lo, hi):
            rn = idxn_ref[0, 0, j]
            gather_x(rn, j, 1 - slot).start()
            gather_a(rn, j, a_nxt).start()
            scatter(idxp_ref[0, 0, j], j, a_prv).start()

    def chunk_rows(rb, c):
        return pl.ds(rb * FFN_ROWS * chunks + c, FFN_ROWS, stride=chunks)

    def gate_rows(rb):
        g_row = gv_ref[0, rb]
        return jnp.concatenate(
            [jnp.broadcast_to(g_row[:, i * LANES:(i + 1) * LANES], (LANES, LANES)).T for i in range(FFN_ROWS // LANES)],
            axis=0)

    hid_prev = None
    for stage in range(nblk + 1):
        up, dn = stage < nblk, stage >= 1
        if up:
            xb = jnp.concatenate([xe_ref[slot, chunk_rows(stage, c), :] for c in range(chunks)], axis=1).astype(BF16)
        if dn:
            gval = gate_rows(stage - 1)
        hid = []
        for ct in range(n_ct):
            cs = slice(ct * FFN_COLS, (ct + 1) * FFN_COLS)
            if up:
                hg = _dot(xb, wg_ref[0, :, cs])
                hu = _dot(xb, wu_ref[0, :, cs])
                hid.append((hg * _sigmoid(hg) * hu).astype(BF16))
            if dn:
                ye = _dot(hid_prev, wd_ref[0, :, cs])
                for i in range(halves):
                    ab_ref[a_cur, chunk_rows(stage - 1, ct * halves + i), :] += ye[:, i * LANES:(i + 1) * LANES] * gval
                issue_rows(body_rows if up else tail_rows)
        hid_prev = jnp.concatenate(hid, axis=1) if up else None
    issue_rows(cap)
    assert next_row[0] == cap

    wait_scatter(a_prv)

    @pl.when(s == last)
    def _():
        wait_x(1 - slot)
        wait_a(a_nxt)

        @pl.loop(0, cap)
        def _(j):
            scatter(idxn_ref[0, 0, j], j, a_cur).start()
        wait_scatter(a_cur)


def _expert_ffn(idx, gval, hn3, acc3, wg, wu, wd):
    B, E, cap = idx.shape
    d_model, ff = wg.shape[1], wg.shape[2]
    chunks = d_model // LANES
    assert B >= 3 and cap % FFN_ROWS == 0 and ff % FFN_COLS == 0 and d_model % FFN_COLS == 0
    ns = E * B
    nblk = cap // FFN_ROWS
    idx3 = idx.reshape(B * E, 1, cap)
    gv4 = gval.reshape(B * E, nblk, 1, FFN_ROWS)
    any_spec = pl.BlockSpec(memory_space=pl.ANY)
    wspec = lambda r, c: pl.BlockSpec((1, r, c), lambda s: (s // B, 0, 0))
    blk = lambda s: (s % B) * E + s // B
    ispec = lambda f: pl.BlockSpec((1, 1, cap), lambda s: (blk(f(s)), 0, 0), memory_space=pltpu.SMEM)
    kern = functools.partial(_ffn_kernel, nb=B, cap=cap, d_model=d_model, ff=ff)
    return pl.pallas_call(
        kern,
        grid=(ns,),
        in_specs=[ispec(lambda s: jnp.maximum(s - 1, 0)), ispec(lambda s: jnp.minimum(s + 1, ns - 1)),
                  pl.BlockSpec((1, nblk, 1, FFN_ROWS), lambda s: (blk(s), 0, 0, 0)),
                  any_spec, any_spec, wspec(d_model, ff), wspec(d_model, ff), wspec(ff, d_model)],
        out_specs=any_spec,
        out_shape=jax.ShapeDtypeStruct(acc3.shape, F32),
        scratch_shapes=[pltpu.VMEM((2, cap * chunks, LANES), F32), pltpu.VMEM((3, cap * chunks, LANES), F32),
                        pltpu.SemaphoreType.DMA((3,))],
        input_output_aliases={4: 0},
        compiler_params=pltpu.CompilerParams(dimension_semantics=("arbitrary",), vmem_limit_bytes=VMEM_LIMIT),
        name="expert_ffn",
    )(idx3, idx3, gv4, hn3, acc3, wg, wu, wd)


def _untile_kernel(a_ref, o_ref):
    tm, chunks = o_ref.shape[0], o_ref.shape[1] // LANES
    for c in range(chunks):
        o_ref[:, c * LANES:(c + 1) * LANES] = a_ref[pl.ds(c, tm, stride=chunks), :]


def _untile(a3, chunks, tm):
    N = a3.shape[0] // chunks
    return pl.pallas_call(
        _untile_kernel,
        grid=(N // tm,),
        in_specs=[pl.BlockSpec((tm * chunks, LANES), lambda i: (i, 0))],
        out_specs=pl.BlockSpec((tm, chunks * LANES), lambda i: (i, 0)),
        out_shape=jax.ShapeDtypeStruct((N, chunks * LANES), F32),
        compiler_params=pltpu.CompilerParams(dimension_semantics=("arbitrary",), vmem_limit_bytes=VMEM_LIMIT),
        name="untile",
    )(a3)


def _layer(x, mem, norm_mix_g, w_in, att_q_norm_g, att_k_norm_g, att_sink, ml_conv_w, ml_conv_b,
           ml_gate_b, ml_out_norm_g, w_out, norm_mem_g, mem_kv_norm_g, w_mem_q, w_mem_kv,
           mem_q_norm_g, mem_k_norm_g, w_mem_o, norm_ffn_g, w_router, b_router,
           w_exp_gate, w_exp_up, w_exp_down):
    B, S, D = x.shape
    row = lambda v: v.reshape(1, -1).astype(F32)
    tm_in = min(512, S)
    tm_mix = min(256, S)
    cap = CAPACITY_FACTOR * S // N_EXPERTS

    mem_k, mem_v = _mem_kv(mem, row(mem_kv_norm_g), w_mem_kv.astype(BF16), row(mem_k_norm_g))

    order = jnp.array(ATT_HEAD_ORDER)
    w_aq = w_in[:, _AQ[0]:_AQ[1]].reshape(D, ATT_HEADS, ATT_DH)[:, order].reshape(D, -1)
    w_main = jnp.concatenate([w_aq, w_in[:, _AQ[1]:_GATES[0]]], axis=1).astype(BF16)
    wg_t = w_in[:, _GATES[0]:_GATES[1]].T.astype(BF16)
    q_g = jnp.tile(row(att_q_norm_g), (1, ATT_HEADS)) * (ATT_DH ** -0.5)
    k_g = jnp.tile(row(att_k_norm_g), (1, ATT_KV))
    aq, ak, av, mqk, mv, mo, g_rows, g_cols = _inproj(x, row(norm_mix_g), w_main, wg_t, ml_gate_b.reshape(-1, 1).astype(F32),
                                               q_g, k_g, tm_in)

    att = _attention(att_sink.astype(F32), aq, ak, av)
    hf, hb = _mlstm(mqk, mv, g_rows, g_cols, ml_conv_w.astype(F32), row(ml_conv_b))

    att_w = ATT_HEADS * ATT_DH
    w_out_att = w_out[:att_w].reshape(ATT_HEADS, ATT_DH, D)[order].reshape(att_w, D)
    w_out_p = jnp.concatenate([w_out_att, w_out[att_w:]], axis=0).astype(BF16)
    y3, hn3, aff_t = _mix(x, att, hf, hb, mo, row(ml_out_norm_g), w_out_p, row(norm_mem_g),
                             w_mem_q.astype(BF16), row(mem_q_norm_g), mem_k, mem_v, w_mem_o.astype(BF16),
                             row(norm_ffn_g), w_router.astype(F32), row(b_router), tm_mix)

    idx, gval = _topc(aff_t, cap, D // LANES)
    out3 = _expert_ffn(idx, gval, hn3, y3, w_exp_gate.astype(BF16), w_exp_up.astype(BF16), w_exp_down.astype(BF16))
    return _untile(out3, D // LANES, tm_in).reshape(B, S, D)


def kernel(x, mem, norm_mix_g, w_in, att_q_norm_g, att_k_norm_g, att_sink, ml_conv_w, ml_conv_b, ml_gate_b,
           ml_out_norm_g, w_out, norm_mem_g, mem_kv_norm_g, w_mem_q, w_mem_kv, mem_q_norm_g, mem_k_norm_g,
           w_mem_o, norm_ffn_g, w_router, b_router, w_exp_gate, w_exp_up, w_exp_down):
    params = (norm_mix_g, w_in, att_q_norm_g, att_k_norm_g, att_sink, ml_conv_w, ml_conv_b, ml_gate_b,
              ml_out_norm_g, w_out, norm_mem_g, mem_kv_norm_g, w_mem_q, w_mem_kv, mem_q_norm_g, mem_k_norm_g,
              w_mem_o, norm_ffn_g, w_router, b_router, w_exp_gate, w_exp_up, w_exp_down)
    depth = norm_mix_g.shape[0]
    for l in range(depth):
        x = _layer(x, mem, *[p[l] for p in params])
    return x
```

```python
import functools

import jax
import jax.numpy as jnp
from jax import lax
from jax.experimental import pallas as pl
from jax.experimental.pallas import tpu as pltpu

F32 = jnp.float32
BF16 = jnp.bfloat16
I32 = jnp.int32

EPS = 1e-6
LANES = 128
BLK = 128
ATT_HEADS, ATT_KV, ATT_DH = 8, 2, 64
ML_HEADS, ML_DH = 4, 128
MEM_HEADS, MEM_DH = 4, 128
N_EXPERTS = 16
CAPACITY_FACTOR = 2
NEG = -1e30
HALO_X = 8
VMEM_LIMIT = 48 * 1024 * 1024

_NT = (((1,), (1,)), ((), ()))


def _dot(a, b):
    return jnp.dot(a, b, preferred_element_type=F32)


def _dot_nt(a, b):
    return lax.dot_general(a, b, _NT, preferred_element_type=F32)


def _split3(x):
    hi = x.astype(BF16)
    r1 = x - hi.astype(F32)
    mid = r1.astype(BF16)
    lo = (r1 - mid.astype(F32)).astype(BF16)
    return hi, mid, lo


def _dot01_nt(m01, x):
    hi, mid, lo = _split3(x)
    return _dot_nt(m01, hi) + _dot_nt(m01, mid) + _dot_nt(m01, lo)


def _dot01(x, m01):
    hi, mid, lo = _split3(x)
    return _dot(hi, m01) + _dot(mid, m01) + _dot(lo, m01)


def _rms(x, g):
    ms = jnp.mean(x * x, axis=-1, keepdims=True)
    return x * lax.rsqrt(ms + EPS) * g


def _sigmoid(x):
    return 1.0 / (1.0 + jnp.exp(-x))


def _iota(shape, dim):
    return lax.broadcasted_iota(I32, shape, dim)


def _mem_kv_kernel(mem_ref, g_ref, w_ref, kg_ref, k_ref, v_ref):
    mn = _rms(mem_ref[0], g_ref[...]).astype(BF16)
    kv = _dot(mn, w_ref[...])
    width = MEM_HEADS * MEM_DH
    for h in range(MEM_HEADS):
        sl = slice(h * MEM_DH, (h + 1) * MEM_DH)
        k_ref[0, :, sl] = _rms(kv[:, sl], kg_ref[...]).astype(BF16)
    v_ref[0] = kv[:, width:].astype(BF16)


def _mem_kv(mem, g, w_kv, k_g):
    B, M, D = mem.shape
    width = MEM_HEADS * MEM_DH
    full = lambda *s: pl.BlockSpec(s, lambda b: (0,) * len(s))
    return pl.pallas_call(
        _mem_kv_kernel,
        grid=(B,),
        in_specs=[pl.BlockSpec((1, M, D), lambda b: (b, 0, 0)), full(1, D), full(D, 2 * width), full(1, MEM_DH)],
        out_specs=[pl.BlockSpec((1, M, width), lambda b: (b, 0, 0))] * 2,
        out_shape=[jax.ShapeDtypeStruct((B, M, width), BF16)] * 2,
        compiler_params=pltpu.CompilerParams(dimension_semantics=("arbitrary",), vmem_limit_bytes=VMEM_LIMIT),
        name="mem_kv",
    )(mem, g, w_kv, k_g)


_AQ = (0, 512)
_AK = (512, 640)
_AV = (640, 768)
_MQK = (768, 1792)
_MV = (1792, 2304)
_MO = (2304, 2816)
_GATES = (2816, 2832)


GC_F, GC_GMAX, GC_EEND, GC_MLOC = 0, 1, 2, 3


def _inproj_kernel(x_ref, xp_ref, xn_ref, g_ref, w_ref, wgt_ref, gb_ref, qg_ref, kg_ref, bdq_ref, bdk_ref, trif_ref,
                   trib_ref, blk_ref, cw_ref, cb_ref, aq_ref, ak_ref, av_ref, mqk_ref, mv_ref, mo_ref, grow_ref, gcol_ref):
    h = _rms(x_ref[0], g_ref[...]).astype(BF16)
    gt = _dot_nt(wgt_ref[...], h) + gb_ref[...]
    logsig = jnp.minimum(gt, 0.0) - jnp.log(1.0 + jnp.exp(-jnp.abs(gt)))
    tm = gt.shape[1]
    H = ML_HEADS
    pos = _iota((H, tm), 1) % BLK

    def chunk_scan(li, lf, tri_ref, fwd):
        f = _dot01(lf, tri_ref[...])
        g = li - f
        gmax = g
        sh = 1
        while sh < BLK:
            if fwd:
                gmax = jnp.where(pos >= sh, jnp.maximum(gmax, pltpu.roll(gmax, sh, axis=1)), gmax)
            else:
                gmax = jnp.where(pos < BLK - sh, jnp.maximum(gmax, pltpu.roll(gmax, tm - sh, axis=1)), gmax)
            sh *= 2
        w_end = _dot01(lf, blk_ref[...]) + g
        m_loc = jnp.concatenate(
            [jnp.broadcast_to(jnp.max(w_end[:, c0:c0 + BLK], axis=1, keepdims=True), (H, BLK))
             for c0 in range(0, tm, BLK)], axis=1)
        return f, g, gmax, jnp.exp(w_end - m_loc), m_loc

    f_f, g_f, gm_f, ee_f, ml_f = chunk_scan(gt[0:H], logsig[H:2 * H], trif_ref, True)
    f_b, g_b, gm_b, ee_b, ml_b = chunk_scan(gt[2 * H:3 * H], logsig[3 * H:4 * H], trib_ref, False)
    grow_ref[0] = jnp.concatenate([g_f, g_b], axis=0)
    kinds = [None] * 4
    kinds[GC_F], kinds[GC_GMAX], kinds[GC_EEND], kinds[GC_MLOC] = (f_f, f_b), (gm_f, gm_b), (ee_f, ee_b), (ml_f, ml_b)
    rows = jnp.concatenate([r for pair in kinds for r in pair], axis=0)
    pad = jnp.zeros((LANES - rows.shape[0], BLK), F32)
    for c0 in range(0, tm, BLK):
        gcol_ref[0, c0:c0 + BLK, :] = jnp.concatenate([rows[:, c0:c0 + BLK], pad], axis=0).T
    sec = lambda s: _dot(h, w_ref[:, s[0]:s[1]])
    aq = sec(_AQ)
    ssq = _dot((aq * aq).astype(BF16), bdq_ref[...])
    aq_ref[0] = (aq * lax.rsqrt(ssq * (1.0 / ATT_DH) + EPS) * qg_ref[...]).astype(BF16)
    ak = sec(_AK)
    ssk = _dot((ak * ak).astype(BF16), bdk_ref[...])
    ak_ref[0] = (ak * lax.rsqrt(ssk * (1.0 / ATT_DH) + EPS) * kg_ref[...]).astype(BF16)
    av_ref[0] = sec(_AV).astype(BF16)
    i, ni = pl.program_id(1), pl.num_programs(1)
    h_halo = jnp.concatenate([_rms(xp_ref[0], g_ref[...]), _rms(xn_ref[0], g_ref[...])], axis=0).astype(BF16)
    halo = _dot(h_halo, w_ref[:, _MQK[0]:_MQK[1]])
    before = jnp.where(i == 0, 0.0, halo[HALO_X - 1:HALO_X, :])
    after = jnp.where(i == ni - 1, 0.0, halo[HALO_X:HALO_X + 1, :])
    raw = sec(_MQK)
    r = _iota((tm, 1), 0)
    x_prev = jnp.where(r == 0, before, pltpu.roll(raw, 1, axis=0))
    x_next = jnp.where(r == tm - 1, after, pltpu.roll(raw, tm - 1, axis=0))
    y = cw_ref[0:1, :] * x_prev + cw_ref[1:2, :] * raw + cw_ref[2:3, :] * x_next + cb_ref[...]
    k_scale = jnp.where(_iota((1, y.shape[1]), 1) < ML_HEADS * ML_DH, 1.0, ML_DH ** -0.5)
    mqk_ref[0] = (y * _sigmoid(y) * k_scale).astype(BF16)
    mv_ref[0] = sec(_MV).astype(BF16)
    mo_ref[0] = sec(_MO).astype(BF16)


def _inproj(x, g, w_main, wg_t, gate_b, q_g, k_g, conv_w, conv_b, tm):
    B, S, D = x.shape
    per = tm // HALO_X
    last = S // HALO_X - 1
    bdq = (jnp.arange(512)[:, None] // ATT_DH == jnp.arange(512)[None, :] // ATT_DH).astype(BF16)
    bdk = bdq[:128, :128]
    s_from, s_to = jnp.arange(tm)[:, None], jnp.arange(tm)[None, :]
    same = s_from // BLK == s_to // BLK
    tri_f = (same & (s_from <= s_to)).astype(BF16)
    tri_b = (same & (s_from >= s_to)).astype(BF16)
    full = lambda *s: pl.BlockSpec(s, lambda b, i: (0,) * len(s))
    tok = lambda w: pl.BlockSpec((1, tm, w), lambda b, i: (b, i, 0))
    widths = (512, 128, 128, 1024, 512, 512)
    return pl.pallas_call(
        _inproj_kernel,
        grid=(B, S // tm),
        in_specs=[tok(D),
                  pl.BlockSpec((1, HALO_X, D), lambda b, i: (b, jnp.maximum(i * per - 1, 0), 0)),
                  pl.BlockSpec((1, HALO_X, D), lambda b, i: (b, jnp.minimum((i + 1) * per, last), 0)),
                  full(1, D), full(D, w_main.shape[1]), full(16, D), full(16, 1),
                  full(1, 512), full(1, 128), full(512, 512), full(128, 128), full(tm, tm), full(tm, tm), full(tm, tm),
                  full(3, 2 * ML_HEADS * ML_DH), full(1, 2 * ML_HEADS * ML_DH)],
        out_specs=[tok(w) for w in widths] + [pl.BlockSpec((1, 2 * ML_HEADS, tm), lambda b, i: (b, 0, i)), tok(LANES)],
        out_shape=[jax.ShapeDtypeStruct((B, S, w), BF16) for w in widths]
        + [jax.ShapeDtypeStruct((B, 2 * ML_HEADS, S), F32), jax.ShapeDtypeStruct((B, S, LANES), F32)],
        compiler_params=pltpu.CompilerParams(dimension_semantics=("arbitrary", "arbitrary"),
                                             vmem_limit_bytes=VMEM_LIMIT),
        name="inproj",
    )(x, x, x, g, w_main, wg_t, gate_b, q_g, k_g, bdq, bdk, tri_f, tri_b, same.astype(BF16), conv_w, conv_b)


ATT_GROUP = ATT_HEADS // ATT_KV
ATT_HEAD_ORDER = tuple(kv * ATT_GROUP + g for g in range(ATT_GROUP) for kv in range(ATT_KV))


def _attn_kernel(sink_ref, q_ref, kp_ref, kc_ref, kn_ref, vp_ref, vc_ref, vn_ref, bias_ref, o_ref):
    n = pl.program_id(1)
    nb = pl.num_programs(1)
    G = ATT_GROUP
    kband = jnp.concatenate([kp_ref[0], kc_ref[0], kn_ref[0]], axis=0)
    vband = jnp.concatenate([vp_ref[0], vc_ref[0], vn_ref[0]], axis=0)
    first = _iota(kband.shape, 1) < ATT_DH
    kband, vband = kband.astype(F32), vband.astype(F32)
    keep_lanes = lambda a, mine: jnp.where(mine, a, 0.0).astype(BF16)
    q = q_ref[0]
    qs = jnp.concatenate([q[:, g * LANES:(g + 1) * LANES] for g in range(G)], axis=0)
    si = _iota((1, 3 * BLK), 1)
    edge = jnp.where(((si < BLK) & (n == 0)) | ((si >= 2 * BLK) & (n == nb - 1)), NEG, 0.0)
    tot = None
    sink_terms = []
    for kv in range(ATT_KV):
        mine = first if kv == 0 else ~first
        s = _dot_nt(qs, keep_lanes(kband, mine))
        vaug = jnp.concatenate([keep_lanes(vband, mine), mine.astype(F32).astype(BF16)], axis=1)
        ps, st = [], []
        for g in range(G):
            h = kv * G + g
            sg = s[g * BLK:(g + 1) * BLK] + bias_ref[h] + edge
            sink = sink_ref[h]
            m = jnp.maximum(jnp.max(sg, axis=-1, keepdims=True), sink)
            ps.append(jnp.exp(sg - m).astype(BF16))
            st.append(jnp.exp(sink - m))
        part = _dot(jnp.concatenate(ps, axis=0), vaug)
        tot = part if tot is None else tot + part
        sink_terms.append(jnp.concatenate(st, axis=0))
    lane_first = _iota((1, LANES), 1) < ATT_DH
    den = tot[:, LANES:] + jnp.where(lane_first, sink_terms[0], sink_terms[1])
    out = (tot[:, :LANES] / den).astype(BF16)
    o_ref[0] = jnp.concatenate([out[g * BLK:(g + 1) * BLK] for g in range(G)], axis=1)


def _attention(sink, aq, ak, av):
    B, S, _ = aq.shape
    nb = S // BLK
    kvw = ATT_KV * ATT_DH
    assert kvw == LANES
    dist = jnp.abs(jnp.arange(BLK)[:, None] + BLK - jnp.arange(3 * BLK)[None, :]).astype(F32)
    slopes = jnp.exp2(-8.0 * jnp.arange(1, ATT_HEADS + 1, dtype=F32) / ATT_HEADS)
    bias = jnp.where(dist <= BLK, -slopes[:, None, None] * dist, NEG)
    prev = pl.BlockSpec((1, BLK, kvw), lambda b, n, *_: (b, jnp.maximum(n - 1, 0), 0))
    cur = pl.BlockSpec((1, BLK, kvw), lambda b, n, *_: (b, n, 0))
    nxt = pl.BlockSpec((1, BLK, kvw), lambda b, n, *_: (b, jnp.minimum(n + 1, nb - 1), 0))
    qspec = pl.BlockSpec((1, BLK, ATT_HEADS * ATT_DH), lambda b, n, *_: (b, n, 0))
    bspec = pl.BlockSpec((ATT_HEADS, BLK, 3 * BLK), lambda b, n, *_: (0, 0, 0))
    return pl.pallas_call(
        _attn_kernel,
        grid_spec=pltpu.PrefetchScalarGridSpec(
            num_scalar_prefetch=1, grid=(B, nb),
            in_specs=[qspec, prev, cur, nxt, prev, cur, nxt, bspec], out_specs=qspec),
        out_shape=jax.ShapeDtypeStruct(aq.shape, BF16),
        compiler_params=pltpu.CompilerParams(dimension_semantics=("arbitrary", "arbitrary"),
                                             vmem_limit_bytes=VMEM_LIMIT),
        name="win_attn",
    )(sink, aq, ak, ak, ak, av, av, av, bias)


def _mlstm_kernel(xf_ref, xb_ref, vf_ref, vb_ref, grf_ref, grb_ref, gcf_ref, gcb_ref, hf_ref, hb_ref, cn_ref, m_ref):
    c = pl.program_id(1)
    nc = pl.num_programs(1)
    L = BLK
    width = ML_HEADS * ML_DH

    @pl.when(c == 0)
    def _():
        cn_ref[...] = jnp.zeros_like(cn_ref)
        m_ref[...] = jnp.zeros_like(m_ref)

    tt = _iota((L, L), 0)
    ss = _iota((L, L), 1)
    ones_v = jnp.ones((L, ML_DH), BF16)

    def direction(d, qk_ref, v_ref, grow_ref, gcol_ref, out_ref):
        H = ML_HEADS
        keep = (ss <= tt) if d == 0 else (ss >= tt)
        g_row = grow_ref[0][H * d:H * (d + 1), :]
        gc = gcol_ref[0]
        kind = lambda i: gc[:, 2 * H * i + H * d:2 * H * i + H * (d + 1)]
        f, g_max, e_end, m_loc = kind(GC_F), kind(GC_GMAX), kind(GC_EEND), kind(GC_MLOC)[0:1, :]
        end = L - 1 if d == 0 else 0
        f_end = f[end:end + 1, :]
        m0 = m_ref[d, 0:1, 0:H]
        mm = jnp.maximum(m0, g_max)
        e_inter = jnp.exp(m0 - mm)
        floor = jnp.exp(-(f + mm))
        m_new = jnp.maximum(f_end + m0, m_loc)
        ca = jnp.exp(f_end + m0 - m_new)
        cb = jnp.exp(m_loc - m_new)
        m_ref[d, 0:1, 0:H] = m_new
        col = lambda a, h, w=ML_DH: jnp.broadcast_to(a[:, h:h + 1], (L, w))
        v_all = v_ref[0]
        for h in range(H):
            u = H * d + h
            hs = slice(h * ML_DH, (h + 1) * ML_DH)
            qb = qk_ref[0, :, hs]
            kb = qk_ref[0, :, width + h * ML_DH: width + (h + 1) * ML_DH]
            v1 = jnp.concatenate([v_all[:, hs], ones_v], axis=1)
            cn0 = cn_ref[u]
            dec = jnp.where(keep, jnp.exp(g_row[h:h + 1, :] - col(mm, h)), 0.0)
            s_qk = (_dot_nt(qb, kb) * dec).astype(BF16)
            tot = col(e_inter, h, 2 * ML_DH) * _dot(qb, cn0.astype(BF16)) + _dot(s_qk, v1)
            out_ref[0, :, hs] = tot[:, :ML_DH] / jnp.maximum(jnp.abs(tot[:, ML_DH:]), col(floor, h))
            ks = kb.astype(F32) * col(e_end, h)
            cn_ref[u] = ca[:, h:h + 1] * cn0 + cb[:, h:h + 1] * _dot(ks.T.astype(BF16), v1)

    direction(0, xf_ref, vf_ref, grf_ref, gcf_ref, hf_ref)
    direction(1, xb_ref, vb_ref, grb_ref, gcb_ref, hb_ref)


def _mlstm(mqk, mv, g_rows, g_cols):
    B, S, _ = mqk.shape
    nc = S // BLK
    width = ML_HEADS * ML_DH
    fwd = lambda c: c
    bwd = lambda c: nc - 1 - c
    qkspec = lambda ci: pl.BlockSpec((1, BLK, 2 * width), lambda b, c: (b, ci(c), 0))
    vspec = lambda ci: pl.BlockSpec((1, BLK, width), lambda b, c: (b, ci(c), 0))
    gspec = lambda ci: pl.BlockSpec((1, 2 * ML_HEADS, BLK), lambda b, c: (b, 0, ci(c)))
    cspec = lambda ci: pl.BlockSpec((1, BLK, LANES), lambda b, c: (b, ci(c), 0))
    units = 2 * ML_HEADS
    return pl.pallas_call(
        _mlstm_kernel,
        grid=(B, nc),
        in_specs=[qkspec(fwd), qkspec(bwd), vspec(fwd), vspec(bwd), gspec(fwd), gspec(bwd), cspec(fwd), cspec(bwd)],
        out_specs=[vspec(fwd), vspec(bwd)],
        out_shape=[jax.ShapeDtypeStruct((B, S, width), F32)] * 2,
        scratch_shapes=[pltpu.VMEM((units, ML_DH, 2 * ML_DH), F32), pltpu.VMEM((2, 8, LANES), F32)],
        compiler_params=pltpu.CompilerParams(dimension_semantics=("arbitrary", "arbitrary"),
                                             vmem_limit_bytes=VMEM_LIMIT),
        name="mlstm",
    )(mqk, mqk, mv, mv, g_rows, g_rows, g_cols, g_cols)


def _mix_kernel(x_ref, att_ref, hf_ref, hb_ref, mo_ref, og_ref, wo_ref,
                gm_ref, wq_ref, mqg_ref, k_ref, v_ref, wmo_ref,
                gf_ref, wr_ref, brt_ref,
                y_ref, hn_ref, afft_ref):
    x = x_ref[0]
    width = ML_HEADS * ML_DH
    ml = hf_ref[0] + hb_ref[0]
    mo = mo_ref[0].astype(F32)
    parts = []
    for h in range(ML_HEADS):
        sl = slice(h * ML_DH, (h + 1) * ML_DH)
        parts.append((_sigmoid(mo[:, sl]) * _rms(ml[:, sl], og_ref[:, sl])).astype(BF16))
    ml_out = jnp.concatenate(parts, axis=1)
    y1 = x + _dot(att_ref[0], wo_ref[0:width, :]) + _dot(ml_out, wo_ref[width:2 * width, :])
    h2 = _rms(y1, gm_ref[...]).astype(BF16)
    qm = _dot(h2, wq_ref[...])
    k = k_ref[0]
    v = v_ref[0]
    outs = []
    for h in range(MEM_HEADS):
        sl = slice(h * MEM_DH, (h + 1) * MEM_DH)
        qh = (_rms(qm[:, sl], mqg_ref[...]) * (MEM_DH ** -0.5)).astype(BF16)
        s = _dot_nt(qh, k[:, sl])
        p = jnp.exp(s - jnp.max(s, axis=-1, keepdims=True))
        o = _dot(p.astype(BF16), v[:, sl]) / jnp.sum(p, axis=-1, keepdims=True)
        outs.append(o.astype(BF16))
    y2 = y1 + _dot(jnp.concatenate(outs, axis=1), wmo_ref[...])
    h3 = _rms(y2, gf_ref[...])
    h_hi = h3.astype(BF16)
    h_lo = (h3 - h_hi.astype(F32)).astype(BF16)
    wr = wr_ref[...]
    w_hi = wr.astype(BF16)
    w_lo = (wr - w_hi.astype(F32)).astype(BF16)
    logits = _dot(h_hi, w_hi) + _dot(h_lo, w_hi) + _dot(h_hi, w_lo)
    logits_t = jnp.concatenate([logits[r0:r0 + LANES, :].T[:N_EXPERTS, :] for r0 in range(0, logits.shape[0], LANES)],
                               axis=1) + brt_ref[...]
    pt = jnp.exp(logits_t - jnp.max(logits_t, axis=0, keepdims=True))
    afft_ref[0] = pt / jnp.sum(pt, axis=0, keepdims=True)
    tm, chunks = y2.shape[0], y2.shape[1] // LANES
    for c in range(chunks):
        y_ref[pl.ds(c, tm, stride=chunks), :] = y2[:, c * LANES:(c + 1) * LANES]
        hn_ref[pl.ds(c, tm, stride=chunks), :] = h3[:, c * LANES:(c + 1) * LANES]


def _mix(x, att, hf, hb, mo, out_g, w_out, g_mem, w_q, mq_g, mem_k, mem_v, w_mo, g_ffn, w_r, b_r, tm):
    B, S, D = x.shape
    M = mem_k.shape[1]
    width = ML_HEADS * ML_DH
    mw = MEM_HEADS * MEM_DH
    full = lambda *s: pl.BlockSpec(s, lambda b, i: (0,) * len(s))
    tok = lambda w: pl.BlockSpec((1, tm, w), lambda b, i: (b, i, 0))
    chunks = D // LANES
    tiled = pl.BlockSpec((tm * chunks, LANES), lambda b, i: (b * (S // tm) + i, 0))
    memspec = pl.BlockSpec((1, M, mw), lambda b, i: (b, 0, 0))
    tiled_shape = jax.ShapeDtypeStruct((B * S * chunks, LANES), F32)
    return pl.pallas_call(
        _mix_kernel,
        grid=(B, S // tm),
        in_specs=[tok(D), tok(width), tok(width), tok(width), tok(width), full(1, width), full(2 * width, D),
                  full(1, D), full(D, mw), full(1, MEM_DH), memspec, memspec, full(mw, D),
                  full(1, D), full(D, LANES), full(N_EXPERTS, 1)],
        out_specs=[tiled, tiled, pl.BlockSpec((1, N_EXPERTS, tm), lambda b, i: (b, 0, i))],
        out_shape=[tiled_shape, tiled_shape, jax.ShapeDtypeStruct((B, N_EXPERTS, S), F32)],
        compiler_params=pltpu.CompilerParams(dimension_semantics=("arbitrary", "arbitrary"),
                                             vmem_limit_bytes=VMEM_LIMIT),
        name="mix_mem_router",
    )(x, att, hf, hb, mo, out_g, w_out, g_mem, w_q, mq_g, mem_k, mem_v, w_mo, g_ffn,
      jnp.pad(w_r, ((0, 0), (0, LANES - N_EXPERTS))), b_r.T)


def _topc_kernel(aff2_ref, aff3_ref, idx_ref, gval_ref, *, cap, seq, row_pitch):
    a2 = aff2_ref[0]
    bits2 = pltpu.bitcast(a2, I32)
    capf = float(cap)

    def bisect(i, lo):
        cand = lo | jnp.left_shift(jnp.int32(1), 30 - i)
        cnt = jnp.sum((bits2 >= cand).astype(F32), axis=1, keepdims=True)
        return jnp.where(cnt >= capf, cand, lo)

    thr_all = lax.fori_loop(0, 31, bisect, jnp.zeros((N_EXPERTS, 1), I32))
    need_all = capf - jnp.sum((bits2 > thr_all).astype(F32), axis=1, keepdims=True)

    T = aff3_ref.shape[2]
    tri_u = (_iota((LANES, LANES), 0) <= _iota((LANES, LANES), 1)).astype(BF16)
    tri_l = (_iota((LANES, LANES), 1) <= _iota((LANES, LANES), 0)).astype(BF16)
    ones8 = jnp.ones((8, LANES), BF16)
    before = _iota((T, T), 1) < _iota((T, T), 0)
    kcol = _iota((T, 1), 0).astype(F32)
    j = _iota((1, cap), 1).astype(F32)
    eye = (_iota((LANES, LANES), 0) == _iota((LANES, LANES), 1)).astype(BF16)
    lane_pos = _iota((LANES, cap), 0).astype(F32)

    def tile_starts(maskb):
        tot_row = _dot_nt(ones8, maskb)[0:1, :]
        return jnp.sum(jnp.where(before, tot_row, 0.0), axis=1, keepdims=True)

    for e in range(N_EXPERTS):
        bits = pltpu.bitcast(aff3_ref[0, e], I32)
        thr = thr_all[e:e + 1, :]
        need = need_all[e:e + 1, :]
        gt = bits > thr
        eq = bits == thr
        eqb = eq.astype(BF16)
        eq_rank = _dot(eqb, tri_u) + tile_starts(eqb) - eq.astype(F32)
        sel = gt | (eq & (eq_rank < need))
        selb = sel.astype(BF16)
        cs = _dot(selb, tri_u)
        start = tile_starts(selb)
        end = start + cs[:, LANES - 1:LANES]
        onehot = (start <= j) & (j < end)
        ohf = onehot.astype(F32)
        tile_of = jnp.sum(ohf * kcol, axis=0, keepdims=True)
        j_loc = j - jnp.sum(ohf * start, axis=0, keepdims=True)
        cs_t = _dot_nt(tri_l, selb)
        ohb = onehot.astype(BF16)
        r_t = _dot(cs_t.astype(BF16), ohb)
        local = jnp.sum((r_t <= j_loc).astype(F32), axis=0, keepdims=True)
        idx_ref[0, e:e + 1, :] = ((tile_of * LANES + local).astype(I32) + pl.program_id(0) * seq) * row_pitch
        a_t = _dot01_nt(eye, aff3_ref[0, e])
        a_tile = sum(_dot(term, ohb) for term in _split3(a_t))
        gval_ref[0, e:e + 1, :] = jnp.sum(jnp.where(lane_pos == local, a_tile, 0.0), axis=0, keepdims=True)


def _topc(aff_t, cap, row_pitch):
    B, E, S = aff_t.shape
    T = S // LANES
    aff3 = aff_t.reshape(B, E, T, LANES)
    return pl.pallas_call(
        functools.partial(_topc_kernel, cap=cap, seq=S, row_pitch=row_pitch),
        grid=(B,),
        in_specs=[pl.BlockSpec((1, E, S), lambda b: (b, 0, 0)), pl.BlockSpec((1, E, T, LANES), lambda b: (b, 0, 0, 0))],
        out_specs=[pl.BlockSpec((1, E, cap), lambda b: (b, 0, 0))] * 2,
        out_shape=[jax.ShapeDtypeStruct((B, E, cap), I32), jax.ShapeDtypeStruct((B, E, cap), F32)],
        compiler_params=pltpu.CompilerParams(dimension_semantics=("arbitrary",), vmem_limit_bytes=VMEM_LIMIT),
        name="topc",
    )(aff_t, aff3)


FFN_ROWS = 256
FFN_COLS = 256
AB_RING = 4


def _ffn_kernel(idxp_ref, idxn_ref, gv_ref, hn_hbm, acc_in_hbm, wg_ref, wu_ref, wd_ref, acc_hbm,
                xe_ref, ab_ref, sems, *, nb, cap, d_model, ff):
    del acc_in_hbm
    s = pl.program_id(0)
    last = pl.num_programs(0) - 1
    slot = s % 2
    a_cur = s % AB_RING
    a_nxt = (s + 1) % AB_RING
    a_old = (s + 2) % AB_RING
    a_prv = (s + 3) % AB_RING
    chunks = d_model // LANES

    def hbm_tile(ref, r):
        return ref.at[pl.ds(pl.multiple_of(r, chunks), chunks)]

    def vmem_tile(ref, sl, j):
        return ref.at[sl, pl.ds(pl.multiple_of(j * chunks, chunks), chunks)]

    def gather_x(r, j, sl):
        return pltpu.make_async_copy(hbm_tile(hn_hbm, r), vmem_tile(xe_ref, sl, j), sems.at[0])

    def gather_a(r, j, sl):
        return pltpu.make_async_copy(hbm_tile(acc_hbm, r), vmem_tile(ab_ref, sl, j), sems.at[1])

    def scatter(r, j, sl):
        return pltpu.make_async_copy(vmem_tile(ab_ref, sl, j), hbm_tile(acc_hbm, r), sems.at[2])

    def wait_x(sl):
        pltpu.make_async_copy(hn_hbm.at[pl.ds(0, cap * chunks)], xe_ref.at[sl], sems.at[0]).wait()

    def wait_a(sl):
        pltpu.make_async_copy(acc_hbm.at[pl.ds(0, cap * chunks)], ab_ref.at[sl], sems.at[1]).wait()

    def wait_scatter(sl):
        pltpu.make_async_copy(ab_ref.at[sl], acc_hbm.at[pl.ds(0, cap * chunks)], sems.at[2]).wait()

    @pl.when(s == 0)
    def _():
        @pl.loop(0, cap)
        def _(j):
            r = idxp_ref[0, 0, j]
            gather_x(r, j, 0).start()
            gather_a(r, j, 0).start()
            gather_a(r, j, AB_RING - 1).start()
        wait_a(AB_RING - 1)

    @pl.when(s > 0)
    def _():
        wait_scatter(a_old)

    wait_x(slot)
    wait_a(a_cur)

    halves = FFN_COLS // LANES
    nblk = cap // FFN_ROWS
    n_ct = ff // FFN_COLS
    assert d_model // FFN_COLS == n_ct

    n_updates = nblk * n_ct
    early = n_updates - n_ct if nblk > 1 else n_updates
    gather_rows = -(-cap // early)
    scatter_rows = -(-cap // n_updates)
    next_gather, next_scatter, n_issue = [0], [0], [0]

    def issue_rows():
        k = n_issue[0]
        n_issue[0] += 1
        lo, hi = next_gather[0], min(next_gather[0] + (gather_rows if k < early else 0), cap)
        next_gather[0] = hi
        for j in range(lo, hi):
            rn = idxn_ref[0, 0, j]
            gather_x(rn, j, 1 - slot).start()
            gather_a(rn, j, a_nxt).start()
        lo, hi = next_scatter[0], min(next_scatter[0] + scatter_rows, cap)
        next_scatter[0] = hi
        for j in range(lo, hi):
            scatter(idxp_ref[0, 0, j], j, a_prv).start()

    def chunk_rows(rb, c):
        return pl.ds(rb * FFN_ROWS * chunks + c, FFN_ROWS, stride=chunks)

    def gate_rows(rb):
        g_row = gv_ref[0, rb]
        return jnp.concatenate(
            [jnp.broadcast_to(g_row[:, i * LANES:(i + 1) * LANES], (LANES, LANES)).T for i in range(FFN_ROWS // LANES)],
            axis=0)

    hid_prev = None
    for stage in range(nblk + 1):
        up, dn = stage < nblk, stage >= 1
        if up:
            xb = jnp.concatenate([xe_ref[slot, chunk_rows(stage, c), :] for c in range(chunks)], axis=1).astype(BF16)
        if dn:
            gval = gate_rows(stage - 1)
        hid = []
        for ct in range(n_ct):
            cs = slice(ct * FFN_COLS, (ct + 1) * FFN_COLS)
            if up:
                hg = _dot(xb, wg_ref[0, :, cs])
                hu = _dot(xb, wu_ref[0, :, cs])
                hid.append((hg * _sigmoid(hg) * hu).astype(BF16))
            if dn:
                ye = _dot(hid_prev, wd_ref[0, :, cs])
                for i in range(halves):
                    ab_ref[a_cur, chunk_rows(stage - 1, ct * halves + i), :] += ye[:, i * LANES:(i + 1) * LANES] * gval
                issue_rows()
        hid_prev = jnp.concatenate(hid, axis=1) if up else None
    assert next_gather[0] == cap and next_scatter[0] == cap and n_issue[0] == n_updates

    @pl.when(s == last)
    def _():
        wait_x(1 - slot)
        wait_a(a_nxt)
        wait_scatter(a_prv)

        @pl.loop(0, cap)
        def _(j):
            scatter(idxn_ref[0, 0, j], j, a_cur).start()
        wait_scatter(a_cur)


def _expert_ffn(idx, gval, hn3, acc3, wg, wu, wd):
    B, E, cap = idx.shape
    d_model, ff = wg.shape[1], wg.shape[2]
    chunks = d_model // LANES
    assert B >= 3 and cap % FFN_ROWS == 0 and ff % FFN_COLS == 0 and d_model % FFN_COLS == 0
    ns = E * B
    nblk = cap // FFN_ROWS
    idx3 = idx.reshape(B * E, 1, cap)
    gv4 = gval.reshape(B * E, nblk, 1, FFN_ROWS)
    any_spec = pl.BlockSpec(memory_space=pl.ANY)
    wspec = lambda r, c: pl.BlockSpec((1, r, c), lambda s: (s // B, 0, 0))
    blk = lambda s: (s % B) * E + s // B
    ispec = lambda f: pl.BlockSpec((1, 1, cap), lambda s: (blk(f(s)), 0, 0), memory_space=pltpu.SMEM)
    kern = functools.partial(_ffn_kernel, nb=B, cap=cap, d_model=d_model, ff=ff)
    return pl.pallas_call(
        kern,
        grid=(ns,),
        in_specs=[ispec(lambda s: jnp.maximum(s - 1, 0)), ispec(lambda s: jnp.minimum(s + 1, ns - 1)),
                  pl.BlockSpec((1, nblk, 1, FFN_ROWS), lambda s: (blk(s), 0, 0, 0)),
                  any_spec, any_spec, wspec(d_model, ff), wspec(d_model, ff), wspec(ff, d_model)],
        out_specs=any_spec,
        out_shape=jax.ShapeDtypeStruct(acc3.shape, F32),
        scratch_shapes=[pltpu.VMEM((2, cap * chunks, LANES), F32), pltpu.VMEM((AB_RING, cap * chunks, LANES), F32),
                        pltpu.SemaphoreType.DMA((3,))],
        input_output_aliases={4: 0},
        compiler_params=pltpu.CompilerParams(dimension_semantics=("arbitrary",), vmem_limit_bytes=VMEM_LIMIT),
        name="expert_ffn",
    )(idx3, idx3, gv4, hn3, acc3, wg, wu, wd)


def _untile_kernel(a_ref, o_ref):
    tm, chunks = o_ref.shape[0], o_ref.shape[1] // LANES
    for c in range(chunks):
        o_ref[:, c * LANES:(c + 1) * LANES] = a_ref[pl.ds(c, tm, stride=chunks), :]


def _untile(a3, chunks, tm):
    N = a3.shape[0] // chunks
    return pl.pallas_call(
        _untile_kernel,
        grid=(N // tm,),
        in_specs=[pl.BlockSpec((tm * chunks, LANES), lambda i: (i, 0))],
        out_specs=pl.BlockSpec((tm, chunks * LANES), lambda i: (i, 0)),
        out_shape=jax.ShapeDtypeStruct((N, chunks * LANES), F32),
        compiler_params=pltpu.CompilerParams(dimension_semantics=("arbitrary",), vmem_limit_bytes=VMEM_LIMIT),
        name="untile",
    )(a3)


def _layer(x, mem, norm_mix_g, w_in, att_q_norm_g, att_k_norm_g, att_sink, ml_conv_w, ml_conv_b,
           ml_gate_b, ml_out_norm_g, w_out, norm_mem_g, mem_kv_norm_g, w_mem_q, w_mem_kv,
           mem_q_norm_g, mem_k_norm_g, w_mem_o, norm_ffn_g, w_router, b_router,
           w_exp_gate, w_exp_up, w_exp_down):
    B, S, D = x.shape
    row = lambda v: v.reshape(1, -1).astype(F32)
    tm_in = min(512, S)
    tm_mix = min(256, S)
    cap = CAPACITY_FACTOR * S // N_EXPERTS

    mem_k, mem_v = _mem_kv(mem, row(mem_kv_norm_g), w_mem_kv.astype(BF16), row(mem_k_norm_g))

    order = jnp.array(ATT_HEAD_ORDER)
    w_aq = w_in[:, _AQ[0]:_AQ[1]].reshape(D, ATT_HEADS, ATT_DH)[:, order].reshape(D, -1)
    w_main = jnp.concatenate([w_aq, w_in[:, _AQ[1]:_GATES[0]]], axis=1).astype(BF16)
    wg_t = w_in[:, _GATES[0]:_GATES[1]].T.astype(BF16)
    q_g = jnp.tile(row(att_q_norm_g), (1, ATT_HEADS)) * (ATT_DH ** -0.5)
    k_g = jnp.tile(row(att_k_norm_g), (1, ATT_KV))
    aq, ak, av, mqk, mv, mo, g_rows, g_cols = _inproj(x, row(norm_mix_g), w_main, wg_t, ml_gate_b.reshape(-1, 1).astype(F32),
                                               q_g, k_g, ml_conv_w.astype(F32), row(ml_conv_b), tm_in)

    att = _attention(att_sink.astype(F32), aq, ak, av)
    hf, hb = _mlstm(mqk, mv, g_rows, g_cols)

    att_w = ATT_HEADS * ATT_DH
    w_out_att = w_out[:att_w].reshape(ATT_HEADS, ATT_DH, D)[order].reshape(att_w, D)
    w_out_p = jnp.concatenate([w_out_att, w_out[att_w:]], axis=0).astype(BF16)
    y3, hn3, aff_t = _mix(x, att, hf, hb, mo, row(ml_out_norm_g), w_out_p, row(norm_mem_g),
                             w_mem_q.astype(BF16), row(mem_q_norm_g), mem_k, mem_v, w_mem_o.astype(BF16),
                             row(norm_ffn_g), w_router.astype(F32), row(b_router), tm_mix)

    idx, gval = _topc(aff_t, cap, D // LANES)
    out3 = _expert_ffn(idx, gval, hn3, y3, w_exp_gate.astype(BF16), w_exp_up.astype(BF16), w_exp_down.astype(BF16))
    return _untile(out3, D // LANES, tm_in).reshape(B, S, D)


def kernel(x, mem, norm_mix_g, w_in, att_q_norm_g, att_k_norm_g, att_sink, ml_conv_w, ml_conv_b, ml_gate_b,
           ml_out_norm_g, w_out, norm_mem_g, mem_kv_norm_g, w_mem_q, w_mem_kv, mem_q_norm_g, mem_k_norm_g,
           w_mem_o, norm_ffn_g, w_router, b_router, w_exp_gate, w_exp_up, w_exp_down):
    params = (norm_mix_g, w_in, att_q_norm_g, att_k_norm_g, att_sink, ml_conv_w, ml_conv_b, ml_gate_b,
              ml_out_norm_g, w_out, norm_mem_g, mem_kv_norm_g, w_mem_q, w_mem_kv, mem_q_norm_g, mem_k_norm_g,
              w_mem_o, norm_ffn_g, w_router, b_router, w_exp_gate, w_exp_up, w_exp_down)
    depth = norm_mix_g.shape[0]
    for l in range(depth):
        x = _layer(x, mem, *[p[l] for p in params])
    return x
```

```python
import functools

import jax
import jax.numpy as jnp
from jax import lax
from jax.experimental import pallas as pl
from jax.experimental.pallas import tpu as pltpu

F32 = jnp.float32
BF16 = jnp.bfloat16
I32 = jnp.int32

EPS = 1e-6
LANES = 128
BLK = 128
ATT_HEADS, ATT_KV, ATT_DH = 8, 2, 64
ML_HEADS, ML_DH = 4, 128
MEM_HEADS, MEM_DH = 4, 128
N_EXPERTS = 16
CAPACITY_FACTOR = 2
NEG = -1e30
HALO_X = 8
VMEM_LIMIT = 48 * 1024 * 1024

_NT = (((1,), (1,)), ((), ()))


def _dot(a, b):
    return jnp.dot(a, b, preferred_element_type=F32)


def _dot_nt(a, b):
    return lax.dot_general(a, b, _NT, preferred_element_type=F32)


def _split3(x):
    hi = x.astype(BF16)
    r1 = x - hi.astype(F32)
    mid = r1.astype(BF16)
    lo = (r1 - mid.astype(F32)).astype(BF16)
    return hi, mid, lo


def _dot01_nt(m01, x):
    hi, mid, lo = _split3(x)
    return _dot_nt(m01, hi) + _dot_nt(m01, mid) + _dot_nt(m01, lo)


def _dot01(x, m01):
    hi, mid, lo = _split3(x)
    return _dot(hi, m01) + _dot(mid, m01) + _dot(lo, m01)


def _rms(x, g):
    ms = jnp.mean(x * x, axis=-1, keepdims=True)
    return x * lax.rsqrt(ms + EPS) * g


def _sigmoid(x):
    return 1.0 / (1.0 + jnp.exp(-x))


def _iota(shape, dim):
    return lax.broadcasted_iota(I32, shape, dim)


def _mem_kv_kernel(mem_ref, g_ref, w_ref, kg_ref, k_ref, v_ref):
    mn = _rms(mem_ref[0], g_ref[...]).astype(BF16)
    kv = _dot(mn, w_ref[...])
    width = MEM_HEADS * MEM_DH
    for h in range(MEM_HEADS):
        sl = slice(h * MEM_DH, (h + 1) * MEM_DH)
        k_ref[0, :, sl] = _rms(kv[:, sl], kg_ref[...]).astype(BF16)
    v_ref[0] = kv[:, width:].astype(BF16)


def _mem_kv(mem, g, w_kv, k_g):
    B, M, D = mem.shape
    width = MEM_HEADS * MEM_DH
    full = lambda *s: pl.BlockSpec(s, lambda b: (0,) * len(s))
    return pl.pallas_call(
        _mem_kv_kernel,
        grid=(B,),
        in_specs=[pl.BlockSpec((1, M, D), lambda b: (b, 0, 0)), full(1, D), full(D, 2 * width), full(1, MEM_DH)],
        out_specs=[pl.BlockSpec((1, M, width), lambda b: (b, 0, 0))] * 2,
        out_shape=[jax.ShapeDtypeStruct((B, M, width), BF16)] * 2,
        compiler_params=pltpu.CompilerParams(dimension_semantics=("arbitrary",), vmem_limit_bytes=VMEM_LIMIT),
        name="mem_kv",
    )(mem, g, w_kv, k_g)


_AQ = (0, 512)
_AK = (512, 640)
_AV = (640, 768)
_MQK = (768, 1792)
_MV = (1792, 2304)
_MO = (2304, 2816)
_GATES = (2816, 2832)


GC_F, GC_GMAX, GC_EEND, GC_MLOC = 0, 1, 2, 3


def _inproj_kernel(x_ref, xp_ref, xn_ref, g_ref, w_ref, wgt_ref, gb_ref, qg_ref, kg_ref, bdq_ref, bdk_ref, trif_ref,
                   trib_ref, blk_ref, cw_ref, cb_ref, aq_ref, ak_ref, av_ref, mqk_ref, mv_ref, mo_ref, grow_ref, gcol_ref):
    h = _rms(x_ref[0], g_ref[...]).astype(BF16)
    gt = _dot_nt(wgt_ref[...], h) + gb_ref[...]
    logsig = jnp.minimum(gt, 0.0) - jnp.log(1.0 + jnp.exp(-jnp.abs(gt)))
    tm = gt.shape[1]
    H = ML_HEADS
    pos = _iota((H, tm), 1) % BLK

    def chunk_scan(li, lf, tri_ref, fwd):
        f = _dot01(lf, tri_ref[...])
        g = li - f
        gmax = g
        sh = 1
        while sh < BLK:
            if fwd:
                gmax = jnp.where(pos >= sh, jnp.maximum(gmax, pltpu.roll(gmax, sh, axis=1)), gmax)
            else:
                gmax = jnp.where(pos < BLK - sh, jnp.maximum(gmax, pltpu.roll(gmax, tm - sh, axis=1)), gmax)
            sh *= 2
        w_end = _dot01(lf, blk_ref[...]) + g
        m_loc = jnp.concatenate(
            [jnp.broadcast_to(jnp.max(w_end[:, c0:c0 + BLK], axis=1, keepdims=True), (H, BLK))
             for c0 in range(0, tm, BLK)], axis=1)
        return f, g, gmax, jnp.exp(w_end - m_loc), m_loc

    f_f, g_f, gm_f, ee_f, ml_f = chunk_scan(gt[0:H], logsig[H:2 * H], trif_ref, True)
    f_b, g_b, gm_b, ee_b, ml_b = chunk_scan(gt[2 * H:3 * H], logsig[3 * H:4 * H], trib_ref, False)
    grow_ref[0] = jnp.concatenate([g_f, g_b], axis=0)
    kinds = [None] * 4
    kinds[GC_F], kinds[GC_GMAX], kinds[GC_EEND], kinds[GC_MLOC] = (f_f, f_b), (gm_f, gm_b), (ee_f, ee_b), (ml_f, ml_b)
    rows = jnp.concatenate([r for pair in kinds for r in pair], axis=0)
    pad = jnp.zeros((LANES - rows.shape[0], BLK), F32)
    for c0 in range(0, tm, BLK):
        gcol_ref[0, c0:c0 + BLK, :] = jnp.concatenate([rows[:, c0:c0 + BLK], pad], axis=0).T
    sec = lambda s: _dot(h, w_ref[:, s[0]:s[1]])
    aq = sec(_AQ)
    ssq = _dot((aq * aq).astype(BF16), bdq_ref[...])
    aq_ref[0] = (aq * lax.rsqrt(ssq * (1.0 / ATT_DH) + EPS) * qg_ref[...]).astype(BF16)
    ak = sec(_AK)
    ssk = _dot((ak * ak).astype(BF16), bdk_ref[...])
    ak_ref[0] = (ak * lax.rsqrt(ssk * (1.0 / ATT_DH) + EPS) * kg_ref[...]).astype(BF16)
    av_ref[0] = sec(_AV).astype(BF16)
    i, ni = pl.program_id(1), pl.num_programs(1)
    h_halo = jnp.concatenate([_rms(xp_ref[0], g_ref[...]), _rms(xn_ref[0], g_ref[...])], axis=0).astype(BF16)
    halo = _dot(h_halo, w_ref[:, _MQK[0]:_MQK[1]])
    before = jnp.where(i == 0, 0.0, halo[HALO_X - 1:HALO_X, :])
    after = jnp.where(i == ni - 1, 0.0, halo[HALO_X:HALO_X + 1, :])
    raw = sec(_MQK)
    r = _iota((tm, 1), 0)
    x_prev = jnp.where(r == 0, before, pltpu.roll(raw, 1, axis=0))
    x_next = jnp.where(r == tm - 1, after, pltpu.roll(raw, tm - 1, axis=0))
    y = cw_ref[0:1, :] * x_prev + cw_ref[1:2, :] * raw + cw_ref[2:3, :] * x_next + cb_ref[...]
    k_scale = jnp.where(_iota((1, y.shape[1]), 1) < ML_HEADS * ML_DH, 1.0, ML_DH ** -0.5)
    mqk_ref[0] = (y * _sigmoid(y) * k_scale).astype(BF16)
    mv_ref[0] = sec(_MV).astype(BF16)
    mo_ref[0] = sec(_MO).astype(BF16)


def _inproj(x, g, w_main, wg_t, gate_b, q_g, k_g, conv_w, conv_b, tm):
    B, S, D = x.shape
    per = tm // HALO_X
    last = S // HALO_X - 1
    bdq = (jnp.arange(512)[:, None] // ATT_DH == jnp.arange(512)[None, :] // ATT_DH).astype(BF16)
    bdk = bdq[:128, :128]
    s_from, s_to = jnp.arange(tm)[:, None], jnp.arange(tm)[None, :]
    same = s_from // BLK == s_to // BLK
    tri_f = (same & (s_from <= s_to)).astype(BF16)
    tri_b = (same & (s_from >= s_to)).astype(BF16)
    full = lambda *s: pl.BlockSpec(s, lambda b, i: (0,) * len(s))
    tok = lambda w: pl.BlockSpec((1, tm, w), lambda b, i: (b, i, 0))
    widths = (512, 128, 128, 1024, 512, 512)
    return pl.pallas_call(
        _inproj_kernel,
        grid=(B, S // tm),
        in_specs=[tok(D),
                  pl.BlockSpec((1, HALO_X, D), lambda b, i: (b, jnp.maximum(i * per - 1, 0), 0)),
                  pl.BlockSpec((1, HALO_X, D), lambda b, i: (b, jnp.minimum((i + 1) * per, last), 0)),
                  full(1, D), full(D, w_main.shape[1]), full(16, D), full(16, 1),
                  full(1, 512), full(1, 128), full(512, 512), full(128, 128), full(tm, tm), full(tm, tm), full(tm, tm),
                  full(3, 2 * ML_HEADS * ML_DH), full(1, 2 * ML_HEADS * ML_DH)],
        out_specs=[tok(w) for w in widths] + [pl.BlockSpec((1, 2 * ML_HEADS, tm), lambda b, i: (b, 0, i)), tok(LANES)],
        out_shape=[jax.ShapeDtypeStruct((B, S, w), BF16) for w in widths]
        + [jax.ShapeDtypeStruct((B, 2 * ML_HEADS, S), F32), jax.ShapeDtypeStruct((B, S, LANES), F32)],
        compiler_params=pltpu.CompilerParams(dimension_semantics=("arbitrary", "arbitrary"),
                                             vmem_limit_bytes=VMEM_LIMIT),
        name="inproj",
    )(x, x, x, g, w_main, wg_t, gate_b, q_g, k_g, bdq, bdk, tri_f, tri_b, same.astype(BF16), conv_w, conv_b)


ATT_GROUP = ATT_HEADS // ATT_KV
ATT_HEAD_ORDER = tuple(kv * ATT_GROUP + g for g in range(ATT_GROUP) for kv in range(ATT_KV))


def _attn_kernel(sink_ref, q_ref, kp_ref, kc_ref, kn_ref, vp_ref, vc_ref, vn_ref, bias_ref, o_ref):
    n = pl.program_id(1)
    nb = pl.num_programs(1)
    G = ATT_GROUP
    kband = jnp.concatenate([kp_ref[0], kc_ref[0], kn_ref[0]], axis=0)
    vband = jnp.concatenate([vp_ref[0], vc_ref[0], vn_ref[0]], axis=0)
    first = _iota(kband.shape, 1) < ATT_DH
    kband, vband = kband.astype(F32), vband.astype(F32)
    keep_lanes = lambda a, mine: jnp.where(mine, a, 0.0).astype(BF16)
    q = q_ref[0]
    qs = jnp.concatenate([q[:, g * LANES:(g + 1) * LANES] for g in range(G)], axis=0)
    si = _iota((1, 3 * BLK), 1)
    edge = jnp.where(((si < BLK) & (n == 0)) | ((si >= 2 * BLK) & (n == nb - 1)), NEG, 0.0)
    tot = None
    sink_terms = []
    for kv in range(ATT_KV):
        mine = first if kv == 0 else ~first
        s = _dot_nt(qs, keep_lanes(kband, mine))
        vaug = jnp.concatenate([keep_lanes(vband, mine), mine.astype(F32).astype(BF16)], axis=1)
        ps, st = [], []
        for g in range(G):
            h = kv * G + g
            sg = s[g * BLK:(g + 1) * BLK] + bias_ref[h] + edge
            sink = sink_ref[h]
            m = jnp.maximum(jnp.max(sg, axis=-1, keepdims=True), sink)
            ps.append(jnp.exp(sg - m).astype(BF16))
            st.append(jnp.exp(sink - m))
        part = _dot(jnp.concatenate(ps, axis=0), vaug)
        tot = part if tot is None else tot + part
        sink_terms.append(jnp.concatenate(st, axis=0))
    lane_first = _iota((1, LANES), 1) < ATT_DH
    den = tot[:, LANES:] + jnp.where(lane_first, sink_terms[0], sink_terms[1])
    out = (tot[:, :LANES] / den).astype(BF16)
    o_ref[0] = jnp.concatenate([out[g * BLK:(g + 1) * BLK] for g in range(G)], axis=1)


def _attention(sink, aq, ak, av):
    B, S, _ = aq.shape
    nb = S // BLK
    kvw = ATT_KV * ATT_DH
    assert kvw == LANES
    dist = jnp.abs(jnp.arange(BLK)[:, None] + BLK - jnp.arange(3 * BLK)[None, :]).astype(F32)
    slopes = jnp.exp2(-8.0 * jnp.arange(1, ATT_HEADS + 1, dtype=F32) / ATT_HEADS)
    bias = jnp.where(dist <= BLK, -slopes[:, None, None] * dist, NEG)
    prev = pl.BlockSpec((1, BLK, kvw), lambda b, n, *_: (b, jnp.maximum(n - 1, 0), 0))
    cur = pl.BlockSpec((1, BLK, kvw), lambda b, n, *_: (b, n, 0))
    nxt = pl.BlockSpec((1, BLK, kvw), lambda b, n, *_: (b, jnp.minimum(n + 1, nb - 1), 0))
    qspec = pl.BlockSpec((1, BLK, ATT_HEADS * ATT_DH), lambda b, n, *_: (b, n, 0))
    bspec = pl.BlockSpec((ATT_HEADS, BLK, 3 * BLK), lambda b, n, *_: (0, 0, 0))
    return pl.pallas_call(
        _attn_kernel,
        grid_spec=pltpu.PrefetchScalarGridSpec(
            num_scalar_prefetch=1, grid=(B, nb),
            in_specs=[qspec, prev, cur, nxt, prev, cur, nxt, bspec], out_specs=qspec),
        out_shape=jax.ShapeDtypeStruct(aq.shape, BF16),
        compiler_params=pltpu.CompilerParams(dimension_semantics=("arbitrary", "arbitrary"),
                                             vmem_limit_bytes=VMEM_LIMIT),
        name="win_attn",
    )(sink, aq, ak, ak, ak, av, av, av, bias)


def _mlstm_kernel(xf_ref, xb_ref, vf_ref, vb_ref, grf_ref, grb_ref, gcf_ref, gcb_ref, hf_ref, hb_ref, cn_ref, m_ref):
    c = pl.program_id(1)
    nc = pl.num_programs(1)
    L = BLK
    width = ML_HEADS * ML_DH

    @pl.when(c == 0)
    def _():
        cn_ref[...] = jnp.zeros_like(cn_ref)
        m_ref[...] = jnp.zeros_like(m_ref)

    tt = _iota((L, L), 0)
    ss = _iota((L, L), 1)
    ones_v = jnp.ones((L, ML_DH), BF16)

    def direction(d, qk_ref, v_ref, grow_ref, gcol_ref, out_ref):
        H = ML_HEADS
        keep = (ss <= tt) if d == 0 else (ss >= tt)
        g_row = grow_ref[0][H * d:H * (d + 1), :]
        gc = gcol_ref[0]
        kind = lambda i: gc[:, 2 * H * i + H * d:2 * H * i + H * (d + 1)]
        f, g_max, e_end, m_loc = kind(GC_F), kind(GC_GMAX), kind(GC_EEND), kind(GC_MLOC)[0:1, :]
        end = L - 1 if d == 0 else 0
        f_end = f[end:end + 1, :]
        m0 = m_ref[d, 0:1, 0:H]
        mm = jnp.maximum(m0, g_max)
        e_inter = jnp.exp(m0 - mm)
        floor = jnp.exp(-(f + mm))
        m_new = jnp.maximum(f_end + m0, m_loc)
        ca = jnp.exp(f_end + m0 - m_new)
        cb = jnp.exp(m_loc - m_new)
        m_ref[d, 0:1, 0:H] = m_new
        col = lambda a, h, w=ML_DH: jnp.broadcast_to(a[:, h:h + 1], (L, w))
        v_all = v_ref[0]
        for h in range(H):
            u = H * d + h
            hs = slice(h * ML_DH, (h + 1) * ML_DH)
            qb = qk_ref[0, :, hs]
            kb = qk_ref[0, :, width + h * ML_DH: width + (h + 1) * ML_DH]
            v1 = jnp.concatenate([v_all[:, hs], ones_v], axis=1)
            cn0 = cn_ref[u]
            dec = jnp.where(keep, jnp.exp(g_row[h:h + 1, :] - col(mm, h)), 0.0)
            s_qk = (_dot_nt(qb, kb) * dec).astype(BF16)
            tot = col(e_inter, h, 2 * ML_DH) * _dot(qb, cn0.astype(BF16)) + _dot(s_qk, v1)
            out_ref[0, :, hs] = tot[:, :ML_DH] / jnp.maximum(jnp.abs(tot[:, ML_DH:]), col(floor, h))
            ks = kb.astype(F32) * col(e_end, h)
            cn_ref[u] = ca[:, h:h + 1] * cn0 + cb[:, h:h + 1] * _dot(ks.T.astype(BF16), v1)

    direction(0, xf_ref, vf_ref, grf_ref, gcf_ref, hf_ref)
    direction(1, xb_ref, vb_ref, grb_ref, gcb_ref, hb_ref)


def _mlstm(mqk, mv, g_rows, g_cols):
    B, S, _ = mqk.shape
    nc = S // BLK
    width = ML_HEADS * ML_DH
    fwd = lambda c: c
    bwd = lambda c: nc - 1 - c
    qkspec = lambda ci: pl.BlockSpec((1, BLK, 2 * width), lambda b, c: (b, ci(c), 0))
    vspec = lambda ci: pl.BlockSpec((1, BLK, width), lambda b, c: (b, ci(c), 0))
    gspec = lambda ci: pl.BlockSpec((1, 2 * ML_HEADS, BLK), lambda b, c: (b, 0, ci(c)))
    cspec = lambda ci: pl.BlockSpec((1, BLK, LANES), lambda b, c: (b, ci(c), 0))
    units = 2 * ML_HEADS
    return pl.pallas_call(
        _mlstm_kernel,
        grid=(B, nc),
        in_specs=[qkspec(fwd), qkspec(bwd), vspec(fwd), vspec(bwd), gspec(fwd), gspec(bwd), cspec(fwd), cspec(bwd)],
        out_specs=[vspec(fwd), vspec(bwd)],
        out_shape=[jax.ShapeDtypeStruct((B, S, width), F32)] * 2,
        scratch_shapes=[pltpu.VMEM((units, ML_DH, 2 * ML_DH), F32), pltpu.VMEM((2, 8, LANES), F32)],
        compiler_params=pltpu.CompilerParams(dimension_semantics=("arbitrary", "arbitrary"),
                                             vmem_limit_bytes=VMEM_LIMIT),
        name="mlstm",
    )(mqk, mqk, mv, mv, g_rows, g_rows, g_cols, g_cols)


def _mix_kernel(x_ref, att_ref, hf_ref, hb_ref, mo_ref, og_ref, wo_ref,
                gm_ref, wq_ref, mqg_ref, k_ref, v_ref, wmo_ref,
                gf_ref, wr_ref, brt_ref,
                y_ref, hn_ref, afft_ref):
    x = x_ref[0]
    width = ML_HEADS * ML_DH
    ml = hf_ref[0] + hb_ref[0]
    mo = mo_ref[0].astype(F32)
    parts = []
    for h in range(ML_HEADS):
        sl = slice(h * ML_DH, (h + 1) * ML_DH)
        parts.append((_sigmoid(mo[:, sl]) * _rms(ml[:, sl], og_ref[:, sl])).astype(BF16))
    ml_out = jnp.concatenate(parts, axis=1)
    y1 = x + _dot(att_ref[0], wo_ref[0:width, :]) + _dot(ml_out, wo_ref[width:2 * width, :])
    h2 = _rms(y1, gm_ref[...]).astype(BF16)
    qm = _dot(h2, wq_ref[...])
    k = k_ref[0]
    v = v_ref[0]
    outs = []
    for h in range(MEM_HEADS):
        sl = slice(h * MEM_DH, (h + 1) * MEM_DH)
        qh = (_rms(qm[:, sl], mqg_ref[...]) * (MEM_DH ** -0.5)).astype(BF16)
        s = _dot_nt(qh, k[:, sl])
        p = jnp.exp(s - jnp.max(s, axis=-1, keepdims=True))
        o = _dot(p.astype(BF16), v[:, sl]) / jnp.sum(p, axis=-1, keepdims=True)
        outs.append(o.astype(BF16))
    y2 = y1 + _dot(jnp.concatenate(outs, axis=1), wmo_ref[...])
    h3 = _rms(y2, gf_ref[...])
    h_hi = h3.astype(BF16)
    h_lo = (h3 - h_hi.astype(F32)).astype(BF16)
    wr = wr_ref[...]
    w_hi = wr.astype(BF16)
    w_lo = (wr - w_hi.astype(F32)).astype(BF16)
    logits = _dot(h_hi, w_hi) + _dot(h_lo, w_hi) + _dot(h_hi, w_lo)
    logits_t = jnp.concatenate([logits[r0:r0 + LANES, :].T[:N_EXPERTS, :] for r0 in range(0, logits.shape[0], LANES)],
                               axis=1) + brt_ref[...]
    pt = jnp.exp(logits_t - jnp.max(logits_t, axis=0, keepdims=True))
    afft_ref[0] = pt / jnp.sum(pt, axis=0, keepdims=True)
    tm, chunks = y2.shape[0], y2.shape[1] // LANES
    for c in range(chunks):
        y_ref[pl.ds(c, tm, stride=chunks), :] = y2[:, c * LANES:(c + 1) * LANES]
        hn_ref[pl.ds(c, tm, stride=chunks), :] = h3[:, c * LANES:(c + 1) * LANES]


def _mix(x, att, hf, hb, mo, out_g, w_out, g_mem, w_q, mq_g, mem_k, mem_v, w_mo, g_ffn, w_r, b_r, tm):
    B, S, D = x.shape
    M = mem_k.shape[1]
    width = ML_HEADS * ML_DH
    mw = MEM_HEADS * MEM_DH
    full = lambda *s: pl.BlockSpec(s, lambda b, i: (0,) * len(s))
    tok = lambda w: pl.BlockSpec((1, tm, w), lambda b, i: (b, i, 0))
    chunks = D // LANES
    tiled = pl.BlockSpec((tm * chunks, LANES), lambda b, i: (b * (S // tm) + i, 0))
    memspec = pl.BlockSpec((1, M, mw), lambda b, i: (b, 0, 0))
    tiled_shape = jax.ShapeDtypeStruct((B * S * chunks, LANES), F32)
    return pl.pallas_call(
        _mix_kernel,
        grid=(B, S // tm),
        in_specs=[tok(D), tok(width), tok(width), tok(width), tok(width), full(1, width), full(2 * width, D),
                  full(1, D), full(D, mw), full(1, MEM_DH), memspec, memspec, full(mw, D),
                  full(1, D), full(D, LANES), full(N_EXPERTS, 1)],
        out_specs=[tiled, tiled, pl.BlockSpec((1, N_EXPERTS, tm), lambda b, i: (b, 0, i))],
        out_shape=[tiled_shape, tiled_shape, jax.ShapeDtypeStruct((B, N_EXPERTS, S), F32)],
        compiler_params=pltpu.CompilerParams(dimension_semantics=("arbitrary", "arbitrary"),
                                             vmem_limit_bytes=VMEM_LIMIT),
        name="mix_mem_router",
    )(x, att, hf, hb, mo, out_g, w_out, g_mem, w_q, mq_g, mem_k, mem_v, w_mo, g_ffn,
      jnp.pad(w_r, ((0, 0), (0, LANES - N_EXPERTS))), b_r.T)


def _topc_kernel(aff2_ref, aff3_ref, idx_ref, gval_ref, *, cap, seq, row_pitch):
    a2 = aff2_ref[0]
    bits2 = pltpu.bitcast(a2, I32)
    capf = float(cap)

    def bisect(i, lo):
        cand = lo | jnp.left_shift(jnp.int32(1), 30 - i)
        cnt = jnp.sum((bits2 >= cand).astype(F32), axis=1, keepdims=True)
        return jnp.where(cnt >= capf, cand, lo)

    thr_all = lax.fori_loop(0, 31, bisect, jnp.zeros((N_EXPERTS, 1), I32))
    need_all = capf - jnp.sum((bits2 > thr_all).astype(F32), axis=1, keepdims=True)

    T = aff3_ref.shape[2]
    tri_u = (_iota((LANES, LANES), 0) <= _iota((LANES, LANES), 1)).astype(BF16)
    tri_l = (_iota((LANES, LANES), 1) <= _iota((LANES, LANES), 0)).astype(BF16)
    ones8 = jnp.ones((8, LANES), BF16)
    before = _iota((T, T), 1) < _iota((T, T), 0)
    kcol = _iota((T, 1), 0).astype(F32)
    j = _iota((1, cap), 1).astype(F32)
    eye = (_iota((LANES, LANES), 0) == _iota((LANES, LANES), 1)).astype(BF16)
    lane_pos = _iota((LANES, cap), 0).astype(F32)

    def tile_starts(maskb):
        tot_row = _dot_nt(ones8, maskb)[0:1, :]
        return jnp.sum(jnp.where(before, tot_row, 0.0), axis=1, keepdims=True)

    for e in range(N_EXPERTS):
        bits = pltpu.bitcast(aff3_ref[0, e], I32)
        thr = thr_all[e:e + 1, :]
        need = need_all[e:e + 1, :]
        gt = bits > thr
        eq = bits == thr
        eqb = eq.astype(BF16)
        eq_rank = _dot(eqb, tri_u) + tile_starts(eqb) - eq.astype(F32)
        sel = gt | (eq & (eq_rank < need))
        selb = sel.astype(BF16)
        cs = _dot(selb, tri_u)
        start = tile_starts(selb)
        end = start + cs[:, LANES - 1:LANES]
        onehot = (start <= j) & (j < end)
        ohf = onehot.astype(F32)
        tile_of = jnp.sum(ohf * kcol, axis=0, keepdims=True)
        j_loc = j - jnp.sum(ohf * start, axis=0, keepdims=True)
        cs_t = _dot_nt(tri_l, selb)
        ohb = onehot.astype(BF16)
        r_t = _dot(cs_t.astype(BF16), ohb)
        local = jnp.sum((r_t <= j_loc).astype(F32), axis=0, keepdims=True)
        idx_ref[0, e:e + 1, :] = ((tile_of * LANES + local).astype(I32) + pl.program_id(0) * seq) * row_pitch
        a_t = _dot01_nt(eye, aff3_ref[0, e])
        a_tile = sum(_dot(term, ohb) for term in _split3(a_t))
        gval_ref[0, e:e + 1, :] = jnp.sum(jnp.where(lane_pos == local, a_tile, 0.0), axis=0, keepdims=True)


def _topc(aff_t, cap, row_pitch):
    B, E, S = aff_t.shape
    T = S // LANES
    aff3 = aff_t.reshape(B, E, T, LANES)
    return pl.pallas_call(
        functools.partial(_topc_kernel, cap=cap, seq=S, row_pitch=row_pitch),
        grid=(B,),
        in_specs=[pl.BlockSpec((1, E, S), lambda b: (b, 0, 0)), pl.BlockSpec((1, E, T, LANES), lambda b: (b, 0, 0, 0))],
        out_specs=[pl.BlockSpec((1, E, cap), lambda b: (b, 0, 0))] * 2,
        out_shape=[jax.ShapeDtypeStruct((B, E, cap), I32), jax.ShapeDtypeStruct((B, E, cap), F32)],
        compiler_params=pltpu.CompilerParams(dimension_semantics=("arbitrary",), vmem_limit_bytes=VMEM_LIMIT),
        name="topc",
    )(aff_t, aff3)


FFN_ROWS = 256
FFN_COLS = 256
AB_RING = 4


def _ffn_kernel(idxp_ref, idxn_ref, gv_ref, hn_hbm, acc_in_hbm, wg_ref, wu_ref, wd_ref, acc_hbm,
                xe_ref, ab_ref, sems, *, nb, cap, d_model, ff):
    del acc_in_hbm
    s = pl.program_id(0)
    last = pl.num_programs(0) - 1
    slot = s % 2
    a_cur = s % AB_RING
    a_nxt = (s + 1) % AB_RING
    a_old = (s + 2) % AB_RING
    a_prv = (s + 3) % AB_RING
    chunks = d_model // LANES

    def hbm_tile(ref, r):
        return ref.at[pl.ds(pl.multiple_of(r, chunks), chunks)]

    def vmem_tile(ref, sl, j):
        return ref.at[sl, pl.ds(pl.multiple_of(j * chunks, chunks), chunks)]

    def gather_x(r, j, sl):
        return pltpu.make_async_copy(hbm_tile(hn_hbm, r), vmem_tile(xe_ref, sl, j), sems.at[0])

    def gather_a(r, j, sl):
        return pltpu.make_async_copy(hbm_tile(acc_hbm, r), vmem_tile(ab_ref, sl, j), sems.at[1])

    def scatter(r, j, sl):
        return pltpu.make_async_copy(vmem_tile(ab_ref, sl, j), hbm_tile(acc_hbm, r), sems.at[2])

    def wait_x(sl):
        pltpu.make_async_copy(hn_hbm.at[pl.ds(0, cap * chunks)], xe_ref.at[sl], sems.at[0]).wait()

    def wait_a(sl):
        pltpu.make_async_copy(acc_hbm.at[pl.ds(0, cap * chunks)], ab_ref.at[sl], sems.at[1]).wait()

    def wait_scatter(sl):
        pltpu.make_async_copy(ab_ref.at[sl], acc_hbm.at[pl.ds(0, cap * chunks)], sems.at[2]).wait()

    @pl.when(s == 0)
    def _():
        @pl.loop(0, cap)
        def _(j):
            r = idxp_ref[0, 0, j]
            gather_x(r, j, 0).start()
            gather_a(r, j, 0).start()
            gather_a(r, j, AB_RING - 1).start()
        wait_a(AB_RING - 1)

    @pl.when(s > 0)
    def _():
        wait_scatter(a_old)

    wait_x(slot)
    wait_a(a_cur)

    halves = FFN_COLS // LANES
    nblk = cap // FFN_ROWS
    n_ct = ff // FFN_COLS
    assert d_model // FFN_COLS == n_ct

    n_updates = nblk * n_ct
    early = n_updates - n_ct if nblk > 1 else n_updates
    gather_rows = -(-cap // early)
    scatter_rows = -(-cap // n_updates)
    next_gather, next_scatter, n_issue = [0], [0], [0]

    def issue_rows():
        k = n_issue[0]
        n_issue[0] += 1
        lo, hi = next_gather[0], min(next_gather[0] + (gather_rows if k < early else 0), cap)
        next_gather[0] = hi
        for j in range(lo, hi):
            rn = idxn_ref[0, 0, j]
            gather_x(rn, j, 1 - slot).start(priority=0)
            gather_a(rn, j, a_nxt).start(priority=1)
        lo, hi = next_scatter[0], min(next_scatter[0] + scatter_rows, cap)
        next_scatter[0] = hi
        for j in range(lo, hi):
            scatter(idxp_ref[0, 0, j], j, a_prv).start(priority=j % 2)

    def chunk_rows(rb, c):
        return pl.ds(rb * FFN_ROWS * chunks + c, FFN_ROWS, stride=chunks)

    def gate_rows(rb):
        g_row = gv_ref[0, rb]
        return jnp.concatenate(
            [jnp.broadcast_to(g_row[:, i * LANES:(i + 1) * LANES], (LANES, LANES)).T for i in range(FFN_ROWS // LANES)],
            axis=0)

    hid_prev = None
    for stage in range(nblk + 1):
        up, dn = stage < nblk, stage >= 1
        if up:
            xb = jnp.concatenate([xe_ref[slot, chunk_rows(stage, c), :] for c in range(chunks)], axis=1).astype(BF16)
        if dn:
            gval = gate_rows(stage - 1)
        hid = []
        for ct in range(n_ct):
            cs = slice(ct * FFN_COLS, (ct + 1) * FFN_COLS)
            if up:
                hg = _dot(xb, wg_ref[0, :, cs])
                hu = _dot(xb, wu_ref[0, :, cs])
                hid.append((hg * _sigmoid(hg) * hu).astype(BF16))
            if dn:
                ye = _dot(hid_prev, wd_ref[0, :, cs])
                for i in range(halves):
                    ab_ref[a_cur, chunk_rows(stage - 1, ct * halves + i), :] += ye[:, i * LANES:(i + 1) * LANES] * gval
                issue_rows()
        hid_prev = jnp.concatenate(hid, axis=1) if up else None
    assert next_gather[0] == cap and next_scatter[0] == cap and n_issue[0] == n_updates

    @pl.when(s == last)
    def _():
        wait_x(1 - slot)
        wait_a(a_nxt)
        wait_scatter(a_prv)

        @pl.loop(0, cap)
        def _(j):
            scatter(idxn_ref[0, 0, j], j, a_cur).start()
        wait_scatter(a_cur)


def _expert_ffn(idx, gval, hn3, acc3, wg, wu, wd):
    B, E, cap = idx.shape
    d_model, ff = wg.shape[1], wg.shape[2]
    chunks = d_model // LANES
    assert B >= 3 and cap % FFN_ROWS == 0 and ff % FFN_COLS == 0 and d_model % FFN_COLS == 0
    ns = E * B
    nblk = cap // FFN_ROWS
    idx3 = idx.reshape(B * E, 1, cap)
    gv4 = gval.reshape(B * E, nblk, 1, FFN_ROWS)
    any_spec = pl.BlockSpec(memory_space=pl.ANY)
    wspec = lambda r, c: pl.BlockSpec((1, r, c), lambda s: (s // B, 0, 0))
    blk = lambda s: (s % B) * E + s // B
    ispec = lambda f: pl.BlockSpec((1, 1, cap), lambda s: (blk(f(s)), 0, 0), memory_space=pltpu.SMEM)
    kern = functools.partial(_ffn_kernel, nb=B, cap=cap, d_model=d_model, ff=ff)
    return pl.pallas_call(
        kern,
        grid=(ns,),
        in_specs=[ispec(lambda s: jnp.maximum(s - 1, 0)), ispec(lambda s: jnp.minimum(s + 1, ns - 1)),
                  pl.BlockSpec((1, nblk, 1, FFN_ROWS), lambda s: (blk(s), 0, 0, 0)),
                  any_spec, any_spec, wspec(d_model, ff), wspec(d_model, ff), wspec(ff, d_model)],
        out_specs=any_spec,
        out_shape=jax.ShapeDtypeStruct(acc3.shape, F32),
        scratch_shapes=[pltpu.VMEM((2, cap * chunks, LANES), F32), pltpu.VMEM((AB_RING, cap * chunks, LANES), F32),
                        pltpu.SemaphoreType.DMA((3,))],
        input_output_aliases={4: 0},
        compiler_params=pltpu.CompilerParams(dimension_semantics=("arbitrary",), vmem_limit_bytes=VMEM_LIMIT),
        name="expert_ffn",
    )(idx3, idx3, gv4, hn3, acc3, wg, wu, wd)


def _untile_kernel(a_ref, o_ref):
    tm, chunks = o_ref.shape[0], o_ref.shape[1] // LANES
    for c in range(chunks):
        o_ref[:, c * LANES:(c + 1) * LANES] = a_ref[pl.ds(c, tm, stride=chunks), :]


def _untile(a3, chunks, tm):
    N = a3.shape[0] // chunks
    return pl.pallas_call(
        _untile_kernel,
        grid=(N // tm,),
        in_specs=[pl.BlockSpec((tm * chunks, LANES), lambda i: (i, 0))],
        out_specs=pl.BlockSpec((tm, chunks * LANES), lambda i: (i, 0)),
        out_shape=jax.ShapeDtypeStruct((N, chunks * LANES), F32),
        compiler_params=pltpu.CompilerParams(dimension_semantics=("arbitrary",), vmem_limit_bytes=VMEM_LIMIT),
        name="untile",
    )(a3)


def _layer(x, mem, norm_mix_g, w_in, att_q_norm_g, att_k_norm_g, att_sink, ml_conv_w, ml_conv_b,
           ml_gate_b, ml_out_norm_g, w_out, norm_mem_g, mem_kv_norm_g, w_mem_q, w_mem_kv,
           mem_q_norm_g, mem_k_norm_g, w_mem_o, norm_ffn_g, w_router, b_router,
           w_exp_gate, w_exp_up, w_exp_down):
    B, S, D = x.shape
    row = lambda v: v.reshape(1, -1).astype(F32)
    tm_in = min(512, S)
    tm_mix = min(256, S)
    cap = CAPACITY_FACTOR * S // N_EXPERTS

    mem_k, mem_v = _mem_kv(mem, row(mem_kv_norm_g), w_mem_kv.astype(BF16), row(mem_k_norm_g))

    order = jnp.array(ATT_HEAD_ORDER)
    w_aq = w_in[:, _AQ[0]:_AQ[1]].reshape(D, ATT_HEADS, ATT_DH)[:, order].reshape(D, -1)
    w_main = jnp.concatenate([w_aq, w_in[:, _AQ[1]:_GATES[0]]], axis=1).astype(BF16)
    wg_t = w_in[:, _GATES[0]:_GATES[1]].T.astype(BF16)
    q_g = jnp.tile(row(att_q_norm_g), (1, ATT_HEADS)) * (ATT_DH ** -0.5)
    k_g = jnp.tile(row(att_k_norm_g), (1, ATT_KV))
    aq, ak, av, mqk, mv, mo, g_rows, g_cols = _inproj(x, row(norm_mix_g), w_main, wg_t, ml_gate_b.reshape(-1, 1).astype(F32),
                                               q_g, k_g, ml_conv_w.astype(F32), row(ml_conv_b), tm_in)

    att = _attention(att_sink.astype(F32), aq, ak, av)
    hf, hb = _mlstm(mqk, mv, g_rows, g_cols)

    att_w = ATT_HEADS * ATT_DH
    w_out_att = w_out[:att_w].reshape(ATT_HEADS, ATT_DH, D)[order].reshape(att_w, D)
    w_out_p = jnp.concatenate([w_out_att, w_out[att_w:]], axis=0).astype(BF16)
    y3, hn3, aff_t = _mix(x, att, hf, hb, mo, row(ml_out_norm_g), w_out_p, row(norm_mem_g),
                             w_mem_q.astype(BF16), row(mem_q_norm_g), mem_k, mem_v, w_mem_o.astype(BF16),
                             row(norm_ffn_g), w_router.astype(F32), row(b_router), tm_mix)

    idx, gval = _topc(aff_t, cap, D // LANES)
    out3 = _expert_ffn(idx, gval, hn3, y3, w_exp_gate.astype(BF16), w_exp_up.astype(BF16), w_exp_down.astype(BF16))
    return _untile(out3, D // LANES, tm_in).reshape(B, S, D)


def kernel(x, mem, norm_mix_g, w_in, att_q_norm_g, att_k_norm_g, att_sink, ml_conv_w, ml_conv_b, ml_gate_b,
           ml_out_norm_g, w_out, norm_mem_g, mem_kv_norm_g, w_mem_q, w_mem_kv, mem_q_norm_g, mem_k_norm_g,
           w_mem_o, norm_ffn_g, w_router, b_router, w_exp_gate, w_exp_up, w_exp_down):
    params = (norm_mix_g, w_in, att_q_norm_g, att_k_norm_g, att_sink, ml_conv_w, ml_conv_b, ml_gate_b,
              ml_out_norm_g, w_out, norm_mem_g, mem_kv_norm_g, w_mem_q, w_mem_kv, mem_q_norm_g, mem_k_norm_g,
              w_mem_o, norm_ffn_g, w_router, b_router, w_exp_gate, w_exp_up, w_exp_down)
    depth = norm_mix_g.shape[0]
    for l in range(depth):
        x = _layer(x, mem, *[p[l] for p in params])
    return x
```

```python
import functools

import jax
import jax.numpy as jnp
from jax import lax
from jax.experimental import pallas as pl
from jax.experimental.pallas import tpu as pltpu

F32 = jnp.float32
BF16 = jnp.bfloat16
I32 = jnp.int32

EPS = 1e-6
LANES = 128
BLK = 128
ATT_HEADS, ATT_KV, ATT_DH = 8, 2, 64
ML_HEADS, ML_DH = 4, 128
MEM_HEADS, MEM_DH = 4, 128
N_EXPERTS = 16
CAPACITY_FACTOR = 2
NEG = -1e30
HALO_X = 8
VMEM_LIMIT = 48 * 1024 * 1024

_NT = (((1,), (1,)), ((), ()))


def _dot(a, b):
    return jnp.dot(a, b, preferred_element_type=F32)


def _dot_nt(a, b):
    return lax.dot_general(a, b, _NT, preferred_element_type=F32)


def _split3(x):
    hi = x.astype(BF16)
    r1 = x - hi.astype(F32)
    mid = r1.astype(BF16)
    lo = (r1 - mid.astype(F32)).astype(BF16)
    return hi, mid, lo


def _dot01_nt(m01, x):
    hi, mid, lo = _split3(x)
    return _dot_nt(m01, hi) + _dot_nt(m01, mid) + _dot_nt(m01, lo)


def _dot01(x, m01):
    hi, mid, lo = _split3(x)
    return _dot(hi, m01) + _dot(mid, m01) + _dot(lo, m01)


def _rms(x, g):
    ms = jnp.mean(x * x, axis=-1, keepdims=True)
    return x * lax.rsqrt(ms + EPS) * g


def _sigmoid(x):
    return 1.0 / (1.0 + jnp.exp(-x))


def _iota(shape, dim):
    return lax.broadcasted_iota(I32, shape, dim)


def _mem_kv_kernel(mem_ref, g_ref, w_ref, kg_ref, k_ref, v_ref):
    mn = _rms(mem_ref[0], g_ref[...]).astype(BF16)
    kv = _dot(mn, w_ref[...])
    width = MEM_HEADS * MEM_DH
    for h in range(MEM_HEADS):
        sl = slice(h * MEM_DH, (h + 1) * MEM_DH)
        k_ref[0, :, sl] = _rms(kv[:, sl], kg_ref[...]).astype(BF16)
    v_ref[0] = kv[:, width:].astype(BF16)


def _mem_kv(mem, g, w_kv, k_g):
    B, M, D = mem.shape
    width = MEM_HEADS * MEM_DH
    full = lambda *s: pl.BlockSpec(s, lambda b: (0,) * len(s))
    return pl.pallas_call(
        _mem_kv_kernel,
        grid=(B,),
        in_specs=[pl.BlockSpec((1, M, D), lambda b: (b, 0, 0)), full(1, D), full(D, 2 * width), full(1, MEM_DH)],
        out_specs=[pl.BlockSpec((1, M, width), lambda b: (b, 0, 0))] * 2,
        out_shape=[jax.ShapeDtypeStruct((B, M, width), BF16)] * 2,
        compiler_params=pltpu.CompilerParams(dimension_semantics=("arbitrary",), vmem_limit_bytes=VMEM_LIMIT),
        name="mem_kv",
    )(mem, g, w_kv, k_g)


_AQ = (0, 512)
_AK = (512, 640)
_AV = (640, 768)
_MQK = (768, 1792)
_MV = (1792, 2304)
_MO = (2304, 2816)
_GATES = (2816, 2832)


GC_F, GC_GMAX, GC_EEND, GC_MLOC = 0, 1, 2, 3


def _inproj_kernel(x_ref, xp_ref, xn_ref, g_ref, w_ref, wgt_ref, gb_ref, qg_ref, kg_ref, bdq_ref, bdk_ref, trif_ref,
                   trib_ref, blk_ref, cw_ref, cb_ref, aq_ref, ak_ref, av_ref, mqk_ref, mv_ref, mo_ref, grow_ref, gcol_ref):
    h = _rms(x_ref[0], g_ref[...]).astype(BF16)
    gt = _dot_nt(wgt_ref[...], h) + gb_ref[...]
    logsig = jnp.minimum(gt, 0.0) - jnp.log(1.0 + jnp.exp(-jnp.abs(gt)))
    tm = gt.shape[1]
    H = ML_HEADS
    pos = _iota((H, tm), 1) % BLK

    def chunk_scan(li, lf, tri_ref, fwd):
        f = _dot01(lf, tri_ref[...])
        g = li - f
        gmax = g
        sh = 1
        while sh < BLK:
            if fwd:
                gmax = jnp.where(pos >= sh, jnp.maximum(gmax, pltpu.roll(gmax, sh, axis=1)), gmax)
            else:
                gmax = jnp.where(pos < BLK - sh, jnp.maximum(gmax, pltpu.roll(gmax, tm - sh, axis=1)), gmax)
            sh *= 2
        w_end = _dot01(lf, blk_ref[...]) + g
        m_loc = jnp.concatenate(
            [jnp.broadcast_to(jnp.max(w_end[:, c0:c0 + BLK], axis=1, keepdims=True), (H, BLK))
             for c0 in range(0, tm, BLK)], axis=1)
        return f, g, gmax, jnp.exp(w_end - m_loc), m_loc

    f_f, g_f, gm_f, ee_f, ml_f = chunk_scan(gt[0:H], logsig[H:2 * H], trif_ref, True)
    f_b, g_b, gm_b, ee_b, ml_b = chunk_scan(gt[2 * H:3 * H], logsig[3 * H:4 * H], trib_ref, False)
    grow_ref[0] = jnp.concatenate([g_f, g_b], axis=0)
    kinds = [None] * 4
    kinds[GC_F], kinds[GC_GMAX], kinds[GC_EEND], kinds[GC_MLOC] = (f_f, f_b), (gm_f, gm_b), (ee_f, ee_b), (ml_f, ml_b)
    rows = jnp.concatenate([r for pair in kinds for r in pair], axis=0)
    pad = jnp.zeros((LANES - rows.shape[0], BLK), F32)
    for c0 in range(0, tm, BLK):
        gcol_ref[0, c0:c0 + BLK, :] = jnp.concatenate([rows[:, c0:c0 + BLK], pad], axis=0).T
    sec = lambda s: _dot(h, w_ref[:, s[0]:s[1]])
    aq = sec(_AQ)
    ssq = _dot((aq * aq).astype(BF16), bdq_ref[...])
    aq_ref[0] = (aq * lax.rsqrt(ssq * (1.0 / ATT_DH) + EPS) * qg_ref[...]).astype(BF16)
    ak = sec(_AK)
    ssk = _dot((ak * ak).astype(BF16), bdk_ref[...])
    ak_ref[0] = (ak * lax.rsqrt(ssk * (1.0 / ATT_DH) + EPS) * kg_ref[...]).astype(BF16)
    av_ref[0] = sec(_AV).astype(BF16)
    i, ni = pl.program_id(1), pl.num_programs(1)
    h_halo = jnp.concatenate([_rms(xp_ref[0], g_ref[...]), _rms(xn_ref[0], g_ref[...])], axis=0).astype(BF16)
    halo = _dot(h_halo, w_ref[:, _MQK[0]:_MQK[1]])
    before = jnp.where(i == 0, 0.0, halo[HALO_X - 1:HALO_X, :])
    after = jnp.where(i == ni - 1, 0.0, halo[HALO_X:HALO_X + 1, :])
    raw = sec(_MQK)
    r = _iota((tm, 1), 0)
    x_prev = jnp.where(r == 0, before, pltpu.roll(raw, 1, axis=0))
    x_next = jnp.where(r == tm - 1, after, pltpu.roll(raw, tm - 1, axis=0))
    y = cw_ref[0:1, :] * x_prev + cw_ref[1:2, :] * raw + cw_ref[2:3, :] * x_next + cb_ref[...]
    k_scale = jnp.where(_iota((1, y.shape[1]), 1) < ML_HEADS * ML_DH, 1.0, ML_DH ** -0.5)
    mqk_ref[0] = (y * _sigmoid(y) * k_scale).astype(BF16)
    mv_ref[0] = sec(_MV).astype(BF16)
    mo_ref[0] = sec(_MO).astype(BF16)


def _inproj(x, g, w_main, wg_t, gate_b, q_g, k_g, conv_w, conv_b, tm):
    B, S, D = x.shape
    per = tm // HALO_X
    last = S // HALO_X - 1
    bdq = (jnp.arange(512)[:, None] // ATT_DH == jnp.arange(512)[None, :] // ATT_DH).astype(BF16)
    bdk = bdq[:128, :128]
    s_from, s_to = jnp.arange(tm)[:, None], jnp.arange(tm)[None, :]
    same = s_from // BLK == s_to // BLK
    tri_f = (same & (s_from <= s_to)).astype(BF16)
    tri_b = (same & (s_from >= s_to)).astype(BF16)
    full = lambda *s: pl.BlockSpec(s, lambda b, i: (0,) * len(s))
    tok = lambda w: pl.BlockSpec((1, tm, w), lambda b, i: (b, i, 0))
    widths = (512, 128, 128, 1024, 512, 512)
    return pl.pallas_call(
        _inproj_kernel,
        grid=(B, S // tm),
        in_specs=[tok(D),
                  pl.BlockSpec((1, HALO_X, D), lambda b, i: (b, jnp.maximum(i * per - 1, 0), 0)),
                  pl.BlockSpec((1, HALO_X, D), lambda b, i: (b, jnp.minimum((i + 1) * per, last), 0)),
                  full(1, D), full(D, w_main.shape[1]), full(16, D), full(16, 1),
                  full(1, 512), full(1, 128), full(512, 512), full(128, 128), full(tm, tm), full(tm, tm), full(tm, tm),
                  full(3, 2 * ML_HEADS * ML_DH), full(1, 2 * ML_HEADS * ML_DH)],
        out_specs=[tok(w) for w in widths] + [pl.BlockSpec((1, 2 * ML_HEADS, tm), lambda b, i: (b, 0, i)), tok(LANES)],
        out_shape=[jax.ShapeDtypeStruct((B, S, w), BF16) for w in widths]
        + [jax.ShapeDtypeStruct((B, 2 * ML_HEADS, S), F32), jax.ShapeDtypeStruct((B, S, LANES), F32)],
        compiler_params=pltpu.CompilerParams(dimension_semantics=("arbitrary", "arbitrary"),
                                             vmem_limit_bytes=VMEM_LIMIT),
        name="inproj",
    )(x, x, x, g, w_main, wg_t, gate_b, q_g, k_g, bdq, bdk, tri_f, tri_b, same.astype(BF16), conv_w, conv_b)


ATT_GROUP = ATT_HEADS // ATT_KV
ATT_HEAD_ORDER = tuple(kv * ATT_GROUP + g for g in range(ATT_GROUP) for kv in range(ATT_KV))


def _attn_kernel(sink_ref, q_ref, kp_ref, kc_ref, kn_ref, vp_ref, vc_ref, vn_ref, bias_ref, o_ref):
    n = pl.program_id(1)
    nb = pl.num_programs(1)
    G = ATT_GROUP
    kband = jnp.concatenate([kp_ref[0], kc_ref[0], kn_ref[0]], axis=0)
    vband = jnp.concatenate([vp_ref[0], vc_ref[0], vn_ref[0]], axis=0)
    first = _iota(kband.shape, 1) < ATT_DH
    kband, vband = kband.astype(F32), vband.astype(F32)
    keep_lanes = lambda a, mine: jnp.where(mine, a, 0.0).astype(BF16)
    q = q_ref[0]
    qs = jnp.concatenate([q[:, g * LANES:(g + 1) * LANES] for g in range(G)], axis=0)
    si = _iota((1, 3 * BLK), 1)
    edge = jnp.where(((si < BLK) & (n == 0)) | ((si >= 2 * BLK) & (n == nb - 1)), NEG, 0.0)
    tot = None
    sink_terms = []
    for kv in range(ATT_KV):
        mine = first if kv == 0 else ~first
        s = _dot_nt(qs, keep_lanes(kband, mine))
        vaug = jnp.concatenate([keep_lanes(vband, mine), mine.astype(F32).astype(BF16)], axis=1)
        ps, st = [], []
        for g in range(G):
            h = kv * G + g
            sg = s[g * BLK:(g + 1) * BLK] + bias_ref[h] + edge
            sink = sink_ref[h]
            m = jnp.maximum(jnp.max(sg, axis=-1, keepdims=True), sink)
            ps.append(jnp.exp(sg - m).astype(BF16))
            st.append(jnp.exp(sink - m))
        part = _dot(jnp.concatenate(ps, axis=0), vaug)
        tot = part if tot is None else tot + part
        sink_terms.append(jnp.concatenate(st, axis=0))
    lane_first = _iota((1, LANES), 1) < ATT_DH
    den = tot[:, LANES:] + jnp.where(lane_first, sink_terms[0], sink_terms[1])
    out = (tot[:, :LANES] / den).astype(BF16)
    o_ref[0] = jnp.concatenate([out[g * BLK:(g + 1) * BLK] for g in range(G)], axis=1)


def _attention(sink, aq, ak, av):
    B, S, _ = aq.shape
    nb = S // BLK
    kvw = ATT_KV * ATT_DH
    assert kvw == LANES
    dist = jnp.abs(jnp.arange(BLK)[:, None] + BLK - jnp.arange(3 * BLK)[None, :]).astype(F32)
    slopes = jnp.exp2(-8.0 * jnp.arange(1, ATT_HEADS + 1, dtype=F32) / ATT_HEADS)
    bias = jnp.where(dist <= BLK, -slopes[:, None, None] * dist, NEG)
    prev = pl.BlockSpec((1, BLK, kvw), lambda b, n, *_: (b, jnp.maximum(n - 1, 0), 0))
    cur = pl.BlockSpec((1, BLK, kvw), lambda b, n, *_: (b, n, 0))
    nxt = pl.BlockSpec((1, BLK, kvw), lambda b, n, *_: (b, jnp.minimum(n + 1, nb - 1), 0))
    qspec = pl.BlockSpec((1, BLK, ATT_HEADS * ATT_DH), lambda b, n, *_: (b, n, 0))
    bspec = pl.BlockSpec((ATT_HEADS, BLK, 3 * BLK), lambda b, n, *_: (0, 0, 0))
    return pl.pallas_call(
        _attn_kernel,
        grid_spec=pltpu.PrefetchScalarGridSpec(
            num_scalar_prefetch=1, grid=(B, nb),
            in_specs=[qspec, prev, cur, nxt, prev, cur, nxt, bspec], out_specs=qspec),
        out_shape=jax.ShapeDtypeStruct(aq.shape, BF16),
        compiler_params=pltpu.CompilerParams(dimension_semantics=("arbitrary", "arbitrary"),
                                             vmem_limit_bytes=VMEM_LIMIT),
        name="win_attn",
    )(sink, aq, ak, ak, ak, av, av, av, bias)


def _mlstm_kernel(xf_ref, xb_ref, vf_ref, vb_ref, grf_ref, grb_ref, gcf_ref, gcb_ref, hf_ref, hb_ref, cn_ref, m_ref):
    c = pl.program_id(1)
    nc = pl.num_programs(1)
    L = BLK
    width = ML_HEADS * ML_DH

    @pl.when(c == 0)
    def _():
        cn_ref[...] = jnp.zeros_like(cn_ref)
        m_ref[...] = jnp.zeros_like(m_ref)

    tt = _iota((L, L), 0)
    ss = _iota((L, L), 1)
    ones_v = jnp.ones((L, ML_DH), BF16)

    def direction(d, qk_ref, v_ref, grow_ref, gcol_ref, out_ref):
        H = ML_HEADS
        keep = (ss <= tt) if d == 0 else (ss >= tt)
        g_row = grow_ref[0][H * d:H * (d + 1), :]
        gc = gcol_ref[0]
        kind = lambda i: gc[:, 2 * H * i + H * d:2 * H * i + H * (d + 1)]
        f, g_max, e_end, m_loc = kind(GC_F), kind(GC_GMAX), kind(GC_EEND), kind(GC_MLOC)[0:1, :]
        end = L - 1 if d == 0 else 0
        f_end = f[end:end + 1, :]
        m0 = m_ref[d, 0:1, 0:H]
        mm = jnp.maximum(m0, g_max)
        e_inter = jnp.exp(m0 - mm)
        floor = jnp.exp(-(f + mm))
        m_new = jnp.maximum(f_end + m0, m_loc)
        ca = jnp.exp(f_end + m0 - m_new)
        cb = jnp.exp(m_loc - m_new)
        m_ref[d, 0:1, 0:H] = m_new
        col = lambda a, h, w=ML_DH: jnp.broadcast_to(a[:, h:h + 1], (L, w))
        v_all = v_ref[0]
        for h in range(H):
            u = H * d + h
            hs = slice(h * ML_DH, (h + 1) * ML_DH)
            qb = qk_ref[0, :, hs]
            kb = qk_ref[0, :, width + h * ML_DH: width + (h + 1) * ML_DH]
            v1 = jnp.concatenate([v_all[:, hs], ones_v], axis=1)
            cn0 = cn_ref[u]
            dec = jnp.where(keep, jnp.exp(g_row[h:h + 1, :] - col(mm, h)), 0.0)
            s_qk = (_dot_nt(qb, kb) * dec).astype(BF16)
            tot = col(e_inter, h, 2 * ML_DH) * _dot(qb, cn0.astype(BF16)) + _dot(s_qk, v1)
            out_ref[0, :, hs] = tot[:, :ML_DH] / jnp.maximum(jnp.abs(tot[:, ML_DH:]), col(floor, h))
            ks = kb.astype(F32) * col(e_end, h)
            cn_ref[u] = ca[:, h:h + 1] * cn0 + cb[:, h:h + 1] * _dot(ks.T.astype(BF16), v1)

    direction(0, xf_ref, vf_ref, grf_ref, gcf_ref, hf_ref)
    direction(1, xb_ref, vb_ref, grb_ref, gcb_ref, hb_ref)


def _mlstm(mqk, mv, g_rows, g_cols):
    B, S, _ = mqk.shape
    nc = S // BLK
    width = ML_HEADS * ML_DH
    fwd = lambda c: c
    bwd = lambda c: nc - 1 - c
    qkspec = lambda ci: pl.BlockSpec((1, BLK, 2 * width), lambda b, c: (b, ci(c), 0))
    vspec = lambda ci: pl.BlockSpec((1, BLK, width), lambda b, c: (b, ci(c), 0))
    gspec = lambda ci: pl.BlockSpec((1, 2 * ML_HEADS, BLK), lambda b, c: (b, 0, ci(c)))
    cspec = lambda ci: pl.BlockSpec((1, BLK, LANES), lambda b, c: (b, ci(c), 0))
    units = 2 * ML_HEADS
    return pl.pallas_call(
        _mlstm_kernel,
        grid=(B, nc),
        in_specs=[qkspec(fwd), qkspec(bwd), vspec(fwd), vspec(bwd), gspec(fwd), gspec(bwd), cspec(fwd), cspec(bwd)],
        out_specs=[vspec(fwd), vspec(bwd)],
        out_shape=[jax.ShapeDtypeStruct((B, S, width), F32)] * 2,
        scratch_shapes=[pltpu.VMEM((units, ML_DH, 2 * ML_DH), F32), pltpu.VMEM((2, 8, LANES), F32)],
        compiler_params=pltpu.CompilerParams(dimension_semantics=("arbitrary", "arbitrary"),
                                             vmem_limit_bytes=VMEM_LIMIT),
        name="mlstm",
    )(mqk, mqk, mv, mv, g_rows, g_rows, g_cols, g_cols)


def _mix_kernel(x_ref, att_ref, hf_ref, hb_ref, mo_ref, og_ref, wo_ref,
                gm_ref, wq_ref, mqg_ref, k_ref, v_ref, wmo_ref,
                gf_ref, wr_ref, brt_ref,
                slab_ref, afft_ref):
    x = x_ref[0]
    width = ML_HEADS * ML_DH
    ml = hf_ref[0] + hb_ref[0]
    mo = mo_ref[0].astype(F32)
    parts = []
    for h in range(ML_HEADS):
        sl = slice(h * ML_DH, (h + 1) * ML_DH)
        parts.append((_sigmoid(mo[:, sl]) * _rms(ml[:, sl], og_ref[:, sl])).astype(BF16))
    ml_out = jnp.concatenate(parts, axis=1)
    y1 = x + _dot(att_ref[0], wo_ref[0:width, :]) + _dot(ml_out, wo_ref[width:2 * width, :])
    h2 = _rms(y1, gm_ref[...]).astype(BF16)
    qm = _dot(h2, wq_ref[...])
    k = k_ref[0]
    v = v_ref[0]
    outs = []
    for h in range(MEM_HEADS):
        sl = slice(h * MEM_DH, (h + 1) * MEM_DH)
        qh = (_rms(qm[:, sl], mqg_ref[...]) * (MEM_DH ** -0.5)).astype(BF16)
        s = _dot_nt(qh, k[:, sl])
        p = jnp.exp(s - jnp.max(s, axis=-1, keepdims=True))
        o = _dot(p.astype(BF16), v[:, sl]) / jnp.sum(p, axis=-1, keepdims=True)
        outs.append(o.astype(BF16))
    y2 = y1 + _dot(jnp.concatenate(outs, axis=1), wmo_ref[...])
    h3 = _rms(y2, gf_ref[...])
    h_hi = h3.astype(BF16)
    h_lo = (h3 - h_hi.astype(F32)).astype(BF16)
    wr = wr_ref[...]
    w_hi = wr.astype(BF16)
    w_lo = (wr - w_hi.astype(F32)).astype(BF16)
    logits = _dot(h_hi, w_hi) + _dot(h_lo, w_hi) + _dot(h_hi, w_lo)
    logits_t = jnp.concatenate([logits[r0:r0 + LANES, :].T[:N_EXPERTS, :] for r0 in range(0, logits.shape[0], LANES)],
                               axis=1) + brt_ref[...]
    pt = jnp.exp(logits_t - jnp.max(logits_t, axis=0, keepdims=True))
    afft_ref[0] = pt / jnp.sum(pt, axis=0, keepdims=True)
    tm, chunks = y2.shape[0], y2.shape[1] // LANES
    for c in range(chunks):
        slab_ref[SLAB_ACC, pl.ds(c, tm, stride=chunks), :] = y2[:, c * LANES:(c + 1) * LANES]
        slab_ref[SLAB_X, pl.ds(c, tm, stride=chunks), :] = h3[:, c * LANES:(c + 1) * LANES]


def _mix(x, att, hf, hb, mo, out_g, w_out, g_mem, w_q, mq_g, mem_k, mem_v, w_mo, g_ffn, w_r, b_r, tm):
    B, S, D = x.shape
    M = mem_k.shape[1]
    width = ML_HEADS * ML_DH
    mw = MEM_HEADS * MEM_DH
    full = lambda *s: pl.BlockSpec(s, lambda b, i: (0,) * len(s))
    tok = lambda w: pl.BlockSpec((1, tm, w), lambda b, i: (b, i, 0))
    chunks = D // LANES
    tiled = pl.BlockSpec((2, tm * chunks, LANES), lambda b, i: (0, b * (S // tm) + i, 0))
    memspec = pl.BlockSpec((1, M, mw), lambda b, i: (b, 0, 0))
    tiled_shape = jax.ShapeDtypeStruct((2, B * S * chunks, LANES), F32)
    return pl.pallas_call(
        _mix_kernel,
        grid=(B, S // tm),
        in_specs=[tok(D), tok(width), tok(width), tok(width), tok(width), full(1, width), full(2 * width, D),
                  full(1, D), full(D, mw), full(1, MEM_DH), memspec, memspec, full(mw, D),
                  full(1, D), full(D, LANES), full(N_EXPERTS, 1)],
        out_specs=[tiled, pl.BlockSpec((1, N_EXPERTS, tm), lambda b, i: (b, 0, i))],
        out_shape=[tiled_shape, jax.ShapeDtypeStruct((B, N_EXPERTS, S), F32)],
        compiler_params=pltpu.CompilerParams(dimension_semantics=("arbitrary", "arbitrary"),
                                             vmem_limit_bytes=VMEM_LIMIT),
        name="mix_mem_router",
    )(x, att, hf, hb, mo, out_g, w_out, g_mem, w_q, mq_g, mem_k, mem_v, w_mo, g_ffn,
      jnp.pad(w_r, ((0, 0), (0, LANES - N_EXPERTS))), b_r.T)


def _topc_kernel(aff2_ref, aff3_ref, idx_ref, gval_ref, *, cap, seq, row_pitch):
    a2 = aff2_ref[0]
    bits2 = pltpu.bitcast(a2, I32)
    capf = float(cap)

    def bisect(i, lo):
        cand = lo | jnp.left_shift(jnp.int32(1), 30 - i)
        cnt = jnp.sum((bits2 >= cand).astype(F32), axis=1, keepdims=True)
        return jnp.where(cnt >= capf, cand, lo)

    thr_all = lax.fori_loop(0, 31, bisect, jnp.zeros((N_EXPERTS, 1), I32))
    need_all = capf - jnp.sum((bits2 > thr_all).astype(F32), axis=1, keepdims=True)

    T = aff3_ref.shape[2]
    tri_u = (_iota((LANES, LANES), 0) <= _iota((LANES, LANES), 1)).astype(BF16)
    tri_l = (_iota((LANES, LANES), 1) <= _iota((LANES, LANES), 0)).astype(BF16)
    ones8 = jnp.ones((8, LANES), BF16)
    before = _iota((T, T), 1) < _iota((T, T), 0)
    kcol = _iota((T, 1), 0).astype(F32)
    j = _iota((1, cap), 1).astype(F32)
    eye = (_iota((LANES, LANES), 0) == _iota((LANES, LANES), 1)).astype(BF16)
    lane_pos = _iota((LANES, cap), 0).astype(F32)

    def tile_starts(maskb):
        tot_row = _dot_nt(ones8, maskb)[0:1, :]
        return jnp.sum(jnp.where(before, tot_row, 0.0), axis=1, keepdims=True)

    for e in range(N_EXPERTS):
        bits = pltpu.bitcast(aff3_ref[0, e], I32)
        thr = thr_all[e:e + 1, :]
        need = need_all[e:e + 1, :]
        gt = bits > thr
        eq = bits == thr
        eqb = eq.astype(BF16)
        eq_rank = _dot(eqb, tri_u) + tile_starts(eqb) - eq.astype(F32)
        sel = gt | (eq & (eq_rank < need))
        selb = sel.astype(BF16)
        cs = _dot(selb, tri_u)
        start = tile_starts(selb)
        end = start + cs[:, LANES - 1:LANES]
        onehot = (start <= j) & (j < end)
        ohf = onehot.astype(F32)
        tile_of = jnp.sum(ohf * kcol, axis=0, keepdims=True)
        j_loc = j - jnp.sum(ohf * start, axis=0, keepdims=True)
        cs_t = _dot_nt(tri_l, selb)
        ohb = onehot.astype(BF16)
        r_t = _dot(cs_t.astype(BF16), ohb)
        local = jnp.sum((r_t <= j_loc).astype(F32), axis=0, keepdims=True)
        idx_ref[0, e:e + 1, :] = ((tile_of * LANES + local).astype(I32) + pl.program_id(0) * seq) * row_pitch
        a_t = _dot01_nt(eye, aff3_ref[0, e])
        a_tile = sum(_dot(term, ohb) for term in _split3(a_t))
        gval_ref[0, e:e + 1, :] = jnp.sum(jnp.where(lane_pos == local, a_tile, 0.0), axis=0, keepdims=True)


def _topc(aff_t, cap, row_pitch):
    B, E, S = aff_t.shape
    T = S // LANES
    aff3 = aff_t.reshape(B, E, T, LANES)
    return pl.pallas_call(
        functools.partial(_topc_kernel, cap=cap, seq=S, row_pitch=row_pitch),
        grid=(B,),
        in_specs=[pl.BlockSpec((1, E, S), lambda b: (b, 0, 0)), pl.BlockSpec((1, E, T, LANES), lambda b: (b, 0, 0, 0))],
        out_specs=[pl.BlockSpec((1, E, cap), lambda b: (b, 0, 0))] * 2,
        out_shape=[jax.ShapeDtypeStruct((B, E, cap), I32), jax.ShapeDtypeStruct((B, E, cap), F32)],
        compiler_params=pltpu.CompilerParams(dimension_semantics=("arbitrary",), vmem_limit_bytes=VMEM_LIMIT),
        name="topc",
    )(aff_t, aff3)


FFN_ROWS = 256
FFN_COLS = 256
AB_RING = 3


SLAB_X, SLAB_ACC = 0, 1


def _ffn_kernel(idxp_ref, idxn_ref, gv_ref, slab_in_hbm, wg_ref, wu_ref, wd_ref, slab_hbm,
                buf_ref, sems, *, nb, cap, d_model, ff):
    del slab_in_hbm
    s = pl.program_id(0)
    last = pl.num_programs(0) - 1
    a_cur = s % AB_RING
    a_nxt = (s + 1) % AB_RING
    a_prv = (s + 2) % AB_RING
    chunks = d_model // LANES

    def rows_of(start):
        return pl.ds(pl.multiple_of(start, chunks), chunks)

    def gather(r, j, sl):
        return pltpu.make_async_copy(slab_hbm.at[:, rows_of(r)], buf_ref.at[sl, :, rows_of(j * chunks)], sems.at[0])

    def scatter(r, j, sl):
        return pltpu.make_async_copy(buf_ref.at[sl, SLAB_ACC, rows_of(j * chunks)], slab_hbm.at[SLAB_ACC, rows_of(r)],
                                     sems.at[1])

    def wait_gather(sl):
        pltpu.make_async_copy(slab_hbm.at[:, pl.ds(0, cap * chunks)], buf_ref.at[sl], sems.at[0]).wait()

    def wait_scatter(sl):
        pltpu.make_async_copy(buf_ref.at[sl, SLAB_ACC], slab_hbm.at[SLAB_ACC, pl.ds(0, cap * chunks)],
                              sems.at[1]).wait()

    @pl.when(s == 0)
    def _():
        @pl.loop(0, cap)
        def _(j):
            r = idxp_ref[0, 0, j]
            gather(r, j, 0).start()
            gather(r, j, AB_RING - 1).start()
        wait_gather(AB_RING - 1)

    wait_gather(a_cur)

    halves = FFN_COLS // LANES
    nblk = cap // FFN_ROWS
    n_ct = ff // FFN_COLS
    assert d_model // FFN_COLS == n_ct

    n_updates = nblk * n_ct
    early = n_updates - n_ct if nblk > 1 else n_updates
    gather_rows = -(-cap // early)
    scatter_rows = -(-cap // n_updates)
    next_gather, next_scatter, n_issue = [0], [0], [0]

    def issue_rows():
        k = n_issue[0]
        n_issue[0] += 1
        lo, hi = next_gather[0], min(next_gather[0] + (gather_rows if k < early else 0), cap)
        next_gather[0] = hi
        for j in range(lo, hi):
            gather(idxn_ref[0, 0, j], j, a_nxt).start(priority=0)
        lo, hi = next_scatter[0], min(next_scatter[0] + scatter_rows, cap)
        next_scatter[0] = hi
        for j in range(lo, hi):
            scatter(idxp_ref[0, 0, j], j, a_prv).start(priority=1)

    def chunk_rows(rb, c):
        return pl.ds(rb * FFN_ROWS * chunks + c, FFN_ROWS, stride=chunks)

    def gate_rows(rb):
        g_row = gv_ref[0, rb]
        return jnp.concatenate(
            [jnp.broadcast_to(g_row[:, i * LANES:(i + 1) * LANES], (LANES, LANES)).T for i in range(FFN_ROWS // LANES)],
            axis=0)

    hid_prev = None
    for stage in range(nblk + 1):
        up, dn = stage < nblk, stage >= 1
        if up:
            xb = jnp.concatenate([buf_ref[a_cur, SLAB_X, chunk_rows(stage, c), :] for c in range(chunks)],
                                 axis=1).astype(BF16)
        if dn:
            gval = gate_rows(stage - 1)
        hid = []
        for ct in range(n_ct):
            cs = slice(ct * FFN_COLS, (ct + 1) * FFN_COLS)
            if up:
                hg = _dot(xb, wg_ref[0, :, cs])
                hu = _dot(xb, wu_ref[0, :, cs])
                hid.append((hg * _sigmoid(hg) * hu).astype(BF16))
            if dn:
                ye = _dot(hid_prev, wd_ref[0, :, cs])
                for i in range(halves):
                    buf_ref[a_cur, SLAB_ACC, chunk_rows(stage - 1, ct * halves + i), :] += (
                        ye[:, i * LANES:(i + 1) * LANES] * gval)
                issue_rows()
        hid_prev = jnp.concatenate(hid, axis=1) if up else None
    assert next_gather[0] == cap and next_scatter[0] == cap and n_issue[0] == n_updates

    wait_scatter(a_prv)

    @pl.when(s == last)
    def _():
        wait_gather(a_nxt)

        @pl.loop(0, cap)
        def _(j):
            scatter(idxn_ref[0, 0, j], j, a_cur).start()
        wait_scatter(a_cur)


def _expert_ffn(idx, gval, slab, wg, wu, wd):
    B, E, cap = idx.shape
    d_model, ff = wg.shape[1], wg.shape[2]
    chunks = d_model // LANES
    assert B >= 3 and cap % FFN_ROWS == 0 and ff % FFN_COLS == 0 and d_model % FFN_COLS == 0
    ns = E * B
    nblk = cap // FFN_ROWS
    idx3 = idx.reshape(B * E, 1, cap)
    gv4 = gval.reshape(B * E, nblk, 1, FFN_ROWS)
    any_spec = pl.BlockSpec(memory_space=pl.ANY)
    wspec = lambda r, c: pl.BlockSpec((1, r, c), lambda s: (s // B, 0, 0))
    blk = lambda s: (s % B) * E + s // B
    ispec = lambda f: pl.BlockSpec((1, 1, cap), lambda s: (blk(f(s)), 0, 0), memory_space=pltpu.SMEM)
    kern = functools.partial(_ffn_kernel, nb=B, cap=cap, d_model=d_model, ff=ff)
    return pl.pallas_call(
        kern,
        grid=(ns,),
        in_specs=[ispec(lambda s: jnp.maximum(s - 1, 0)), ispec(lambda s: jnp.minimum(s + 1, ns - 1)),
                  pl.BlockSpec((1, nblk, 1, FFN_ROWS), lambda s: (blk(s), 0, 0, 0)),
                  any_spec, wspec(d_model, ff), wspec(d_model, ff), wspec(ff, d_model)],
        out_specs=any_spec,
        out_shape=jax.ShapeDtypeStruct(slab.shape, F32),
        scratch_shapes=[pltpu.VMEM((AB_RING, 2, cap * chunks, LANES), F32), pltpu.SemaphoreType.DMA((2,))],
        input_output_aliases={3: 0},
        compiler_params=pltpu.CompilerParams(dimension_semantics=("arbitrary",), vmem_limit_bytes=VMEM_LIMIT),
        name="expert_ffn",
    )(idx3, idx3, gv4, slab, wg, wu, wd)


def _untile_kernel(a_ref, o_ref):
    tm, chunks = o_ref.shape[0], o_ref.shape[1] // LANES
    for c in range(chunks):
        o_ref[:, c * LANES:(c + 1) * LANES] = a_ref[0, pl.ds(c, tm, stride=chunks), :]


def _untile(slab, plane, chunks, tm):
    N = slab.shape[1] // chunks
    return pl.pallas_call(
        _untile_kernel,
        grid=(N // tm,),
        in_specs=[pl.BlockSpec((1, tm * chunks, LANES), lambda i: (plane, i, 0))],
        out_specs=pl.BlockSpec((tm, chunks * LANES), lambda i: (i, 0)),
        out_shape=jax.ShapeDtypeStruct((N, chunks * LANES), F32),
        compiler_params=pltpu.CompilerParams(dimension_semantics=("arbitrary",), vmem_limit_bytes=VMEM_LIMIT),
        name="untile",
    )(slab)


def _layer(x, mem, norm_mix_g, w_in, att_q_norm_g, att_k_norm_g, att_sink, ml_conv_w, ml_conv_b,
           ml_gate_b, ml_out_norm_g, w_out, norm_mem_g, mem_kv_norm_g, w_mem_q, w_mem_kv,
           mem_q_norm_g, mem_k_norm_g, w_mem_o, norm_ffn_g, w_router, b_router,
           w_exp_gate, w_exp_up, w_exp_down):
    B, S, D = x.shape
    row = lambda v: v.reshape(1, -1).astype(F32)
    tm_in = min(512, S)
    tm_mix = min(256, S)
    cap = CAPACITY_FACTOR * S // N_EXPERTS

    mem_k, mem_v = _mem_kv(mem, row(mem_kv_norm_g), w_mem_kv.astype(BF16), row(mem_k_norm_g))

    order = jnp.array(ATT_HEAD_ORDER)
    w_aq = w_in[:, _AQ[0]:_AQ[1]].reshape(D, ATT_HEADS, ATT_DH)[:, order].reshape(D, -1)
    w_main = jnp.concatenate([w_aq, w_in[:, _AQ[1]:_GATES[0]]], axis=1).astype(BF16)
    wg_t = w_in[:, _GATES[0]:_GATES[1]].T.astype(BF16)
    q_g = jnp.tile(row(att_q_norm_g), (1, ATT_HEADS)) * (ATT_DH ** -0.5)
    k_g = jnp.tile(row(att_k_norm_g), (1, ATT_KV))
    aq, ak, av, mqk, mv, mo, g_rows, g_cols = _inproj(x, row(norm_mix_g), w_main, wg_t, ml_gate_b.reshape(-1, 1).astype(F32),
                                               q_g, k_g, ml_conv_w.astype(F32), row(ml_conv_b), tm_in)

    att = _attention(att_sink.astype(F32), aq, ak, av)
    hf, hb = _mlstm(mqk, mv, g_rows, g_cols)

    att_w = ATT_HEADS * ATT_DH
    w_out_att = w_out[:att_w].reshape(ATT_HEADS, ATT_DH, D)[order].reshape(att_w, D)
    w_out_p = jnp.concatenate([w_out_att, w_out[att_w:]], axis=0).astype(BF16)
    slab, aff_t = _mix(x, att, hf, hb, mo, row(ml_out_norm_g), w_out_p, row(norm_mem_g),
                             w_mem_q.astype(BF16), row(mem_q_norm_g), mem_k, mem_v, w_mem_o.astype(BF16),
                             row(norm_ffn_g), w_router.astype(F32), row(b_router), tm_mix)

    idx, gval = _topc(aff_t, cap, D // LANES)
    slab = _expert_ffn(idx, gval, slab, w_exp_gate.astype(BF16), w_exp_up.astype(BF16), w_exp_down.astype(BF16))
    return _untile(slab, SLAB_ACC, D // LANES, tm_in).reshape(B, S, D)


def kernel(x, mem, norm_mix_g, w_in, att_q_norm_g, att_k_norm_g, att_sink, ml_conv_w, ml_conv_b, ml_gate_b,
           ml_out_norm_g, w_out, norm_mem_g, mem_kv_norm_g, w_mem_q, w_mem_kv, mem_q_norm_g, mem_k_norm_g,
           w_mem_o, norm_ffn_g, w_router, b_router, w_exp_gate, w_exp_up, w_exp_down):
    params = (norm_mix_g, w_in, att_q_norm_g, att_k_norm_g, att_sink, ml_conv_w, ml_conv_b, ml_gate_b,
              ml_out_norm_g, w_out, norm_mem_g, mem_kv_norm_g, w_mem_q, w_mem_kv, mem_q_norm_g, mem_k_norm_g,
              w_mem_o, norm_ffn_g, w_router, b_router, w_exp_gate, w_exp_up, w_exp_down)
    depth = norm_mix_g.shape[0]
    for l in range(depth):
        x = _layer(x, mem, *[p[l] for p in params])
    return x
```

```python
import functools

import jax
import jax.numpy as jnp
from jax import lax
from jax.experimental import pallas as pl
from jax.experimental.pallas import tpu as pltpu

F32 = jnp.float32
BF16 = jnp.bfloat16
I32 = jnp.int32

EPS = 1e-6
LANES = 128
BLK = 128
ATT_HEADS, ATT_KV, ATT_DH = 8, 2, 64
ML_HEADS, ML_DH = 4, 128
MEM_HEADS, MEM_DH = 4, 128
N_EXPERTS = 16
CAPACITY_FACTOR = 2
NEG = -1e30
HALO_X = 8
VMEM_LIMIT = 48 * 1024 * 1024

_NT = (((1,), (1,)), ((), ()))


def _dot(a, b):
    return jnp.dot(a, b, preferred_element_type=F32)


def _dot_nt(a, b):
    return lax.dot_general(a, b, _NT, preferred_element_type=F32)


def _split3(x):
    hi = x.astype(BF16)
    r1 = x - hi.astype(F32)
    mid = r1.astype(BF16)
    lo = (r1 - mid.astype(F32)).astype(BF16)
    return hi, mid, lo


def _dot01_nt(m01, x):
    hi, mid, lo = _split3(x)
    return _dot_nt(m01, hi) + _dot_nt(m01, mid) + _dot_nt(m01, lo)


def _dot01(x, m01):
    hi, mid, lo = _split3(x)
    return _dot(hi, m01) + _dot(mid, m01) + _dot(lo, m01)


def _rms(x, g):
    ms = jnp.mean(x * x, axis=-1, keepdims=True)
    return x * lax.rsqrt(ms + EPS) * g


def _sigmoid(x):
    return 1.0 / (1.0 + jnp.exp(-x))


def _iota(shape, dim):
    return lax.broadcasted_iota(I32, shape, dim)


def _mem_kv_kernel(mem_ref, g_ref, w_ref, kg_ref, k_ref, v_ref):
    mn = _rms(mem_ref[0], g_ref[...]).astype(BF16)
    kv = _dot(mn, w_ref[...])
    width = MEM_HEADS * MEM_DH
    for h in range(MEM_HEADS):
        sl = slice(h * MEM_DH, (h + 1) * MEM_DH)
        k_ref[0, :, sl] = _rms(kv[:, sl], kg_ref[...]).astype(BF16)
    v_ref[0] = kv[:, width:].astype(BF16)


def _mem_kv(mem, g, w_kv, k_g):
    B, M, D = mem.shape
    width = MEM_HEADS * MEM_DH
    full = lambda *s: pl.BlockSpec(s, lambda b: (0,) * len(s))
    return pl.pallas_call(
        _mem_kv_kernel,
        grid=(B,),
        in_specs=[pl.BlockSpec((1, M, D), lambda b: (b, 0, 0)), full(1, D), full(D, 2 * width), full(1, MEM_DH)],
        out_specs=[pl.BlockSpec((1, M, width), lambda b: (b, 0, 0))] * 2,
        out_shape=[jax.ShapeDtypeStruct((B, M, width), BF16)] * 2,
        compiler_params=pltpu.CompilerParams(dimension_semantics=("arbitrary",), vmem_limit_bytes=VMEM_LIMIT),
        name="mem_kv",
    )(mem, g, w_kv, k_g)


_AQ = (0, 512)
_AK = (512, 640)
_AV = (640, 768)
_MQK = (768, 1792)
_MV = (1792, 2304)
_MO = (2304, 2816)
_GATES = (2816, 2832)


GC_F, GC_GMAX, GC_EEND, GC_MLOC = 0, 1, 2, 3


def _inproj_kernel(x_ref, xp_ref, xn_ref, g_ref, w_ref, wgt_ref, gb_ref, qg_ref, kg_ref, bdq_ref, bdk_ref, trif_ref,
                   trib_ref, blk_ref, cw_ref, cb_ref, aq_ref, ak_ref, av_ref, mqk_ref, mv_ref, mo_ref, grow_ref, gcol_ref):
    h = _rms(x_ref[0], g_ref[...]).astype(BF16)
    gt = _dot_nt(wgt_ref[...], h) + gb_ref[...]
    logsig = jnp.minimum(gt, 0.0) - jnp.log(1.0 + jnp.exp(-jnp.abs(gt)))
    tm = gt.shape[1]
    H = ML_HEADS
    pos = _iota((H, tm), 1) % BLK

    def chunk_scan(li, lf, tri_ref, fwd):
        f = _dot01(lf, tri_ref[...])
        g = li - f
        gmax = g
        sh = 1
        while sh < BLK:
            if fwd:
                gmax = jnp.where(pos >= sh, jnp.maximum(gmax, pltpu.roll(gmax, sh, axis=1)), gmax)
            else:
                gmax = jnp.where(pos < BLK - sh, jnp.maximum(gmax, pltpu.roll(gmax, tm - sh, axis=1)), gmax)
            sh *= 2
        w_end = _dot01(lf, blk_ref[...]) + g
        m_loc = jnp.concatenate(
            [jnp.broadcast_to(jnp.max(w_end[:, c0:c0 + BLK], axis=1, keepdims=True), (H, BLK))
             for c0 in range(0, tm, BLK)], axis=1)
        return f, g, gmax, jnp.exp(w_end - m_loc), m_loc

    f_f, g_f, gm_f, ee_f, ml_f = chunk_scan(gt[0:H], logsig[H:2 * H], trif_ref, True)
    f_b, g_b, gm_b, ee_b, ml_b = chunk_scan(gt[2 * H:3 * H], logsig[3 * H:4 * H], trib_ref, False)
    grow_ref[0] = jnp.concatenate([g_f, g_b], axis=0)
    kinds = [None] * 4
    kinds[GC_F], kinds[GC_GMAX], kinds[GC_EEND], kinds[GC_MLOC] = (f_f, f_b), (gm_f, gm_b), (ee_f, ee_b), (ml_f, ml_b)
    rows = jnp.concatenate([r for pair in kinds for r in pair], axis=0)
    pad = jnp.zeros((LANES - rows.shape[0], BLK), F32)
    for c0 in range(0, tm, BLK):
        gcol_ref[0, c0:c0 + BLK, :] = jnp.concatenate([rows[:, c0:c0 + BLK], pad], axis=0).T
    sec = lambda s: _dot(h, w_ref[:, s[0]:s[1]])
    aq = sec(_AQ)
    ssq = _dot((aq * aq).astype(BF16), bdq_ref[...])
    aq_ref[0] = (aq * lax.rsqrt(ssq * (1.0 / ATT_DH) + EPS) * qg_ref[...]).astype(BF16)
    ak = sec(_AK)
    ssk = _dot((ak * ak).astype(BF16), bdk_ref[...])
    ak_ref[0] = (ak * lax.rsqrt(ssk * (1.0 / ATT_DH) + EPS) * kg_ref[...]).astype(BF16)
    av_ref[0] = sec(_AV).astype(BF16)
    i, ni = pl.program_id(1), pl.num_programs(1)
    h_halo = jnp.concatenate([_rms(xp_ref[0], g_ref[...]), _rms(xn_ref[0], g_ref[...])], axis=0).astype(BF16)
    halo = _dot(h_halo, w_ref[:, _MQK[0]:_MQK[1]])
    before = jnp.where(i == 0, 0.0, halo[HALO_X - 1:HALO_X, :])
    after = jnp.where(i == ni - 1, 0.0, halo[HALO_X:HALO_X + 1, :])
    raw = sec(_MQK)
    r = _iota((tm, 1), 0)
    x_prev = jnp.where(r == 0, before, pltpu.roll(raw, 1, axis=0))
    x_next = jnp.where(r == tm - 1, after, pltpu.roll(raw, tm - 1, axis=0))
    y = cw_ref[0:1, :] * x_prev + cw_ref[1:2, :] * raw + cw_ref[2:3, :] * x_next + cb_ref[...]
    k_scale = jnp.where(_iota((1, y.shape[1]), 1) < ML_HEADS * ML_DH, 1.0, ML_DH ** -0.5)
    mqk_ref[0] = (y * _sigmoid(y) * k_scale).astype(BF16)
    mv_ref[0] = sec(_MV).astype(BF16)
    mo_ref[0] = sec(_MO).astype(BF16)


def _inproj(x, g, w_main, wg_t, gate_b, q_g, k_g, conv_w, conv_b, tm):
    B, S, D = x.shape
    per = tm // HALO_X
    last = S // HALO_X - 1
    bdq = (jnp.arange(512)[:, None] // ATT_DH == jnp.arange(512)[None, :] // ATT_DH).astype(BF16)
    bdk = bdq[:128, :128]
    s_from, s_to = jnp.arange(tm)[:, None], jnp.arange(tm)[None, :]
    same = s_from // BLK == s_to // BLK
    tri_f = (same & (s_from <= s_to)).astype(BF16)
    tri_b = (same & (s_from >= s_to)).astype(BF16)
    full = lambda *s: pl.BlockSpec(s, lambda b, i: (0,) * len(s))
    tok = lambda w: pl.BlockSpec((1, tm, w), lambda b, i: (b, i, 0))
    widths = (512, 128, 128, 1024, 512, 512)
    return pl.pallas_call(
        _inproj_kernel,
        grid=(B, S // tm),
        in_specs=[tok(D),
                  pl.BlockSpec((1, HALO_X, D), lambda b, i: (b, jnp.maximum(i * per - 1, 0), 0)),
                  pl.BlockSpec((1, HALO_X, D), lambda b, i: (b, jnp.minimum((i + 1) * per, last), 0)),
                  full(1, D), full(D, w_main.shape[1]), full(16, D), full(16, 1),
                  full(1, 512), full(1, 128), full(512, 512), full(128, 128), full(tm, tm), full(tm, tm), full(tm, tm),
                  full(3, 2 * ML_HEADS * ML_DH), full(1, 2 * ML_HEADS * ML_DH)],
        out_specs=[tok(w) for w in widths] + [pl.BlockSpec((1, 2 * ML_HEADS, tm), lambda b, i: (b, 0, i)), tok(LANES)],
        out_shape=[jax.ShapeDtypeStruct((B, S, w), BF16) for w in widths]
        + [jax.ShapeDtypeStruct((B, 2 * ML_HEADS, S), F32), jax.ShapeDtypeStruct((B, S, LANES), F32)],
        compiler_params=pltpu.CompilerParams(dimension_semantics=("arbitrary", "arbitrary"),
                                             vmem_limit_bytes=VMEM_LIMIT),
        name="inproj",
    )(x, x, x, g, w_main, wg_t, gate_b, q_g, k_g, bdq, bdk, tri_f, tri_b, same.astype(BF16), conv_w, conv_b)


ATT_GROUP = ATT_HEADS // ATT_KV
ATT_HEAD_ORDER = tuple(kv * ATT_GROUP + g for g in range(ATT_GROUP) for kv in range(ATT_KV))


ATT_QB = 4


def _attn_kernel(sink_ref, q_ref, kp_ref, kc_ref, kn_ref, vp_ref, vc_ref, vn_ref, bias_ref, o_ref):
    n = pl.program_id(1)
    nb = pl.num_programs(1)
    G, J = ATT_GROUP, ATT_QB
    kband = jnp.concatenate([kp_ref[0], kc_ref[0], kn_ref[0]], axis=0)
    vband = jnp.concatenate([vp_ref[0], vc_ref[0], vn_ref[0]], axis=0)
    first = _iota(kband.shape, 1) < ATT_DH
    kband, vband = kband.astype(F32), vband.astype(F32)
    keep_lanes = lambda a, mine: jnp.where(mine, a, 0.0).astype(BF16)
    si = _iota((1, (J + 2) * BLK), 1)
    edge = jnp.where(((si < BLK) & (n == 0)) | ((si >= (J + 1) * BLK) & (n == nb - 1)), NEG, 0.0)
    lane_first = _iota((1, LANES), 1) < ATT_DH
    k_kv = [keep_lanes(kband, first), keep_lanes(kband, ~first)]
    v_kv = [jnp.concatenate([keep_lanes(vband, m), m.astype(F32).astype(BF16)], axis=1) for m in (first, ~first)]
    for j in range(J):
        rows = slice(j * BLK, (j + 1) * BLK)
        band = slice(j * BLK, (j + 3) * BLK)
        q = q_ref[0, rows, :]
        qs = jnp.concatenate([q[:, g * LANES:(g + 1) * LANES] for g in range(G)], axis=0)
        tot = None
        sink_terms = []
        for kv in range(ATT_KV):
            s = _dot_nt(qs, k_kv[kv][band])
            ps, st = [], []
            for g in range(G):
                h = kv * G + g
                sink = sink_ref[h]
                sg = s[g * BLK:(g + 1) * BLK] + bias_ref[h] + edge[:, band]
                m = jnp.maximum(jnp.max(sg, axis=-1, keepdims=True), sink)
                ps.append(jnp.exp(sg - m).astype(BF16))
                st.append(jnp.exp(sink - m))
            part = _dot(jnp.concatenate(ps, axis=0), v_kv[kv][band])
            tot = part if tot is None else tot + part
            sink_terms.append(jnp.concatenate(st, axis=0))
        den = tot[:, LANES:] + jnp.where(lane_first, sink_terms[0], sink_terms[1])
        out = (tot[:, :LANES] / den).astype(BF16)
        o_ref[0, rows, :] = jnp.concatenate([out[g * BLK:(g + 1) * BLK] for g in range(G)], axis=1)


def _attention(sink, aq, ak, av):
    B, S, _ = aq.shape
    J = ATT_QB
    nb = S // (J * BLK)
    last_blk = S // BLK - 1
    kvw = ATT_KV * ATT_DH
    assert kvw == LANES and S % (J * BLK) == 0
    dist = jnp.abs(jnp.arange(BLK)[:, None] + BLK - jnp.arange(3 * BLK)[None, :]).astype(F32)
    slopes = jnp.exp2(-8.0 * jnp.arange(1, ATT_HEADS + 1, dtype=F32) / ATT_HEADS)
    bias = jnp.where(dist <= BLK, -slopes[:, None, None] * dist, NEG)
    prev = pl.BlockSpec((1, BLK, kvw), lambda b, n, *_: (b, jnp.maximum(J * n - 1, 0), 0))
    cur = pl.BlockSpec((1, J * BLK, kvw), lambda b, n, *_: (b, n, 0))
    nxt = pl.BlockSpec((1, BLK, kvw), lambda b, n, *_: (b, jnp.minimum(J * n + J, last_blk), 0))
    qspec = pl.BlockSpec((1, J * BLK, ATT_HEADS * ATT_DH), lambda b, n, *_: (b, n, 0))
    bspec = pl.BlockSpec((ATT_HEADS, BLK, 3 * BLK), lambda b, n, *_: (0, 0, 0))
    return pl.pallas_call(
        _attn_kernel,
        grid_spec=pltpu.PrefetchScalarGridSpec(
            num_scalar_prefetch=1, grid=(B, nb),
            in_specs=[qspec, prev, cur, nxt, prev, cur, nxt, bspec], out_specs=qspec),
        out_shape=jax.ShapeDtypeStruct(aq.shape, BF16),
        compiler_params=pltpu.CompilerParams(dimension_semantics=("arbitrary", "arbitrary"),
                                             vmem_limit_bytes=VMEM_LIMIT),
        name="win_attn",
    )(sink, aq, ak, ak, ak, av, av, av, bias)


ML_CPS = 4


def _mlstm_kernel(xf_ref, xb_ref, vf_ref, vb_ref, grf_ref, grb_ref, gcf_ref, gcb_ref, hf_ref, hb_ref, cn_ref, m_ref):
    c = pl.program_id(1)
    L = BLK
    width = ML_HEADS * ML_DH

    @pl.when(c == 0)
    def _():
        cn_ref[...] = jnp.zeros_like(cn_ref)
        m_ref[...] = jnp.zeros_like(m_ref)

    tt = _iota((L, L), 0)
    ss = _iota((L, L), 1)
    ones_v = jnp.ones((L, ML_DH), BF16)

    def chunk(d, sub, qk_ref, v_ref, grow_ref, gcol_ref, out_ref):
        H = ML_HEADS
        rows = slice(sub * L, (sub + 1) * L)
        keep = (ss <= tt) if d == 0 else (ss >= tt)
        g_row = grow_ref[0, H * d:H * (d + 1), rows]
        gc = gcol_ref[0, rows, :]
        kind = lambda i: gc[:, 2 * H * i + H * d:2 * H * i + H * (d + 1)]
        f, g_max, e_end, m_loc = kind(GC_F), kind(GC_GMAX), kind(GC_EEND), kind(GC_MLOC)[0:1, :]
        end = L - 1 if d == 0 else 0
        f_end = f[end:end + 1, :]
        m0 = m_ref[d, 0:1, 0:H]
        mm = jnp.maximum(m0, g_max)
        e_inter = jnp.exp(m0 - mm)
        floor = jnp.exp(-(f + mm))
        m_new = jnp.maximum(f_end + m0, m_loc)
        ca = jnp.exp(f_end + m0 - m_new)
        cb = jnp.exp(m_loc - m_new)
        m_ref[d, 0:1, 0:H] = m_new
        col = lambda a, h, w=ML_DH: jnp.broadcast_to(a[:, h:h + 1], (L, w))
        for h in range(H):
            u = H * d + h
            hs = slice(h * ML_DH, (h + 1) * ML_DH)
            qb = qk_ref[0, rows, hs]
            kb = qk_ref[0, rows, width + h * ML_DH: width + (h + 1) * ML_DH]
            v1 = jnp.concatenate([v_ref[0, rows, hs], ones_v], axis=1)
            cn0 = cn_ref[u]
            dec = jnp.where(keep, jnp.exp(g_row[h:h + 1, :] - col(mm, h)), 0.0)
            s_qk = (_dot_nt(qb, kb) * dec).astype(BF16)
            tot = col(e_inter, h, 2 * ML_DH) * _dot(qb, cn0.astype(BF16)) + _dot(s_qk, v1)
            out_ref[0, rows, hs] = tot[:, :ML_DH] / jnp.maximum(jnp.abs(tot[:, ML_DH:]), col(floor, h))
            ks = kb.astype(F32) * col(e_end, h)
            cn_ref[u] = ca[:, h:h + 1] * cn0 + cb[:, h:h + 1] * _dot(ks.T.astype(BF16), v1)

    for i in range(ML_CPS):
        chunk(0, i, xf_ref, vf_ref, grf_ref, gcf_ref, hf_ref)
        chunk(1, ML_CPS - 1 - i, xb_ref, vb_ref, grb_ref, gcb_ref, hb_ref)


def _mlstm(mqk, mv, g_rows, g_cols):
    B, S, _ = mqk.shape
    rows = ML_CPS * BLK
    ns = S // rows
    assert S % rows == 0
    width = ML_HEADS * ML_DH
    fwd = lambda c: c
    bwd = lambda c: ns - 1 - c
    qkspec = lambda ci: pl.BlockSpec((1, rows, 2 * width), lambda b, c: (b, ci(c), 0))
    vspec = lambda ci: pl.BlockSpec((1, rows, width), lambda b, c: (b, ci(c), 0))
    gspec = lambda ci: pl.BlockSpec((1, 2 * ML_HEADS, rows), lambda b, c: (b, 0, ci(c)))
    cspec = lambda ci: pl.BlockSpec((1, rows, LANES), lambda b, c: (b, ci(c), 0))
    units = 2 * ML_HEADS
    return pl.pallas_call(
        _mlstm_kernel,
        grid=(B, ns),
        in_specs=[qkspec(fwd), qkspec(bwd), vspec(fwd), vspec(bwd), gspec(fwd), gspec(bwd), cspec(fwd), cspec(bwd)],
        out_specs=[vspec(fwd), vspec(bwd)],
        out_shape=[jax.ShapeDtypeStruct((B, S, width), F32)] * 2,
        scratch_shapes=[pltpu.VMEM((units, ML_DH, 2 * ML_DH), F32), pltpu.VMEM((2, 8, LANES), F32)],
        compiler_params=pltpu.CompilerParams(dimension_semantics=("arbitrary", "arbitrary"),
                                             vmem_limit_bytes=VMEM_LIMIT),
        name="mlstm",
    )(mqk, mqk, mv, mv, g_rows, g_rows, g_cols, g_cols)


MIX_ROWS = 256


def _mix_kernel(x_ref, att_ref, hf_ref, hb_ref, mo_ref, og_ref, wo_ref,
                gm_ref, wq_ref, mqg_ref, k_ref, v_ref, wmo_ref,
                gf_ref, wr_ref, brt_ref,
                slab_ref, afft_ref):
    width = ML_HEADS * ML_DH
    k = k_ref[0]
    v = v_ref[0]
    wr = wr_ref[...]
    w_hi = wr.astype(BF16)
    w_lo = (wr - w_hi.astype(F32)).astype(BF16)
    for r0 in range(0, x_ref.shape[1], MIX_ROWS):
        rows = slice(r0, r0 + MIX_ROWS)
        x = x_ref[0, rows, :]
        ml = hf_ref[0, rows, :] + hb_ref[0, rows, :]
        mo = mo_ref[0, rows, :].astype(F32)
        parts = []
        for h in range(ML_HEADS):
            sl = slice(h * ML_DH, (h + 1) * ML_DH)
            parts.append((_sigmoid(mo[:, sl]) * _rms(ml[:, sl], og_ref[:, sl])).astype(BF16))
        ml_out = jnp.concatenate(parts, axis=1)
        y1 = x + _dot(att_ref[0, rows, :], wo_ref[0:width, :]) + _dot(ml_out, wo_ref[width:2 * width, :])
        h2 = _rms(y1, gm_ref[...]).astype(BF16)
        qm = _dot(h2, wq_ref[...])
        outs = []
        for h in range(MEM_HEADS):
            sl = slice(h * MEM_DH, (h + 1) * MEM_DH)
            qh = (_rms(qm[:, sl], mqg_ref[...]) * (MEM_DH ** -0.5)).astype(BF16)
            s = _dot_nt(qh, k[:, sl])
            p = jnp.exp(s - jnp.max(s, axis=-1, keepdims=True))
            o = _dot(p.astype(BF16), v[:, sl]) / jnp.sum(p, axis=-1, keepdims=True)
            outs.append(o.astype(BF16))
        y2 = y1 + _dot(jnp.concatenate(outs, axis=1), wmo_ref[...])
        h3 = _rms(y2, gf_ref[...])
        h_hi = h3.astype(BF16)
        h_lo = (h3 - h_hi.astype(F32)).astype(BF16)
        logits = _dot(h_hi, w_hi) + _dot(h_lo, w_hi) + _dot(h_hi, w_lo)
        logits_t = jnp.concatenate([logits[q0:q0 + LANES, :].T[:N_EXPERTS, :] for q0 in range(0, MIX_ROWS, LANES)],
                                   axis=1) + brt_ref[...]
        pt = jnp.exp(logits_t - jnp.max(logits_t, axis=0, keepdims=True))
        afft_ref[0, :, rows] = pt / jnp.sum(pt, axis=0, keepdims=True)
        chunks = y2.shape[1] // LANES
        for c in range(chunks):
            tiles = pl.ds(r0 * chunks + c, MIX_ROWS, stride=chunks)
            slab_ref[SLAB_ACC, tiles, :] = y2[:, c * LANES:(c + 1) * LANES]
            slab_ref[SLAB_X, tiles, :] = h3[:, c * LANES:(c + 1) * LANES]


def _mix(x, att, hf, hb, mo, out_g, w_out, g_mem, w_q, mq_g, mem_k, mem_v, w_mo, g_ffn, w_r, b_r, tm):
    B, S, D = x.shape
    M = mem_k.shape[1]
    width = ML_HEADS * ML_DH
    mw = MEM_HEADS * MEM_DH
    full = lambda *s: pl.BlockSpec(s, lambda b, i: (0,) * len(s))
    tok = lambda w: pl.BlockSpec((1, tm, w), lambda b, i: (b, i, 0))
    chunks = D // LANES
    tiled = pl.BlockSpec((2, tm * chunks, LANES), lambda b, i: (0, b * (S // tm) + i, 0))
    memspec = pl.BlockSpec((1, M, mw), lambda b, i: (b, 0, 0))
    tiled_shape = jax.ShapeDtypeStruct((2, B * S * chunks, LANES), F32)
    return pl.pallas_call(
        _mix_kernel,
        grid=(B, S // tm),
        in_specs=[tok(D), tok(width), tok(width), tok(width), tok(width), full(1, width), full(2 * width, D),
                  full(1, D), full(D, mw), full(1, MEM_DH), memspec, memspec, full(mw, D),
                  full(1, D), full(D, LANES), full(N_EXPERTS, 1)],
        out_specs=[tiled, pl.BlockSpec((1, N_EXPERTS, tm), lambda b, i: (b, 0, i))],
        out_shape=[tiled_shape, jax.ShapeDtypeStruct((B, N_EXPERTS, S), F32)],
        compiler_params=pltpu.CompilerParams(dimension_semantics=("arbitrary", "arbitrary"),
                                             vmem_limit_bytes=VMEM_LIMIT),
        name="mix_mem_router",
    )(x, att, hf, hb, mo, out_g, w_out, g_mem, w_q, mq_g, mem_k, mem_v, w_mo, g_ffn,
      jnp.pad(w_r, ((0, 0), (0, LANES - N_EXPERTS))), b_r.T)


def _topc_kernel(aff2_ref, aff3_ref, idx_ref, gval_ref, *, cap, seq, row_pitch):
    a2 = aff2_ref[0]
    bits2 = pltpu.bitcast(a2, I32)
    capf = float(cap)

    def bisect(i, lo):
        cand = lo | jnp.left_shift(jnp.int32(1), 30 - i)
        cnt = jnp.sum((bits2 >= cand).astype(F32), axis=1, keepdims=True)
        return jnp.where(cnt >= capf, cand, lo)

    thr_all = lax.fori_loop(0, 31, bisect, jnp.zeros((N_EXPERTS, 1), I32))
    need_all = capf - jnp.sum((bits2 > thr_all).astype(F32), axis=1, keepdims=True)

    T = aff3_ref.shape[2]
    tri_u = (_iota((LANES, LANES), 0) <= _iota((LANES, LANES), 1)).astype(BF16)
    tri_l = (_iota((LANES, LANES), 1) <= _iota((LANES, LANES), 0)).astype(BF16)
    ones8 = jnp.ones((8, LANES), BF16)
    before = _iota((T, T), 1) < _iota((T, T), 0)
    kcol = _iota((T, 1), 0).astype(F32)
    j = _iota((1, cap), 1).astype(F32)
    eye = (_iota((LANES, LANES), 0) == _iota((LANES, LANES), 1)).astype(BF16)
    lane_pos = _iota((LANES, cap), 0).astype(F32)

    def tile_starts(maskb):
        tot_row = _dot_nt(ones8, maskb)[0:1, :]
        return jnp.sum(jnp.where(before, tot_row, 0.0), axis=1, keepdims=True)

    for e in range(N_EXPERTS):
        bits = pltpu.bitcast(aff3_ref[0, e], I32)
        thr = thr_all[e:e + 1, :]
        need = need_all[e:e + 1, :]
        gt = bits > thr
        eq = bits == thr
        eqb = eq.astype(BF16)
        eq_rank = _dot(eqb, tri_u) + tile_starts(eqb) - eq.astype(F32)
        sel = gt | (eq & (eq_rank < need))
        selb = sel.astype(BF16)
        cs = _dot(selb, tri_u)
        start = tile_starts(selb)
        end = start + cs[:, LANES - 1:LANES]
        onehot = (start <= j) & (j < end)
        ohf = onehot.astype(F32)
        tile_of = jnp.sum(ohf * kcol, axis=0, keepdims=True)
        j_loc = j - jnp.sum(ohf * start, axis=0, keepdims=True)
        cs_t = _dot_nt(tri_l, selb)
        ohb = onehot.astype(BF16)
        r_t = _dot(cs_t.astype(BF16), ohb)
        local = jnp.sum((r_t <= j_loc).astype(F32), axis=0, keepdims=True)
        idx_ref[0, e:e + 1, :] = ((tile_of * LANES + local).astype(I32) + pl.program_id(0) * seq) * row_pitch
        a_t = _dot01_nt(eye, aff3_ref[0, e])
        a_tile = sum(_dot(term, ohb) for term in _split3(a_t))
        gval_ref[0, e:e + 1, :] = jnp.sum(jnp.where(lane_pos == local, a_tile, 0.0), axis=0, keepdims=True)


def _topc(aff_t, cap, row_pitch):
    B, E, S = aff_t.shape
    T = S // LANES
    aff3 = aff_t.reshape(B, E, T, LANES)
    return pl.pallas_call(
        functools.partial(_topc_kernel, cap=cap, seq=S, row_pitch=row_pitch),
        grid=(B,),
        in_specs=[pl.BlockSpec((1, E, S), lambda b: (b, 0, 0)), pl.BlockSpec((1, E, T, LANES), lambda b: (b, 0, 0, 0))],
        out_specs=[pl.BlockSpec((1, E, cap), lambda b: (b, 0, 0))] * 2,
        out_shape=[jax.ShapeDtypeStruct((B, E, cap), I32), jax.ShapeDtypeStruct((B, E, cap), F32)],
        compiler_params=pltpu.CompilerParams(dimension_semantics=("arbitrary",), vmem_limit_bytes=VMEM_LIMIT),
        name="topc",
    )(aff_t, aff3)


FFN_ROWS = 256
FFN_COLS = 256
AB_RING = 3


SLAB_X, SLAB_ACC = 0, 1


def _ffn_kernel(idxp_ref, idxn_ref, gv_ref, slab_in_hbm, wg_ref, wu_ref, wd_ref, slab_hbm,
                buf_ref, sems, *, nb, cap, d_model, ff):
    del slab_in_hbm
    s = pl.program_id(0)
    last = pl.num_programs(0) - 1
    a_cur = s % AB_RING
    a_nxt = (s + 1) % AB_RING
    a_prv = (s + 2) % AB_RING
    chunks = d_model // LANES

    def rows_of(start):
        return pl.ds(pl.multiple_of(start, chunks), chunks)

    def gather(r, j, sl):
        return pltpu.make_async_copy(slab_hbm.at[:, rows_of(r)], buf_ref.at[sl, :, rows_of(j * chunks)], sems.at[0])

    def scatter(r, j, sl):
        return pltpu.make_async_copy(buf_ref.at[sl, SLAB_ACC, rows_of(j * chunks)], slab_hbm.at[SLAB_ACC, rows_of(r)],
                                     sems.at[1])

    def wait_gather(sl):
        pltpu.make_async_copy(slab_hbm.at[:, pl.ds(0, cap * chunks)], buf_ref.at[sl], sems.at[0]).wait()

    def wait_scatter(sl):
        pltpu.make_async_copy(buf_ref.at[sl, SLAB_ACC], slab_hbm.at[SLAB_ACC, pl.ds(0, cap * chunks)],
                              sems.at[1]).wait()

    @pl.when(s == 0)
    def _():
        @pl.loop(0, cap)
        def _(j):
            r = idxp_ref[0, 0, j]
            gather(r, j, 0).start()
            gather(r, j, AB_RING - 1).start()
        wait_gather(AB_RING - 1)

    wait_gather(a_cur)

    halves = FFN_COLS // LANES
    nblk = cap // FFN_ROWS
    n_ct = ff // FFN_COLS
    assert d_model // FFN_COLS == n_ct

    n_updates = nblk * n_ct
    early = n_updates - n_ct if nblk > 1 else n_updates
    gather_rows = -(-cap // early)
    scatter_rows = -(-cap // n_updates)
    next_gather, next_scatter, n_issue = [0], [0], [0]

    def issue_rows():
        k = n_issue[0]
        n_issue[0] += 1
        lo, hi = next_gather[0], min(next_gather[0] + (gather_rows if k < early else 0), cap)
        next_gather[0] = hi
        for j in range(lo, hi):
            gather(idxn_ref[0, 0, j], j, a_nxt).start(priority=0)
        lo, hi = next_scatter[0], min(next_scatter[0] + scatter_rows, cap)
        next_scatter[0] = hi
        for j in range(lo, hi):
            scatter(idxp_ref[0, 0, j], j, a_prv).start(priority=1)

    def chunk_rows(rb, c):
        return pl.ds(rb * FFN_ROWS * chunks + c, FFN_ROWS, stride=chunks)

    def gate_rows(rb):
        g_row = gv_ref[0, rb]
        return jnp.concatenate(
            [jnp.broadcast_to(g_row[:, i * LANES:(i + 1) * LANES], (LANES, LANES)).T for i in range(FFN_ROWS // LANES)],
            axis=0)

    hid_prev = None
    for stage in range(nblk + 1):
        up, dn = stage < nblk, stage >= 1
        if up:
            xb = jnp.concatenate([buf_ref[a_cur, SLAB_X, chunk_rows(stage, c), :] for c in range(chunks)],
                                 axis=1).astype(BF16)
        if dn:
            gval = gate_rows(stage - 1)
        hid = []
        for ct in range(n_ct):
            cs = slice(ct * FFN_COLS, (ct + 1) * FFN_COLS)
            if up:
                hg = _dot(xb, wg_ref[0, :, cs])
                hu = _dot(xb, wu_ref[0, :, cs])
                hid.append((hg * _sigmoid(hg) * hu).astype(BF16))
            if dn:
                ye = _dot(hid_prev, wd_ref[0, :, cs])
                for i in range(halves):
                    buf_ref[a_cur, SLAB_ACC, chunk_rows(stage - 1, ct * halves + i), :] += (
                        ye[:, i * LANES:(i + 1) * LANES] * gval)
                issue_rows()
        hid_prev = jnp.concatenate(hid, axis=1) if up else None
    assert next_gather[0] == cap and next_scatter[0] == cap and n_issue[0] == n_updates

    wait_scatter(a_prv)

    @pl.when(s == last)
    def _():
        wait_gather(a_nxt)

        @pl.loop(0, cap)
        def _(j):
            scatter(idxn_ref[0, 0, j], j, a_cur).start()
        wait_scatter(a_cur)


def _expert_ffn(idx, gval, slab, wg, wu, wd):
    B, E, cap = idx.shape
    d_model, ff = wg.shape[1], wg.shape[2]
    chunks = d_model // LANES
    assert B >= 3 and cap % FFN_ROWS == 0 and ff % FFN_COLS == 0 and d_model % FFN_COLS == 0
    ns = E * B
    nblk = cap // FFN_ROWS
    idx3 = idx.reshape(B * E, 1, cap)
    gv4 = gval.reshape(B * E, nblk, 1, FFN_ROWS)
    any_spec = pl.BlockSpec(memory_space=pl.ANY)
    wspec = lambda r, c: pl.BlockSpec((1, r, c), lambda s: (s // B, 0, 0))
    blk = lambda s: (s % B) * E + s // B
    ispec = lambda f: pl.BlockSpec((1, 1, cap), lambda s: (blk(f(s)), 0, 0), memory_space=pltpu.SMEM)
    kern = functools.partial(_ffn_kernel, nb=B, cap=cap, d_model=d_model, ff=ff)
    return pl.pallas_call(
        kern,
        grid=(ns,),
        in_specs=[ispec(lambda s: jnp.maximum(s - 1, 0)), ispec(lambda s: jnp.minimum(s + 1, ns - 1)),
                  pl.BlockSpec((1, nblk, 1, FFN_ROWS), lambda s: (blk(s), 0, 0, 0)),
                  any_spec, wspec(d_model, ff), wspec(d_model, ff), wspec(ff, d_model)],
        out_specs=any_spec,
        out_shape=jax.ShapeDtypeStruct(slab.shape, F32),
        scratch_shapes=[pltpu.VMEM((AB_RING, 2, cap * chunks, LANES), F32), pltpu.SemaphoreType.DMA((2,))],
        input_output_aliases={3: 0},
        compiler_params=pltpu.CompilerParams(dimension_semantics=("arbitrary",), vmem_limit_bytes=VMEM_LIMIT),
        name="expert_ffn",
    )(idx3, idx3, gv4, slab, wg, wu, wd)


def _untile_kernel(a_ref, o_ref):
    tm, chunks = o_ref.shape[0], o_ref.shape[1] // LANES
    for c in range(chunks):
        o_ref[:, c * LANES:(c + 1) * LANES] = a_ref[0, pl.ds(c, tm, stride=chunks), :]


def _untile(slab, plane, chunks, tm):
    N = slab.shape[1] // chunks
    return pl.pallas_call(
        _untile_kernel,
        grid=(N // tm,),
        in_specs=[pl.BlockSpec((1, tm * chunks, LANES), lambda i: (plane, i, 0))],
        out_specs=pl.BlockSpec((tm, chunks * LANES), lambda i: (i, 0)),
        out_shape=jax.ShapeDtypeStruct((N, chunks * LANES), F32),
        compiler_params=pltpu.CompilerParams(dimension_semantics=("arbitrary",), vmem_limit_bytes=VMEM_LIMIT),
        name="untile",
    )(slab)


def _layer(x, mem, norm_mix_g, w_in, att_q_norm_g, att_k_norm_g, att_sink, ml_conv_w, ml_conv_b,
           ml_gate_b, ml_out_norm_g, w_out, norm_mem_g, mem_kv_norm_g, w_mem_q, w_mem_kv,
           mem_q_norm_g, mem_k_norm_g, w_mem_o, norm_ffn_g, w_router, b_router,
           w_exp_gate, w_exp_up, w_exp_down):
    B, S, D = x.shape
    row = lambda v: v.reshape(1, -1).astype(F32)
    tm_in = min(512, S)
    tm_mix = min(2 * MIX_ROWS, S)
    cap = CAPACITY_FACTOR * S // N_EXPERTS

    mem_k, mem_v = _mem_kv(mem, row(mem_kv_norm_g), w_mem_kv.astype(BF16), row(mem_k_norm_g))

    order = jnp.array(ATT_HEAD_ORDER)
    w_aq = w_in[:, _AQ[0]:_AQ[1]].reshape(D, ATT_HEADS, ATT_DH)[:, order].reshape(D, -1)
    w_main = jnp.concatenate([w_aq, w_in[:, _AQ[1]:_GATES[0]]], axis=1).astype(BF16)
    wg_t = w_in[:, _GATES[0]:_GATES[1]].T.astype(BF16)
    q_g = jnp.tile(row(att_q_norm_g), (1, ATT_HEADS)) * (ATT_DH ** -0.5)
    k_g = jnp.tile(row(att_k_norm_g), (1, ATT_KV))
    aq, ak, av, mqk, mv, mo, g_rows, g_cols = _inproj(x, row(norm_mix_g), w_main, wg_t, ml_gate_b.reshape(-1, 1).astype(F32),
                                               q_g, k_g, ml_conv_w.astype(F32), row(ml_conv_b), tm_in)

    att = _attention(att_sink.astype(F32), aq, ak, av)
    hf, hb = _mlstm(mqk, mv, g_rows, g_cols)

    att_w = ATT_HEADS * ATT_DH
    w_out_att = w_out[:att_w].reshape(ATT_HEADS, ATT_DH, D)[order].reshape(att_w, D)
    w_out_p = jnp.concatenate([w_out_att, w_out[att_w:]], axis=0).astype(BF16)
    slab, aff_t = _mix(x, att, hf, hb, mo, row(ml_out_norm_g), w_out_p, row(norm_mem_g),
                             w_mem_q.astype(BF16), row(mem_q_norm_g), mem_k, mem_v, w_mem_o.astype(BF16),
                             row(norm_ffn_g), w_router.astype(F32), row(b_router), tm_mix)

    idx, gval = _topc(aff_t, cap, D // LANES)
    slab = _expert_ffn(idx, gval, slab, w_exp_gate.astype(BF16), w_exp_up.astype(BF16), w_exp_down.astype(BF16))
    return _untile(slab, SLAB_ACC, D // LANES, tm_in).reshape(B, S, D)


def kernel(x, mem, norm_mix_g, w_in, att_q_norm_g, att_k_norm_g, att_sink, ml_conv_w, ml_conv_b, ml_gate_b,
           ml_out_norm_g, w_out, norm_mem_g, mem_kv_norm_g, w_mem_q, w_mem_kv, mem_q_norm_g, mem_k_norm_g,
           w_mem_o, norm_ffn_g, w_router, b_router, w_exp_gate, w_exp_up, w_exp_down):
    params = (norm_mix_g, w_in, att_q_norm_g, att_k_norm_g, att_sink, ml_conv_w, ml_conv_b, ml_gate_b,
              ml_out_norm_g, w_out, norm_mem_g, mem_kv_norm_g, w_mem_q, w_mem_kv, mem_q_norm_g, mem_k_norm_g,
              w_mem_o, norm_ffn_g, w_router, b_router, w_exp_gate, w_exp_up, w_exp_down)
    depth = norm_mix_g.shape[0]
    for l in range(depth):
        x = _layer(x, mem, *[p[l] for p in params])
    return x
```

```python
import functools

import jax
import jax.numpy as jnp
from jax import lax
from jax.experimental import pallas as pl
from jax.experimental.pallas import tpu as pltpu

F32 = jnp.float32
BF16 = jnp.bfloat16
I32 = jnp.int32

EPS = 1e-6
LANES = 128
BLK = 128
ATT_HEADS, ATT_KV, ATT_DH = 8, 2, 64
ML_HEADS, ML_DH = 4, 128
MEM_HEADS, MEM_DH = 4, 128
N_EXPERTS = 16
CAPACITY_FACTOR = 2
NEG = -1e30
HALO_X = 8
VMEM_LIMIT = 48 * 1024 * 1024

_NT = (((1,), (1,)), ((), ()))


def _dot(a, b):
    return jnp.dot(a, b, preferred_element_type=F32)


def _dot_nt(a, b):
    return lax.dot_general(a, b, _NT, preferred_element_type=F32)


def _split3(x):
    hi = x.astype(BF16)
    r1 = x - hi.astype(F32)
    mid = r1.astype(BF16)
    lo = (r1 - mid.astype(F32)).astype(BF16)
    return hi, mid, lo


def _dot01_nt(m01, x):
    hi, mid, lo = _split3(x)
    return _dot_nt(m01, hi) + _dot_nt(m01, mid) + _dot_nt(m01, lo)


def _dot01(x, m01):
    hi, mid, lo = _split3(x)
    return _dot(hi, m01) + _dot(mid, m01) + _dot(lo, m01)


def _rms(x, g):
    ms = jnp.mean(x * x, axis=-1, keepdims=True)
    return x * lax.rsqrt(ms + EPS) * g


def _sigmoid(x):
    return 1.0 / (1.0 + jnp.exp(-x))


def _iota(shape, dim):
    return lax.broadcasted_iota(I32, shape, dim)


def _mem_kv_kernel(mem_ref, g_ref, w_ref, kg_ref, k_ref, v_ref):
    mn = _rms(mem_ref[0], g_ref[...]).astype(BF16)
    kv = _dot(mn, w_ref[...])
    width = MEM_HEADS * MEM_DH
    for h in range(MEM_HEADS):
        sl = slice(h * MEM_DH, (h + 1) * MEM_DH)
        k_ref[0, :, sl] = _rms(kv[:, sl], kg_ref[...]).astype(BF16)
    v_ref[0] = kv[:, width:].astype(BF16)


def _mem_kv(mem, g, w_kv, k_g):
    B, M, D = mem.shape
    width = MEM_HEADS * MEM_DH
    full = lambda *s: pl.BlockSpec(s, lambda b: (0,) * len(s))
    return pl.pallas_call(
        _mem_kv_kernel,
        grid=(B,),
        in_specs=[pl.BlockSpec((1, M, D), lambda b: (b, 0, 0)), full(1, D), full(D, 2 * width), full(1, MEM_DH)],
        out_specs=[pl.BlockSpec((1, M, width), lambda b: (b, 0, 0))] * 2,
        out_shape=[jax.ShapeDtypeStruct((B, M, width), BF16)] * 2,
        compiler_params=pltpu.CompilerParams(dimension_semantics=("arbitrary",), vmem_limit_bytes=VMEM_LIMIT),
        name="mem_kv",
    )(mem, g, w_kv, k_g)


_AQ = (0, 512)
_AK = (512, 640)
_AV = (640, 768)
_MQK = (768, 1792)
_MV = (1792, 2304)
_MO = (2304, 2816)
_GATES = (2816, 2832)


GC_F, GC_GMAX, GC_EEND, GC_MLOC = 0, 1, 2, 3


INPROJ_ROWS = 256


def _inproj_kernel(x_ref, xp_ref, xn_ref, g_ref, w_ref, wgt_ref, gb_ref, qg_ref, kg_ref, bdq_ref, bdk_ref, trif_ref,
                   trib_ref, blk_ref, cw_ref, cb_ref, aq_ref, ak_ref, av_ref, mqk_ref, mv_ref, mo_ref, grow_ref, gcol_ref):
    tm = x_ref.shape[1]
    R = INPROJ_ROWS
    H = ML_HEADS
    pos = _iota((H, R), 1) % BLK
    raws = []
    for r0 in range(0, tm, R):
        rows = slice(r0, r0 + R)
        h = _rms(x_ref[0, rows, :], g_ref[...]).astype(BF16)
        gt = _dot_nt(wgt_ref[...], h) + gb_ref[...]
        logsig = jnp.minimum(gt, 0.0) - jnp.log(1.0 + jnp.exp(-jnp.abs(gt)))

        def chunk_scan(li, lf, tri_ref, fwd):
            f = _dot01(lf, tri_ref[...])
            g = li - f
            gmax = g
            sh = 1
            while sh < BLK:
                if fwd:
                    gmax = jnp.where(pos >= sh, jnp.maximum(gmax, pltpu.roll(gmax, sh, axis=1)), gmax)
                else:
                    gmax = jnp.where(pos < BLK - sh, jnp.maximum(gmax, pltpu.roll(gmax, R - sh, axis=1)), gmax)
                sh *= 2
            w_end = _dot01(lf, blk_ref[...]) + g
            m_loc = jnp.concatenate(
                [jnp.broadcast_to(jnp.max(w_end[:, c0:c0 + BLK], axis=1, keepdims=True), (H, BLK))
                 for c0 in range(0, R, BLK)], axis=1)
            return f, g, gmax, jnp.exp(w_end - m_loc), m_loc

        f_f, g_f, gm_f, ee_f, ml_f = chunk_scan(gt[0:H], logsig[H:2 * H], trif_ref, True)
        f_b, g_b, gm_b, ee_b, ml_b = chunk_scan(gt[2 * H:3 * H], logsig[3 * H:4 * H], trib_ref, False)
        grow_ref[0, :, rows] = jnp.concatenate([g_f, g_b], axis=0)
        kinds = [None] * 4
        kinds[GC_F], kinds[GC_GMAX], kinds[GC_EEND], kinds[GC_MLOC] = (f_f, f_b), (gm_f, gm_b), (ee_f, ee_b), (ml_f, ml_b)
        cols = jnp.concatenate([q for pair in kinds for q in pair], axis=0)
        pad = jnp.zeros((LANES - cols.shape[0], BLK), F32)
        for c0 in range(0, R, BLK):
            gcol_ref[0, r0 + c0:r0 + c0 + BLK, :] = jnp.concatenate([cols[:, c0:c0 + BLK], pad], axis=0).T
        sec = lambda s: _dot(h, w_ref[:, s[0]:s[1]])
        aq = sec(_AQ)
        ssq = _dot((aq * aq).astype(BF16), bdq_ref[...])
        aq_ref[0, rows, :] = (aq * lax.rsqrt(ssq * (1.0 / ATT_DH) + EPS) * qg_ref[...]).astype(BF16)
        ak = sec(_AK)
        ssk = _dot((ak * ak).astype(BF16), bdk_ref[...])
        ak_ref[0, rows, :] = (ak * lax.rsqrt(ssk * (1.0 / ATT_DH) + EPS) * kg_ref[...]).astype(BF16)
        av_ref[0, rows, :] = sec(_AV).astype(BF16)
        raws.append(sec(_MQK))
        mv_ref[0, rows, :] = sec(_MV).astype(BF16)
        mo_ref[0, rows, :] = sec(_MO).astype(BF16)
    i, ni = pl.program_id(1), pl.num_programs(1)
    h_halo = jnp.concatenate([_rms(xp_ref[0], g_ref[...]), _rms(xn_ref[0], g_ref[...])], axis=0).astype(BF16)
    halo = _dot(h_halo, w_ref[:, _MQK[0]:_MQK[1]])
    before = jnp.where(i == 0, 0.0, halo[HALO_X - 1:HALO_X, :])
    after = jnp.where(i == ni - 1, 0.0, halo[HALO_X:HALO_X + 1, :])
    raw = jnp.concatenate(raws, axis=0)
    r = _iota((tm, 1), 0)
    x_prev = jnp.where(r == 0, before, pltpu.roll(raw, 1, axis=0))
    x_next = jnp.where(r == tm - 1, after, pltpu.roll(raw, tm - 1, axis=0))
    y = cw_ref[0:1, :] * x_prev + cw_ref[1:2, :] * raw + cw_ref[2:3, :] * x_next + cb_ref[...]
    k_scale = jnp.where(_iota((1, y.shape[1]), 1) < ML_HEADS * ML_DH, 1.0, ML_DH ** -0.5)
    mqk_ref[0] = (y * _sigmoid(y) * k_scale).astype(BF16)


def _inproj(x, g, w_main, wg_t, gate_b, q_g, k_g, conv_w, conv_b, tm):
    B, S, D = x.shape
    per = tm // HALO_X
    last = S // HALO_X - 1
    bdq = (jnp.arange(512)[:, None] // ATT_DH == jnp.arange(512)[None, :] // ATT_DH).astype(BF16)
    bdk = bdq[:128, :128]
    R = min(INPROJ_ROWS, tm)
    s_from, s_to = jnp.arange(R)[:, None], jnp.arange(R)[None, :]
    same = s_from // BLK == s_to // BLK
    tri_f = (same & (s_from <= s_to)).astype(BF16)
    tri_b = (same & (s_from >= s_to)).astype(BF16)
    full = lambda *s: pl.BlockSpec(s, lambda b, i: (0,) * len(s))
    tok = lambda w: pl.BlockSpec((1, tm, w), lambda b, i: (b, i, 0))
    widths = (512, 128, 128, 1024, 512, 512)
    return pl.pallas_call(
        _inproj_kernel,
        grid=(B, S // tm),
        in_specs=[tok(D),
                  pl.BlockSpec((1, HALO_X, D), lambda b, i: (b, jnp.maximum(i * per - 1, 0), 0)),
                  pl.BlockSpec((1, HALO_X, D), lambda b, i: (b, jnp.minimum((i + 1) * per, last), 0)),
                  full(1, D), full(D, w_main.shape[1]), full(16, D), full(16, 1),
                  full(1, 512), full(1, 128), full(512, 512), full(128, 128), full(R, R), full(R, R), full(R, R),
                  full(3, 2 * ML_HEADS * ML_DH), full(1, 2 * ML_HEADS * ML_DH)],
        out_specs=[tok(w) for w in widths] + [pl.BlockSpec((1, 2 * ML_HEADS, tm), lambda b, i: (b, 0, i)), tok(LANES)],
        out_shape=[jax.ShapeDtypeStruct((B, S, w), BF16) for w in widths]
        + [jax.ShapeDtypeStruct((B, 2 * ML_HEADS, S), F32), jax.ShapeDtypeStruct((B, S, LANES), F32)],
        compiler_params=pltpu.CompilerParams(dimension_semantics=("arbitrary", "arbitrary"),
                                             vmem_limit_bytes=VMEM_LIMIT),
        name="inproj",
    )(x, x, x, g, w_main, wg_t, gate_b, q_g, k_g, bdq, bdk, tri_f, tri_b, same.astype(BF16), conv_w, conv_b)


ATT_GROUP = ATT_HEADS // ATT_KV
ATT_HEAD_ORDER = tuple(kv * ATT_GROUP + g for g in range(ATT_GROUP) for kv in range(ATT_KV))


ATT_QB = 8


def _attn_kernel(sink_ref, q_ref, kp_ref, kc_ref, kn_ref, vp_ref, vc_ref, vn_ref, bias_ref, o_ref):
    n = pl.program_id(1)
    nb = pl.num_programs(1)
    G, J = ATT_GROUP, ATT_QB
    kband = jnp.concatenate([kp_ref[0], kc_ref[0], kn_ref[0]], axis=0)
    vband = jnp.concatenate([vp_ref[0], vc_ref[0], vn_ref[0]], axis=0)
    first = _iota(kband.shape, 1) < ATT_DH
    kband, vband = kband.astype(F32), vband.astype(F32)
    keep_lanes = lambda a, mine: jnp.where(mine, a, 0.0).astype(BF16)
    si = _iota((1, (J + 2) * BLK), 1)
    edge = jnp.where(((si < BLK) & (n == 0)) | ((si >= (J + 1) * BLK) & (n == nb - 1)), NEG, 0.0)
    lane_first = _iota((1, LANES), 1) < ATT_DH
    k_kv = [keep_lanes(kband, first), keep_lanes(kband, ~first)]
    v_kv = [jnp.concatenate([keep_lanes(vband, m), m.astype(F32).astype(BF16)], axis=1) for m in (first, ~first)]
    for j in range(J):
        rows = slice(j * BLK, (j + 1) * BLK)
        band = slice(j * BLK, (j + 3) * BLK)
        q = q_ref[0, rows, :]
        qs = jnp.concatenate([q[:, g * LANES:(g + 1) * LANES] for g in range(G)], axis=0)
        tot = None
        sink_terms = []
        for kv in range(ATT_KV):
            s = _dot_nt(qs, k_kv[kv][band])
            ps, st = [], []
            for g in range(G):
                h = kv * G + g
                sink = sink_ref[h]
                sg = s[g * BLK:(g + 1) * BLK] + bias_ref[h] + edge[:, band]
                m = jnp.maximum(jnp.max(sg, axis=-1, keepdims=True), sink)
                ps.append(jnp.exp(sg - m).astype(BF16))
                st.append(jnp.exp(sink - m))
            part = _dot(jnp.concatenate(ps, axis=0), v_kv[kv][band])
            tot = part if tot is None else tot + part
            sink_terms.append(jnp.concatenate(st, axis=0))
        den = tot[:, LANES:] + jnp.where(lane_first, sink_terms[0], sink_terms[1])
        out = (tot[:, :LANES] / den).astype(BF16)
        o_ref[0, rows, :] = jnp.concatenate([out[g * BLK:(g + 1) * BLK] for g in range(G)], axis=1)


def _attention(sink, aq, ak, av):
    B, S, _ = aq.shape
    J = ATT_QB
    nb = S // (J * BLK)
    last_blk = S // BLK - 1
    kvw = ATT_KV * ATT_DH
    assert kvw == LANES and S % (J * BLK) == 0
    dist = jnp.abs(jnp.arange(BLK)[:, None] + BLK - jnp.arange(3 * BLK)[None, :]).astype(F32)
    slopes = jnp.exp2(-8.0 * jnp.arange(1, ATT_HEADS + 1, dtype=F32) / ATT_HEADS)
    bias = jnp.where(dist <= BLK, -slopes[:, None, None] * dist, NEG)
    prev = pl.BlockSpec((1, BLK, kvw), lambda b, n, *_: (b, jnp.maximum(J * n - 1, 0), 0))
    cur = pl.BlockSpec((1, J * BLK, kvw), lambda b, n, *_: (b, n, 0))
    nxt = pl.BlockSpec((1, BLK, kvw), lambda b, n, *_: (b, jnp.minimum(J * n + J, last_blk), 0))
    qspec = pl.BlockSpec((1, J * BLK, ATT_HEADS * ATT_DH), lambda b, n, *_: (b, n, 0))
    bspec = pl.BlockSpec((ATT_HEADS, BLK, 3 * BLK), lambda b, n, *_: (0, 0, 0))
    return pl.pallas_call(
        _attn_kernel,
        grid_spec=pltpu.PrefetchScalarGridSpec(
            num_scalar_prefetch=1, grid=(B, nb),
            in_specs=[qspec, prev, cur, nxt, prev, cur, nxt, bspec], out_specs=qspec),
        out_shape=jax.ShapeDtypeStruct(aq.shape, BF16),
        compiler_params=pltpu.CompilerParams(dimension_semantics=("arbitrary", "arbitrary"),
                                             vmem_limit_bytes=VMEM_LIMIT),
        name="win_attn",
    )(sink, aq, ak, ak, ak, av, av, av, bias)


ML_CPS = 8


def _mlstm_kernel(xf_ref, xb_ref, vf_ref, vb_ref, grf_ref, grb_ref, gcf_ref, gcb_ref, hf_ref, hb_ref, cn_ref, m_ref):
    c = pl.program_id(1)
    L = BLK
    width = ML_HEADS * ML_DH

    @pl.when(c == 0)
    def _():
        cn_ref[...] = jnp.zeros_like(cn_ref)
        m_ref[...] = jnp.zeros_like(m_ref)

    tt = _iota((L, L), 0)
    ss = _iota((L, L), 1)
    ones_v = jnp.ones((L, ML_DH), BF16)

    def chunk(d, sub, qk_ref, v_ref, grow_ref, gcol_ref, out_ref):
        H = ML_HEADS
        rows = slice(sub * L, (sub + 1) * L)
        keep = (ss <= tt) if d == 0 else (ss >= tt)
        g_row = grow_ref[0, H * d:H * (d + 1), rows]
        gc = gcol_ref[0, rows, :]
        kind = lambda i: gc[:, 2 * H * i + H * d:2 * H * i + H * (d + 1)]
        f, g_max, e_end, m_loc = kind(GC_F), kind(GC_GMAX), kind(GC_EEND), kind(GC_MLOC)[0:1, :]
        end = L - 1 if d == 0 else 0
        f_end = f[end:end + 1, :]
        m0 = m_ref[d, 0:1, 0:H]
        mm = jnp.maximum(m0, g_max)
        e_inter = jnp.exp(m0 - mm)
        floor = jnp.exp(-(f + mm))
        m_new = jnp.maximum(f_end + m0, m_loc)
        ca = jnp.exp(f_end + m0 - m_new)
        cb = jnp.exp(m_loc - m_new)
        m_ref[d, 0:1, 0:H] = m_new
        col = lambda a, h, w=ML_DH: jnp.broadcast_to(a[:, h:h + 1], (L, w))
        for h in range(H):
            u = H * d + h
            hs = slice(h * ML_DH, (h + 1) * ML_DH)
            qb = qk_ref[0, rows, hs]
            kb = qk_ref[0, rows, width + h * ML_DH: width + (h + 1) * ML_DH]
            v1 = jnp.concatenate([v_ref[0, rows, hs], ones_v], axis=1)
            cn0 = cn_ref[u]
            dec = jnp.where(keep, jnp.exp(g_row[h:h + 1, :] - col(mm, h)), 0.0)
            s_qk = (_dot_nt(qb, kb) * dec).astype(BF16)
            tot = col(e_inter, h, 2 * ML_DH) * _dot(qb, cn0.astype(BF16)) + _dot(s_qk, v1)
            out_ref[0, rows, hs] = tot[:, :ML_DH] / jnp.maximum(jnp.abs(tot[:, ML_DH:]), col(floor, h))
            ks = kb.astype(F32) * col(e_end, h)
            cn_ref[u] = ca[:, h:h + 1] * cn0 + cb[:, h:h + 1] * _dot(ks.T.astype(BF16), v1)

    for i in range(ML_CPS):
        chunk(0, i, xf_ref, vf_ref, grf_ref, gcf_ref, hf_ref)
        chunk(1, ML_CPS - 1 - i, xb_ref, vb_ref, grb_ref, gcb_ref, hb_ref)


def _mlstm(mqk, mv, g_rows, g_cols):
    B, S, _ = mqk.shape
    rows = ML_CPS * BLK
    ns = S // rows
    assert S % rows == 0
    width = ML_HEADS * ML_DH
    fwd = lambda c: c
    bwd = lambda c: ns - 1 - c
    qkspec = lambda ci: pl.BlockSpec((1, rows, 2 * width), lambda b, c: (b, ci(c), 0))
    vspec = lambda ci: pl.BlockSpec((1, rows, width), lambda b, c: (b, ci(c), 0))
    gspec = lambda ci: pl.BlockSpec((1, 2 * ML_HEADS, rows), lambda b, c: (b, 0, ci(c)))
    cspec = lambda ci: pl.BlockSpec((1, rows, LANES), lambda b, c: (b, ci(c), 0))
    units = 2 * ML_HEADS
    return pl.pallas_call(
        _mlstm_kernel,
        grid=(B, ns),
        in_specs=[qkspec(fwd), qkspec(bwd), vspec(fwd), vspec(bwd), gspec(fwd), gspec(bwd), cspec(fwd), cspec(bwd)],
        out_specs=[vspec(fwd), vspec(bwd)],
        out_shape=[jax.ShapeDtypeStruct((B, S, width), F32)] * 2,
        scratch_shapes=[pltpu.VMEM((units, ML_DH, 2 * ML_DH), F32), pltpu.VMEM((2, 8, LANES), F32)],
        compiler_params=pltpu.CompilerParams(dimension_semantics=("arbitrary", "arbitrary"),
                                             vmem_limit_bytes=VMEM_LIMIT),
        name="mlstm",
    )(mqk, mqk, mv, mv, g_rows, g_rows, g_cols, g_cols)


MIX_ROWS = 256


def _mix_kernel(x_ref, att_ref, hf_ref, hb_ref, mo_ref, og_ref, wo_ref,
                gm_ref, wq_ref, mqg_ref, k_ref, v_ref, wmo_ref,
                gf_ref, wr_ref, brt_ref,
                slab_ref, afft_ref):
    width = ML_HEADS * ML_DH
    k = k_ref[0]
    v = v_ref[0]
    wr = wr_ref[...]
    w_hi = wr.astype(BF16)
    w_lo = (wr - w_hi.astype(F32)).astype(BF16)
    for r0 in range(0, x_ref.shape[1], MIX_ROWS):
        rows = slice(r0, r0 + MIX_ROWS)
        x = x_ref[0, rows, :]
        ml = hf_ref[0, rows, :] + hb_ref[0, rows, :]
        mo = mo_ref[0, rows, :].astype(F32)
        parts = []
        for h in range(ML_HEADS):
            sl = slice(h * ML_DH, (h + 1) * ML_DH)
            parts.append((_sigmoid(mo[:, sl]) * _rms(ml[:, sl], og_ref[:, sl])).astype(BF16))
        ml_out = jnp.concatenate(parts, axis=1)
        y1 = x + _dot(att_ref[0, rows, :], wo_ref[0:width, :]) + _dot(ml_out, wo_ref[width:2 * width, :])
        h2 = _rms(y1, gm_ref[...]).astype(BF16)
        qm = _dot(h2, wq_ref[...])
        outs = []
        for h in range(MEM_HEADS):
            sl = slice(h * MEM_DH, (h + 1) * MEM_DH)
            qh = (_rms(qm[:, sl], mqg_ref[...]) * (MEM_DH ** -0.5)).astype(BF16)
            s = _dot_nt(qh, k[:, sl])
            p = jnp.exp(s - jnp.max(s, axis=-1, keepdims=True))
            o = _dot(p.astype(BF16), v[:, sl]) / jnp.sum(p, axis=-1, keepdims=True)
            outs.append(o.astype(BF16))
        y2 = y1 + _dot(jnp.concatenate(outs, axis=1), wmo_ref[...])
        h3 = _rms(y2, gf_ref[...])
        h_hi = h3.astype(BF16)
        h_lo = (h3 - h_hi.astype(F32)).astype(BF16)
        logits = _dot(h_hi, w_hi) + _dot(h_lo, w_hi) + _dot(h_hi, w_lo)
        logits_t = jnp.concatenate([logits[q0:q0 + LANES, :].T[:N_EXPERTS, :] for q0 in range(0, MIX_ROWS, LANES)],
                                   axis=1) + brt_ref[...]
        pt = jnp.exp(logits_t - jnp.max(logits_t, axis=0, keepdims=True))
        afft_ref[0, :, rows] = pt / jnp.sum(pt, axis=0, keepdims=True)
        chunks = y2.shape[1] // LANES
        for c in range(chunks):
            tiles = pl.ds(r0 * chunks + c, MIX_ROWS, stride=chunks)
            slab_ref[SLAB_ACC, tiles, :] = y2[:, c * LANES:(c + 1) * LANES]
            slab_ref[SLAB_X, tiles, :] = h3[:, c * LANES:(c + 1) * LANES]


def _mix(x, att, hf, hb, mo, out_g, w_out, g_mem, w_q, mq_g, mem_k, mem_v, w_mo, g_ffn, w_r, b_r, tm):
    B, S, D = x.shape
    M = mem_k.shape[1]
    width = ML_HEADS * ML_DH
    mw = MEM_HEADS * MEM_DH
    full = lambda *s: pl.BlockSpec(s, lambda b, i: (0,) * len(s))
    tok = lambda w: pl.BlockSpec((1, tm, w), lambda b, i: (b, i, 0))
    chunks = D // LANES
    tiled = pl.BlockSpec((2, tm * chunks, LANES), lambda b, i: (0, b * (S // tm) + i, 0))
    memspec = pl.BlockSpec((1, M, mw), lambda b, i: (b, 0, 0))
    tiled_shape = jax.ShapeDtypeStruct((2, B * S * chunks, LANES), F32)
    return pl.pallas_call(
        _mix_kernel,
        grid=(B, S // tm),
        in_specs=[tok(D), tok(width), tok(width), tok(width), tok(width), full(1, width), full(2 * width, D),
                  full(1, D), full(D, mw), full(1, MEM_DH), memspec, memspec, full(mw, D),
                  full(1, D), full(D, LANES), full(N_EXPERTS, 1)],
        out_specs=[tiled, pl.BlockSpec((1, N_EXPERTS, tm), lambda b, i: (b, 0, i))],
        out_shape=[tiled_shape, jax.ShapeDtypeStruct((B, N_EXPERTS, S), F32)],
        compiler_params=pltpu.CompilerParams(dimension_semantics=("arbitrary", "arbitrary"),
                                             vmem_limit_bytes=VMEM_LIMIT),
        name="mix_mem_router",
    )(x, att, hf, hb, mo, out_g, w_out, g_mem, w_q, mq_g, mem_k, mem_v, w_mo, g_ffn,
      jnp.pad(w_r, ((0, 0), (0, LANES - N_EXPERTS))), b_r.T)


def _topc_kernel(aff2_ref, aff3_ref, idx_ref, gval_ref, *, cap, seq, row_pitch):
    a2 = aff2_ref[0]
    bits2 = pltpu.bitcast(a2, I32)
    capf = float(cap)

    def bisect(i, lo):
        cand = lo | jnp.left_shift(jnp.int32(1), 30 - i)
        cnt = jnp.sum((bits2 >= cand).astype(F32), axis=1, keepdims=True)
        return jnp.where(cnt >= capf, cand, lo)

    thr_all = lax.fori_loop(0, 31, bisect, jnp.zeros((N_EXPERTS, 1), I32))
    need_all = capf - jnp.sum((bits2 > thr_all).astype(F32), axis=1, keepdims=True)

    T = aff3_ref.shape[2]
    tri_u = (_iota((LANES, LANES), 0) <= _iota((LANES, LANES), 1)).astype(BF16)
    tri_l = (_iota((LANES, LANES), 1) <= _iota((LANES, LANES), 0)).astype(BF16)
    ones8 = jnp.ones((8, LANES), BF16)
    before = _iota((T, T), 1) < _iota((T, T), 0)
    kcol = _iota((T, 1), 0).astype(F32)
    j = _iota((1, cap), 1).astype(F32)
    eye = (_iota((LANES, LANES), 0) == _iota((LANES, LANES), 1)).astype(BF16)
    lane_pos = _iota((LANES, cap), 0).astype(F32)

    def tile_starts(maskb):
        tot_row = _dot_nt(ones8, maskb)[0:1, :]
        return jnp.sum(jnp.where(before, tot_row, 0.0), axis=1, keepdims=True)

    for e in range(N_EXPERTS):
        bits = pltpu.bitcast(aff3_ref[0, e], I32)
        thr = thr_all[e:e + 1, :]
        need = need_all[e:e + 1, :]
        gt = bits > thr
        eq = bits == thr
        eqb = eq.astype(BF16)
        eq_rank = _dot(eqb, tri_u) + tile_starts(eqb) - eq.astype(F32)
        sel = gt | (eq & (eq_rank < need))
        selb = sel.astype(BF16)
        cs = _dot(selb, tri_u)
        start = tile_starts(selb)
        end = start + cs[:, LANES - 1:LANES]
        onehot = (start <= j) & (j < end)
        ohf = onehot.astype(F32)
        tile_of = jnp.sum(ohf * kcol, axis=0, keepdims=True)
        j_loc = j - jnp.sum(ohf * start, axis=0, keepdims=True)
        cs_t = _dot_nt(tri_l, selb)
        ohb = onehot.astype(BF16)
        r_t = _dot(cs_t.astype(BF16), ohb)
        local = jnp.sum((r_t <= j_loc).astype(F32), axis=0, keepdims=True)
        idx_ref[0, e:e + 1, :] = ((tile_of * LANES + local).astype(I32) + pl.program_id(0) * seq) * row_pitch
        a_t = _dot01_nt(eye, aff3_ref[0, e])
        a_tile = sum(_dot(term, ohb) for term in _split3(a_t))
        gval_ref[0, e:e + 1, :] = jnp.sum(jnp.where(lane_pos == local, a_tile, 0.0), axis=0, keepdims=True)


def _topc(aff_t, cap, row_pitch):
    B, E, S = aff_t.shape
    T = S // LANES
    aff3 = aff_t.reshape(B, E, T, LANES)
    return pl.pallas_call(
        functools.partial(_topc_kernel, cap=cap, seq=S, row_pitch=row_pitch),
        grid=(B,),
        in_specs=[pl.BlockSpec((1, E, S), lambda b: (b, 0, 0)), pl.BlockSpec((1, E, T, LANES), lambda b: (b, 0, 0, 0))],
        out_specs=[pl.BlockSpec((1, E, cap), lambda b: (b, 0, 0))] * 2,
        out_shape=[jax.ShapeDtypeStruct((B, E, cap), I32), jax.ShapeDtypeStruct((B, E, cap), F32)],
        compiler_params=pltpu.CompilerParams(dimension_semantics=("arbitrary",), vmem_limit_bytes=VMEM_LIMIT),
        name="topc",
    )(aff_t, aff3)


FFN_ROWS = 256
FFN_COLS = 256
AB_RING = 3


SLAB_X, SLAB_ACC = 0, 1


def _ffn_kernel(idxp_ref, idxn_ref, gv_ref, slab_in_hbm, wg_ref, wu_ref, wd_ref, slab_hbm,
                buf_ref, sems, *, nb, cap, d_model, ff):
    del slab_in_hbm
    s = pl.program_id(0)
    last = pl.num_programs(0) - 1
    a_cur = s % AB_RING
    a_nxt = (s + 1) % AB_RING
    a_prv = (s + 2) % AB_RING
    chunks = d_model // LANES

    def rows_of(start):
        return pl.ds(pl.multiple_of(start, chunks), chunks)

    def gather(r, j, sl):
        return pltpu.make_async_copy(slab_hbm.at[:, rows_of(r)], buf_ref.at[sl, :, rows_of(j * chunks)], sems.at[0])

    def scatter(r, j, sl):
        return pltpu.make_async_copy(buf_ref.at[sl, SLAB_ACC, rows_of(j * chunks)], slab_hbm.at[SLAB_ACC, rows_of(r)],
                                     sems.at[1])

    def wait_gather(sl):
        pltpu.make_async_copy(slab_hbm.at[:, pl.ds(0, cap * chunks)], buf_ref.at[sl], sems.at[0]).wait()

    def wait_scatter(sl):
        pltpu.make_async_copy(buf_ref.at[sl, SLAB_ACC], slab_hbm.at[SLAB_ACC, pl.ds(0, cap * chunks)],
                              sems.at[1]).wait()

    @pl.when(s == 0)
    def _():
        @pl.loop(0, cap)
        def _(j):
            r = idxp_ref[0, 0, j]
            gather(r, j, 0).start()
            gather(r, j, AB_RING - 1).start()
        wait_gather(AB_RING - 1)

    wait_gather(a_cur)

    halves = FFN_COLS // LANES
    nblk = cap // FFN_ROWS
    n_ct = ff // FFN_COLS
    assert d_model // FFN_COLS == n_ct

    n_updates = nblk * n_ct
    early = n_updates - n_ct if nblk > 1 else n_updates
    gather_rows = -(-cap // early)
    scatter_rows = -(-cap // n_updates)
    next_gather, next_scatter, n_issue = [0], [0], [0]

    def issue_rows():
        k = n_issue[0]
        n_issue[0] += 1
        lo, hi = next_gather[0], min(next_gather[0] + (gather_rows if k < early else 0), cap)
        next_gather[0] = hi
        for j in range(lo, hi):
            gather(idxn_ref[0, 0, j], j, a_nxt).start(priority=0)
        lo, hi = next_scatter[0], min(next_scatter[0] + scatter_rows, cap)
        next_scatter[0] = hi
        for j in range(lo, hi):
            scatter(idxp_ref[0, 0, j], j, a_prv).start(priority=1)

    def chunk_rows(rb, c):
        return pl.ds(rb * FFN_ROWS * chunks + c, FFN_ROWS, stride=chunks)

    def gate_rows(rb):
        g_row = gv_ref[0, rb]
        return jnp.concatenate(
            [jnp.broadcast_to(g_row[:, i * LANES:(i + 1) * LANES], (LANES, LANES)).T for i in range(FFN_ROWS // LANES)],
            axis=0)

    hid_prev = None
    for stage in range(nblk + 1):
        up, dn = stage < nblk, stage >= 1
        if up:
            xb = jnp.concatenate([buf_ref[a_cur, SLAB_X, chunk_rows(stage, c), :] for c in range(chunks)],
                                 axis=1).astype(BF16)
        if dn:
            gval = gate_rows(stage - 1)
        hid = []
        for ct in range(n_ct):
            cs = slice(ct * FFN_COLS, (ct + 1) * FFN_COLS)
            if up:
                hg = _dot(xb, wg_ref[0, :, cs])
                hu = _dot(xb, wu_ref[0, :, cs])
                hid.append((hg * _sigmoid(hg) * hu).astype(BF16))
            if dn:
                ye = _dot(hid_prev, wd_ref[0, :, cs])
                for i in range(halves):
                    buf_ref[a_cur, SLAB_ACC, chunk_rows(stage - 1, ct * halves + i), :] += (
                        ye[:, i * LANES:(i + 1) * LANES] * gval)
                issue_rows()
        hid_prev = jnp.concatenate(hid, axis=1) if up else None
    assert next_gather[0] == cap and next_scatter[0] == cap and n_issue[0] == n_updates

    wait_scatter(a_prv)

    @pl.when(s == last)
    def _():
        wait_gather(a_nxt)

        @pl.loop(0, cap)
        def _(j):
            scatter(idxn_ref[0, 0, j], j, a_cur).start()
        wait_scatter(a_cur)


def _expert_ffn(idx, gval, slab, wg, wu, wd):
    B, E, cap = idx.shape
    d_model, ff = wg.shape[1], wg.shape[2]
    chunks = d_model // LANES
    assert B >= 3 and cap % FFN_ROWS == 0 and ff % FFN_COLS == 0 and d_model % FFN_COLS == 0
    ns = E * B
    nblk = cap // FFN_ROWS
    idx3 = idx.reshape(B * E, 1, cap)
    gv4 = gval.reshape(B * E, nblk, 1, FFN_ROWS)
    any_spec = pl.BlockSpec(memory_space=pl.ANY)
    wspec = lambda r, c: pl.BlockSpec((1, r, c), lambda s: (s // B, 0, 0))
    blk = lambda s: (s % B) * E + s // B
    ispec = lambda f: pl.BlockSpec((1, 1, cap), lambda s: (blk(f(s)), 0, 0), memory_space=pltpu.SMEM)
    kern = functools.partial(_ffn_kernel, nb=B, cap=cap, d_model=d_model, ff=ff)
    return pl.pallas_call(
        kern,
        grid=(ns,),
        in_specs=[ispec(lambda s: jnp.maximum(s - 1, 0)), ispec(lambda s: jnp.minimum(s + 1, ns - 1)),
                  pl.BlockSpec((1, nblk, 1, FFN_ROWS), lambda s: (blk(s), 0, 0, 0)),
                  any_spec, wspec(d_model, ff), wspec(d_model, ff), wspec(ff, d_model)],
        out_specs=any_spec,
        out_shape=jax.ShapeDtypeStruct(slab.shape, F32),
        scratch_shapes=[pltpu.VMEM((AB_RING, 2, cap * chunks, LANES), F32), pltpu.SemaphoreType.DMA((2,))],
        input_output_aliases={3: 0},
        compiler_params=pltpu.CompilerParams(dimension_semantics=("arbitrary",), vmem_limit_bytes=VMEM_LIMIT),
        name="expert_ffn",
    )(idx3, idx3, gv4, slab, wg, wu, wd)


def _untile_kernel(a_ref, o_ref):
    tm, chunks = o_ref.shape[0], o_ref.shape[1] // LANES
    for c in range(chunks):
        o_ref[:, c * LANES:(c + 1) * LANES] = a_ref[0, pl.ds(c, tm, stride=chunks), :]


def _untile(slab, plane, chunks, tm):
    N = slab.shape[1] // chunks
    return pl.pallas_call(
        _untile_kernel,
        grid=(N // tm,),
        in_specs=[pl.BlockSpec((1, tm * chunks, LANES), lambda i: (plane, i, 0))],
        out_specs=pl.BlockSpec((tm, chunks * LANES), lambda i: (i, 0)),
        out_shape=jax.ShapeDtypeStruct((N, chunks * LANES), F32),
        compiler_params=pltpu.CompilerParams(dimension_semantics=("arbitrary",), vmem_limit_bytes=VMEM_LIMIT),
        name="untile",
    )(slab)


def _layer(x, mem, norm_mix_g, w_in, att_q_norm_g, att_k_norm_g, att_sink, ml_conv_w, ml_conv_b,
           ml_gate_b, ml_out_norm_g, w_out, norm_mem_g, mem_kv_norm_g, w_mem_q, w_mem_kv,
           mem_q_norm_g, mem_k_norm_g, w_mem_o, norm_ffn_g, w_router, b_router,
           w_exp_gate, w_exp_up, w_exp_down):
    B, S, D = x.shape
    row = lambda v: v.reshape(1, -1).astype(F32)
    tm_in = min(1024, S)
    tm_mix = min(2 * MIX_ROWS, S)
    cap = CAPACITY_FACTOR * S // N_EXPERTS

    mem_k, mem_v = _mem_kv(mem, row(mem_kv_norm_g), w_mem_kv.astype(BF16), row(mem_k_norm_g))

    order = jnp.array(ATT_HEAD_ORDER)
    w_aq = w_in[:, _AQ[0]:_AQ[1]].reshape(D, ATT_HEADS, ATT_DH)[:, order].reshape(D, -1)
    w_main = jnp.concatenate([w_aq, w_in[:, _AQ[1]:_GATES[0]]], axis=1).astype(BF16)
    wg_t = w_in[:, _GATES[0]:_GATES[1]].T.astype(BF16)
    q_g = jnp.tile(row(att_q_norm_g), (1, ATT_HEADS)) * (ATT_DH ** -0.5)
    k_g = jnp.tile(row(att_k_norm_g), (1, ATT_KV))
    aq, ak, av, mqk, mv, mo, g_rows, g_cols = _inproj(x, row(norm_mix_g), w_main, wg_t, ml_gate_b.reshape(-1, 1).astype(F32),
                                               q_g, k_g, ml_conv_w.astype(F32), row(ml_conv_b), tm_in)

    att = _attention(att_sink.astype(F32), aq, ak, av)
    hf, hb = _mlstm(mqk, mv, g_rows, g_cols)

    att_w = ATT_HEADS * ATT_DH
    w_out_att = w_out[:att_w].reshape(ATT_HEADS, ATT_DH, D)[order].reshape(att_w, D)
    w_out_p = jnp.concatenate([w_out_att, w_out[att_w:]], axis=0).astype(BF16)
    slab, aff_t = _mix(x, att, hf, hb, mo, row(ml_out_norm_g), w_out_p, row(norm_mem_g),
                             w_mem_q.astype(BF16), row(mem_q_norm_g), mem_k, mem_v, w_mem_o.astype(BF16),
                             row(norm_ffn_g), w_router.astype(F32), row(b_router), tm_mix)

    idx, gval = _topc(aff_t, cap, D // LANES)
    slab = _expert_ffn(idx, gval, slab, w_exp_gate.astype(BF16), w_exp_up.astype(BF16), w_exp_down.astype(BF16))
    return _untile(slab, SLAB_ACC, D // LANES, tm_in).reshape(B, S, D)


def kernel(x, mem, norm_mix_g, w_in, att_q_norm_g, att_k_norm_g, att_sink, ml_conv_w, ml_conv_b, ml_gate_b,
           ml_out_norm_g, w_out, norm_mem_g, mem_kv_norm_g, w_mem_q, w_mem_kv, mem_q_norm_g, mem_k_norm_g,
           w_mem_o, norm_ffn_g, w_router, b_router, w_exp_gate, w_exp_up, w_exp_down):
    params = (norm_mix_g, w_in, att_q_norm_g, att_k_norm_g, att_sink, ml_conv_w, ml_conv_b, ml_gate_b,
              ml_out_norm_g, w_out, norm_mem_g, mem_kv_norm_g, w_mem_q, w_mem_kv, mem_q_norm_g, mem_k_norm_g,
              w_mem_o, norm_ffn_g, w_router, b_router, w_exp_gate, w_exp_up, w_exp_down)
    depth = norm_mix_g.shape[0]
    for l in range(depth):
        x = _layer(x, mem, *[p[l] for p in params])
    return x
```

```python
import functools

import jax
import jax.numpy as jnp
from jax import lax
from jax.experimental import pallas as pl
from jax.experimental.pallas import tpu as pltpu

F32 = jnp.float32
BF16 = jnp.bfloat16
I32 = jnp.int32

EPS = 1e-6
LANES = 128
BLK = 128
ATT_HEADS, ATT_KV, ATT_DH = 8, 2, 64
ML_HEADS, ML_DH = 4, 128
MEM_HEADS, MEM_DH = 4, 128
N_EXPERTS = 16
CAPACITY_FACTOR = 2
NEG = -1e30
HALO_X = 8
VMEM_LIMIT = 48 * 1024 * 1024

_NT = (((1,), (1,)), ((), ()))


def _dot(a, b):
    return jnp.dot(a, b, preferred_element_type=F32)


def _dot_nt(a, b):
    return lax.dot_general(a, b, _NT, preferred_element_type=F32)


def _split3(x):
    hi = x.astype(BF16)
    r1 = x - hi.astype(F32)
    mid = r1.astype(BF16)
    lo = (r1 - mid.astype(F32)).astype(BF16)
    return hi, mid, lo


def _dot01_nt(m01, x):
    hi, mid, lo = _split3(x)
    return _dot_nt(m01, hi) + _dot_nt(m01, mid) + _dot_nt(m01, lo)


def _dot01(x, m01):
    hi, mid, lo = _split3(x)
    return _dot(hi, m01) + _dot(mid, m01) + _dot(lo, m01)


def _rms(x, g):
    ms = jnp.mean(x * x, axis=-1, keepdims=True)
    return x * lax.rsqrt(ms + EPS) * g


def _sigmoid(x):
    return 1.0 / (1.0 + jnp.exp(-x))


def _iota(shape, dim):
    return lax.broadcasted_iota(I32, shape, dim)


def _mem_kv_kernel(mem_ref, g_ref, w_ref, kg_ref, k_ref, v_ref):
    mn = _rms(mem_ref[0], g_ref[...]).astype(BF16)
    kv = _dot(mn, w_ref[...])
    width = MEM_HEADS * MEM_DH
    for h in range(MEM_HEADS):
        sl = slice(h * MEM_DH, (h + 1) * MEM_DH)
        k_ref[0, :, sl] = _rms(kv[:, sl], kg_ref[...]).astype(BF16)
    v_ref[0] = kv[:, width:].astype(BF16)


def _mem_kv(mem, g, w_kv, k_g):
    B, M, D = mem.shape
    width = MEM_HEADS * MEM_DH
    full = lambda *s: pl.BlockSpec(s, lambda b: (0,) * len(s))
    return pl.pallas_call(
        _mem_kv_kernel,
        grid=(B,),
        in_specs=[pl.BlockSpec((1, M, D), lambda b: (b, 0, 0)), full(1, D), full(D, 2 * width), full(1, MEM_DH)],
        out_specs=[pl.BlockSpec((1, M, width), lambda b: (b, 0, 0))] * 2,
        out_shape=[jax.ShapeDtypeStruct((B, M, width), BF16)] * 2,
        compiler_params=pltpu.CompilerParams(dimension_semantics=("arbitrary",), vmem_limit_bytes=VMEM_LIMIT),
        name="mem_kv",
    )(mem, g, w_kv, k_g)


_AQ = (0, 512)
_AK = (512, 640)
_AV = (640, 768)
_MQK = (768, 1792)
_MV = (1792, 2304)
_MO = (2304, 2816)
_GATES = (2816, 2832)


GC_F, GC_GMAX, GC_EEND, GC_MLOC = 0, 1, 2, 3


INPROJ_ROWS = 256


def _inproj_kernel(x_ref, xp_ref, xn_ref, g_ref, w_ref, wgt_ref, gb_ref, qg_ref, kg_ref, bdq_ref, bdk_ref, trif_ref,
                   trib_ref, blk_ref, cw_ref, cb_ref, aq_ref, ak_ref, av_ref, mqk_ref, mv_ref, mo_ref, grow_ref, gcol_ref):
    tm = x_ref.shape[1]
    R = INPROJ_ROWS
    H = ML_HEADS
    pos = _iota((H, R), 1) % BLK
    raws = []
    for r0 in range(0, tm, R):
        rows = slice(r0, r0 + R)
        h = _rms(x_ref[0, rows, :], g_ref[...]).astype(BF16)
        gt = _dot_nt(wgt_ref[...], h) + gb_ref[...]
        logsig = jnp.minimum(gt, 0.0) - jnp.log(1.0 + jnp.exp(-jnp.abs(gt)))

        def chunk_scan(li, lf, tri_ref, fwd):
            f = _dot01(lf, tri_ref[...])
            g = li - f
            gmax = g
            sh = 1
            while sh < BLK:
                if fwd:
                    gmax = jnp.where(pos >= sh, jnp.maximum(gmax, pltpu.roll(gmax, sh, axis=1)), gmax)
                else:
                    gmax = jnp.where(pos < BLK - sh, jnp.maximum(gmax, pltpu.roll(gmax, R - sh, axis=1)), gmax)
                sh *= 2
            w_end = _dot01(lf, blk_ref[...]) + g
            m_loc = jnp.concatenate(
                [jnp.broadcast_to(jnp.max(w_end[:, c0:c0 + BLK], axis=1, keepdims=True), (H, BLK))
                 for c0 in range(0, R, BLK)], axis=1)
            return f, g, gmax, jnp.exp(w_end - m_loc), m_loc

        f_f, g_f, gm_f, ee_f, ml_f = chunk_scan(gt[0:H], logsig[H:2 * H], trif_ref, True)
        f_b, g_b, gm_b, ee_b, ml_b = chunk_scan(gt[2 * H:3 * H], logsig[3 * H:4 * H], trib_ref, False)
        grow_ref[0, :, rows] = jnp.concatenate([g_f, g_b], axis=0)
        kinds = [None] * 4
        kinds[GC_F], kinds[GC_GMAX], kinds[GC_EEND], kinds[GC_MLOC] = (f_f, f_b), (gm_f, gm_b), (ee_f, ee_b), (ml_f, ml_b)
        cols = jnp.concatenate([q for pair in kinds for q in pair], axis=0)
        pad = jnp.zeros((LANES - cols.shape[0], BLK), F32)
        for c0 in range(0, R, BLK):
            gcol_ref[0, r0 + c0:r0 + c0 + BLK, :] = jnp.concatenate([cols[:, c0:c0 + BLK], pad], axis=0).T
        sec = lambda s: _dot(h, w_ref[:, s[0]:s[1]])
        aq = sec(_AQ)
        ssq = _dot((aq * aq).astype(BF16), bdq_ref[...])
        aq_ref[0, rows, :] = (aq * lax.rsqrt(ssq * (1.0 / ATT_DH) + EPS) * qg_ref[...]).astype(BF16)
        ak = sec(_AK)
        ssk = _dot((ak * ak).astype(BF16), bdk_ref[...])
        ak_ref[0, rows, :] = (ak * lax.rsqrt(ssk * (1.0 / ATT_DH) + EPS) * kg_ref[...]).astype(BF16)
        av_ref[0, rows, :] = sec(_AV).astype(BF16)
        raws.append(sec(_MQK))
        mv_ref[0, rows, :] = sec(_MV).astype(BF16)
        mo_ref[0, rows, :] = sec(_MO).astype(BF16)
    i, ni = pl.program_id(1), pl.num_programs(1)
    h_halo = jnp.concatenate([_rms(xp_ref[0], g_ref[...]), _rms(xn_ref[0], g_ref[...])], axis=0).astype(BF16)
    halo = _dot(h_halo, w_ref[:, _MQK[0]:_MQK[1]])
    before = jnp.where(i == 0, 0.0, halo[HALO_X - 1:HALO_X, :])
    after = jnp.where(i == ni - 1, 0.0, halo[HALO_X:HALO_X + 1, :])
    raw = jnp.concatenate(raws, axis=0)
    r = _iota((tm, 1), 0)
    x_prev = jnp.where(r == 0, before, pltpu.roll(raw, 1, axis=0))
    x_next = jnp.where(r == tm - 1, after, pltpu.roll(raw, tm - 1, axis=0))
    y = cw_ref[0:1, :] * x_prev + cw_ref[1:2, :] * raw + cw_ref[2:3, :] * x_next + cb_ref[...]
    k_scale = jnp.where(_iota((1, y.shape[1]), 1) < ML_HEADS * ML_DH, 1.0, ML_DH ** -0.5)
    mqk_ref[0] = (y * _sigmoid(y) * k_scale).astype(BF16)


def _inproj(x, g, w_main, wg_t, gate_b, q_g, k_g, conv_w, conv_b, tm):
    B, S, D = x.shape
    per = tm // HALO_X
    last = S // HALO_X - 1
    bdq = (jnp.arange(512)[:, None] // ATT_DH == jnp.arange(512)[None, :] // ATT_DH).astype(BF16)
    bdk = bdq[:128, :128]
    R = min(INPROJ_ROWS, tm)
    s_from, s_to = jnp.arange(R)[:, None], jnp.arange(R)[None, :]
    same = s_from // BLK == s_to // BLK
    tri_f = (same & (s_from <= s_to)).astype(BF16)
    tri_b = (same & (s_from >= s_to)).astype(BF16)
    full = lambda *s: pl.BlockSpec(s, lambda b, i: (0,) * len(s))
    tok = lambda w: pl.BlockSpec((1, tm, w), lambda b, i: (b, i, 0))
    widths = (512, 128, 128, 1024, 512, 512)
    return pl.pallas_call(
        _inproj_kernel,
        grid=(B, S // tm),
        in_specs=[tok(D),
                  pl.BlockSpec((1, HALO_X, D), lambda b, i: (b, jnp.maximum(i * per - 1, 0), 0)),
                  pl.BlockSpec((1, HALO_X, D), lambda b, i: (b, jnp.minimum((i + 1) * per, last), 0)),
                  full(1, D), full(D, w_main.shape[1]), full(16, D), full(16, 1),
                  full(1, 512), full(1, 128), full(512, 512), full(128, 128), full(R, R), full(R, R), full(R, R),
                  full(3, 2 * ML_HEADS * ML_DH), full(1, 2 * ML_HEADS * ML_DH)],
        out_specs=[tok(w) for w in widths] + [pl.BlockSpec((1, 2 * ML_HEADS, tm), lambda b, i: (b, 0, i)), tok(LANES)],
        out_shape=[jax.ShapeDtypeStruct((B, S, w), BF16) for w in widths]
        + [jax.ShapeDtypeStruct((B, 2 * ML_HEADS, S), F32), jax.ShapeDtypeStruct((B, S, LANES), F32)],
        compiler_params=pltpu.CompilerParams(dimension_semantics=("arbitrary", "arbitrary"),
                                             vmem_limit_bytes=VMEM_LIMIT),
        name="inproj",
    )(x, x, x, g, w_main, wg_t, gate_b, q_g, k_g, bdq, bdk, tri_f, tri_b, same.astype(BF16), conv_w, conv_b)


ATT_GROUP = ATT_HEADS // ATT_KV
ATT_HEAD_ORDER = tuple(kv * ATT_GROUP + g for g in range(ATT_GROUP) for kv in range(ATT_KV))


ATT_QB = 8


def _attn_kernel(sink_ref, q_ref, kp_ref, kc_ref, kn_ref, vp_ref, vc_ref, vn_ref, bias_ref, o_ref):
    n = pl.program_id(1)
    nb = pl.num_programs(1)
    G, J = ATT_GROUP, ATT_QB
    kband = jnp.concatenate([kp_ref[0], kc_ref[0], kn_ref[0]], axis=0)
    vband = jnp.concatenate([vp_ref[0], vc_ref[0], vn_ref[0]], axis=0)
    first = _iota(kband.shape, 1) < ATT_DH
    kband, vband = kband.astype(F32), vband.astype(F32)
    keep_lanes = lambda a, mine: jnp.where(mine, a, 0.0).astype(BF16)
    si = _iota((1, (J + 2) * BLK), 1)
    edge = jnp.where(((si < BLK) & (n == 0)) | ((si >= (J + 1) * BLK) & (n == nb - 1)), NEG, 0.0)
    lane_first = _iota((1, LANES), 1) < ATT_DH
    k_kv = [keep_lanes(kband, first), keep_lanes(kband, ~first)]
    v_kv = [jnp.concatenate([keep_lanes(vband, m), m.astype(F32).astype(BF16)], axis=1) for m in (first, ~first)]
    for j in range(J):
        rows = slice(j * BLK, (j + 1) * BLK)
        band = slice(j * BLK, (j + 3) * BLK)
        q = q_ref[0, rows, :]
        qs = jnp.concatenate([q[:, g * LANES:(g + 1) * LANES] for g in range(G)], axis=0)
        tot = None
        sink_terms = []
        for kv in range(ATT_KV):
            s = _dot_nt(qs, k_kv[kv][band])
            ps, st = [], []
            for g in range(G):
                h = kv * G + g
                sink = sink_ref[h]
                sg = s[g * BLK:(g + 1) * BLK] + bias_ref[h] + edge[:, band]
                m = jnp.maximum(jnp.max(sg, axis=-1, keepdims=True), sink)
                ps.append(jnp.exp(sg - m).astype(BF16))
                st.append(jnp.exp(sink - m))
            part = _dot(jnp.concatenate(ps, axis=0), v_kv[kv][band])
            tot = part if tot is None else tot + part
            sink_terms.append(jnp.concatenate(st, axis=0))
        den = tot[:, LANES:] + jnp.where(lane_first, sink_terms[0], sink_terms[1])
        out = (tot[:, :LANES] / den).astype(BF16)
        o_ref[0, rows, :] = jnp.concatenate([out[g * BLK:(g + 1) * BLK] for g in range(G)], axis=1)


def _attention(sink, aq, ak, av):
    B, S, _ = aq.shape
    J = ATT_QB
    nb = S // (J * BLK)
    last_blk = S // BLK - 1
    kvw = ATT_KV * ATT_DH
    assert kvw == LANES and S % (J * BLK) == 0
    dist = jnp.abs(jnp.arange(BLK)[:, None] + BLK - jnp.arange(3 * BLK)[None, :]).astype(F32)
    slopes = jnp.exp2(-8.0 * jnp.arange(1, ATT_HEADS + 1, dtype=F32) / ATT_HEADS)
    bias = jnp.where(dist <= BLK, -slopes[:, None, None] * dist, NEG)
    prev = pl.BlockSpec((1, BLK, kvw), lambda b, n, *_: (b, jnp.maximum(J * n - 1, 0), 0))
    cur = pl.BlockSpec((1, J * BLK, kvw), lambda b, n, *_: (b, n, 0))
    nxt = pl.BlockSpec((1, BLK, kvw), lambda b, n, *_: (b, jnp.minimum(J * n + J, last_blk), 0))
    qspec = pl.BlockSpec((1, J * BLK, ATT_HEADS * ATT_DH), lambda b, n, *_: (b, n, 0))
    bspec = pl.BlockSpec((ATT_HEADS, BLK, 3 * BLK), lambda b, n, *_: (0, 0, 0))
    return pl.pallas_call(
        _attn_kernel,
        grid_spec=pltpu.PrefetchScalarGridSpec(
            num_scalar_prefetch=1, grid=(B, nb),
            in_specs=[qspec, prev, cur, nxt, prev, cur, nxt, bspec], out_specs=qspec),
        out_shape=jax.ShapeDtypeStruct(aq.shape, BF16),
        compiler_params=pltpu.CompilerParams(dimension_semantics=("arbitrary", "arbitrary"),
                                             vmem_limit_bytes=VMEM_LIMIT),
        name="win_attn",
    )(sink, aq, ak, ak, ak, av, av, av, bias)


ML_CPS = 8


def _mlstm_kernel(xf_ref, xb_ref, vf_ref, vb_ref, grf_ref, grb_ref, gcf_ref, gcb_ref, hf_ref, hb_ref, cn_ref, m_ref):
    c = pl.program_id(1)
    L = BLK
    width = ML_HEADS * ML_DH

    @pl.when(c == 0)
    def _():
        cn_ref[...] = jnp.zeros_like(cn_ref)
        m_ref[...] = jnp.zeros_like(m_ref)

    tt = _iota((L, L), 0)
    ss = _iota((L, L), 1)
    ones_v = jnp.ones((L, ML_DH), BF16)

    def chunk(d, sub, qk_ref, v_ref, grow_ref, gcol_ref, out_ref):
        H = ML_HEADS
        rows = slice(sub * L, (sub + 1) * L)
        keep = (ss <= tt) if d == 0 else (ss >= tt)
        g_row = grow_ref[0, H * d:H * (d + 1), rows]
        gc = gcol_ref[0, rows, :]
        kind = lambda i: gc[:, 2 * H * i + H * d:2 * H * i + H * (d + 1)]
        f, g_max, e_end, m_loc = kind(GC_F), kind(GC_GMAX), kind(GC_EEND), kind(GC_MLOC)[0:1, :]
        end = L - 1 if d == 0 else 0
        f_end = f[end:end + 1, :]
        m0 = m_ref[d, 0:1, 0:H]
        mm = jnp.maximum(m0, g_max)
        e_inter = jnp.exp(m0 - mm)
        floor = jnp.exp(-(f + mm))
        m_new = jnp.maximum(f_end + m0, m_loc)
        ca = jnp.exp(f_end + m0 - m_new)
        cb = jnp.exp(m_loc - m_new)
        m_ref[d, 0:1, 0:H] = m_new
        col = lambda a, h, w=ML_DH: jnp.broadcast_to(a[:, h:h + 1], (L, w))
        for h in range(H):
            u = H * d + h
            hs = slice(h * ML_DH, (h + 1) * ML_DH)
            qb = qk_ref[0, rows, hs]
            kb = qk_ref[0, rows, width + h * ML_DH: width + (h + 1) * ML_DH]
            v1 = jnp.concatenate([v_ref[0, rows, hs], ones_v], axis=1)
            cn0 = cn_ref[u]
            dec = jnp.where(keep, jnp.exp(g_row[h:h + 1, :] - col(mm, h)), 0.0)
            s_qk = (_dot_nt(qb, kb) * dec).astype(BF16)
            tot = col(e_inter, h, 2 * ML_DH) * _dot(qb, cn0.astype(BF16)) + _dot(s_qk, v1)
            out_ref[0, rows, hs] = tot[:, :ML_DH] / jnp.maximum(jnp.abs(tot[:, ML_DH:]), col(floor, h))
            ks = kb.astype(F32) * col(e_end, h)
            cn_ref[u] = ca[:, h:h + 1] * cn0 + cb[:, h:h + 1] * _dot(ks.T.astype(BF16), v1)

    for i in range(ML_CPS):
        chunk(0, i, xf_ref, vf_ref, grf_ref, gcf_ref, hf_ref)
        chunk(1, ML_CPS - 1 - i, xb_ref, vb_ref, grb_ref, gcb_ref, hb_ref)


def _mlstm(mqk, mv, g_rows, g_cols):
    B, S, _ = mqk.shape
    rows = ML_CPS * BLK
    ns = S // rows
    assert S % rows == 0
    width = ML_HEADS * ML_DH
    fwd = lambda c: c
    bwd = lambda c: ns - 1 - c
    qkspec = lambda ci: pl.BlockSpec((1, rows, 2 * width), lambda b, c: (b, ci(c), 0))
    vspec = lambda ci: pl.BlockSpec((1, rows, width), lambda b, c: (b, ci(c), 0))
    gspec = lambda ci: pl.BlockSpec((1, 2 * ML_HEADS, rows), lambda b, c: (b, 0, ci(c)))
    cspec = lambda ci: pl.BlockSpec((1, rows, LANES), lambda b, c: (b, ci(c), 0))
    units = 2 * ML_HEADS
    return pl.pallas_call(
        _mlstm_kernel,
        grid=(B, ns),
        in_specs=[qkspec(fwd), qkspec(bwd), vspec(fwd), vspec(bwd), gspec(fwd), gspec(bwd), cspec(fwd), cspec(bwd)],
        out_specs=[vspec(fwd), vspec(bwd)],
        out_shape=[jax.ShapeDtypeStruct((B, S, width), F32)] * 2,
        scratch_shapes=[pltpu.VMEM((units, ML_DH, 2 * ML_DH), F32), pltpu.VMEM((2, 8, LANES), F32)],
        compiler_params=pltpu.CompilerParams(dimension_semantics=("arbitrary", "arbitrary"),
                                             vmem_limit_bytes=VMEM_LIMIT),
        name="mlstm",
    )(mqk, mqk, mv, mv, g_rows, g_rows, g_cols, g_cols)


MIX_ROWS = 256


def _mix_kernel(x_ref, att_ref, hf_ref, hb_ref, mo_ref, og_ref, wo_ref,
                gm_ref, wq_ref, mqg_ref, k_ref, v_ref, wmo_ref,
                gf_ref, wr_ref, brt_ref,
                slab_ref, afft_ref):
    width = ML_HEADS * ML_DH
    k = k_ref[0]
    v = v_ref[0]
    wr = wr_ref[...]
    w_hi = wr.astype(BF16)
    w_lo = (wr - w_hi.astype(F32)).astype(BF16)
    for r0 in range(0, x_ref.shape[1], MIX_ROWS):
        rows = slice(r0, r0 + MIX_ROWS)
        x = x_ref[0, rows, :]
        ml = hf_ref[0, rows, :] + hb_ref[0, rows, :]
        mo = mo_ref[0, rows, :].astype(F32)
        parts = []
        for h in range(ML_HEADS):
            sl = slice(h * ML_DH, (h + 1) * ML_DH)
            parts.append((_sigmoid(mo[:, sl]) * _rms(ml[:, sl], og_ref[:, sl])).astype(BF16))
        ml_out = jnp.concatenate(parts, axis=1)
        y1 = x + _dot(att_ref[0, rows, :], wo_ref[0:width, :]) + _dot(ml_out, wo_ref[width:2 * width, :])
        h2 = _rms(y1, gm_ref[...]).astype(BF16)
        qm = _dot(h2, wq_ref[...])
        outs = []
        for h in range(MEM_HEADS):
            sl = slice(h * MEM_DH, (h + 1) * MEM_DH)
            qh = (_rms(qm[:, sl], mqg_ref[...]) * (MEM_DH ** -0.5)).astype(BF16)
            s = _dot_nt(qh, k[:, sl])
            p = jnp.exp(s - jnp.max(s, axis=-1, keepdims=True))
            o = _dot(p.astype(BF16), v[:, sl]) / jnp.sum(p, axis=-1, keepdims=True)
            outs.append(o.astype(BF16))
        y2 = y1 + _dot(jnp.concatenate(outs, axis=1), wmo_ref[...])
        h3 = _rms(y2, gf_ref[...])
        h_hi = h3.astype(BF16)
        h_lo = (h3 - h_hi.astype(F32)).astype(BF16)
        logits = _dot(h_hi, w_hi) + _dot(h_lo, w_hi) + _dot(h_hi, w_lo)
        logits_t = jnp.concatenate([logits[q0:q0 + LANES, :].T[:N_EXPERTS, :] for q0 in range(0, MIX_ROWS, LANES)],
                                   axis=1) + brt_ref[...]
        pt = jnp.exp(logits_t - jnp.max(logits_t, axis=0, keepdims=True))
        afft_ref[0, :, rows] = pt / jnp.sum(pt, axis=0, keepdims=True)
        chunks = y2.shape[1] // LANES
        for c in range(chunks):
            tiles = pl.ds(r0 * chunks + c, MIX_ROWS, stride=chunks)
            slab_ref[SLAB_ACC, tiles, :] = y2[:, c * LANES:(c + 1) * LANES]
            slab_ref[SLAB_X, tiles, :] = h3[:, c * LANES:(c + 1) * LANES]


def _mix(x, att, hf, hb, mo, out_g, w_out, g_mem, w_q, mq_g, mem_k, mem_v, w_mo, g_ffn, w_r, b_r, tm):
    B, S, D = x.shape
    M = mem_k.shape[1]
    width = ML_HEADS * ML_DH
    mw = MEM_HEADS * MEM_DH
    full = lambda *s: pl.BlockSpec(s, lambda b, i: (0,) * len(s))
    tok = lambda w: pl.BlockSpec((1, tm, w), lambda b, i: (b, i, 0))
    chunks = D // LANES
    tiled = pl.BlockSpec((2, tm * chunks, LANES), lambda b, i: (0, b * (S // tm) + i, 0))
    memspec = pl.BlockSpec((1, M, mw), lambda b, i: (b, 0, 0))
    tiled_shape = jax.ShapeDtypeStruct((2, B * S * chunks, LANES), F32)
    return pl.pallas_call(
        _mix_kernel,
        grid=(B, S // tm),
        in_specs=[tok(D), tok(width), tok(width), tok(width), tok(width), full(1, width), full(2 * width, D),
                  full(1, D), full(D, mw), full(1, MEM_DH), memspec, memspec, full(mw, D),
                  full(1, D), full(D, LANES), full(N_EXPERTS, 1)],
        out_specs=[tiled, pl.BlockSpec((1, N_EXPERTS, tm), lambda b, i: (b, 0, i))],
        out_shape=[tiled_shape, jax.ShapeDtypeStruct((B, N_EXPERTS, S), F32)],
        compiler_params=pltpu.CompilerParams(dimension_semantics=("arbitrary", "arbitrary"),
                                             vmem_limit_bytes=VMEM_LIMIT),
        name="mix_mem_router",
    )(x, att, hf, hb, mo, out_g, w_out, g_mem, w_q, mq_g, mem_k, mem_v, w_mo, g_ffn,
      jnp.pad(w_r, ((0, 0), (0, LANES - N_EXPERTS))), b_r.T)


def _topc_kernel(aff2_ref, aff3_ref, idx_ref, gval_ref, *, cap, seq, row_pitch):
    a2 = aff2_ref[0]
    bits2 = pltpu.bitcast(a2, I32)
    capf = float(cap)

    def bisect(i, lo):
        cand = lo | jnp.left_shift(jnp.int32(1), 30 - i)
        cnt = jnp.sum((bits2 >= cand).astype(F32), axis=1, keepdims=True)
        return jnp.where(cnt >= capf, cand, lo)

    thr_all = lax.fori_loop(0, 31, bisect, jnp.zeros((N_EXPERTS, 1), I32))
    need_all = capf - jnp.sum((bits2 > thr_all).astype(F32), axis=1, keepdims=True)

    T = aff3_ref.shape[2]
    tri_u = (_iota((LANES, LANES), 0) <= _iota((LANES, LANES), 1)).astype(BF16)
    tri_l = (_iota((LANES, LANES), 1) <= _iota((LANES, LANES), 0)).astype(BF16)
    ones8 = jnp.ones((8, LANES), BF16)
    before = _iota((T, T), 1) < _iota((T, T), 0)
    kcol = _iota((T, 1), 0).astype(F32)
    j = _iota((1, cap), 1).astype(F32)
    assert T < LANES
    tile_pad = jnp.zeros((LANES - T, LANES), F32)
    lane_pos = _iota((LANES, cap), 0).astype(F32)

    def tile_starts(maskb):
        tot_row = _dot_nt(ones8, maskb)[0:1, :]
        return jnp.sum(jnp.where(before, tot_row, 0.0), axis=1, keepdims=True)

    for e in range(N_EXPERTS):
        bits = pltpu.bitcast(aff3_ref[0, e], I32)
        thr = thr_all[e:e + 1, :]
        need = need_all[e:e + 1, :]
        gt = bits > thr
        eq = bits == thr
        eqb = eq.astype(BF16)
        eq_rank = _dot(eqb, tri_u) + tile_starts(eqb) - eq.astype(F32)
        sel = gt | (eq & (eq_rank < need))
        selb = sel.astype(BF16)
        cs = _dot(selb, tri_u)
        start = tile_starts(selb)
        end = start + cs[:, LANES - 1:LANES]
        onehot = (start <= j) & (j < end)
        ohf = onehot.astype(F32)
        tile_of = jnp.sum(ohf * kcol, axis=0, keepdims=True)
        j_loc = j - jnp.sum(ohf * start, axis=0, keepdims=True)
        cs_t = _dot_nt(tri_l, selb)
        ohb = onehot.astype(BF16)
        r_t = _dot(cs_t.astype(BF16), ohb)
        local = jnp.sum((r_t <= j_loc).astype(F32), axis=0, keepdims=True)
        idx_ref[0, e:e + 1, :] = ((tile_of * LANES + local).astype(I32) + pl.program_id(0) * seq) * row_pitch
        a_t = jnp.concatenate([aff3_ref[0, e], tile_pad], axis=0).T[:, :T]
        a_hi = a_t.astype(BF16)
        a_lo = (a_t - a_hi.astype(F32)).astype(BF16)
        a_tile = _dot(a_hi, ohb) + _dot(a_lo, ohb)
        gval_ref[0, e:e + 1, :] = jnp.sum(jnp.where(lane_pos == local, a_tile, 0.0), axis=0, keepdims=True)


def _topc(aff_t, cap, row_pitch):
    B, E, S = aff_t.shape
    T = S // LANES
    aff3 = aff_t.reshape(B, E, T, LANES)
    return pl.pallas_call(
        functools.partial(_topc_kernel, cap=cap, seq=S, row_pitch=row_pitch),
        grid=(B,),
        in_specs=[pl.BlockSpec((1, E, S), lambda b: (b, 0, 0)), pl.BlockSpec((1, E, T, LANES), lambda b: (b, 0, 0, 0))],
        out_specs=[pl.BlockSpec((1, E, cap), lambda b: (b, 0, 0))] * 2,
        out_shape=[jax.ShapeDtypeStruct((B, E, cap), I32), jax.ShapeDtypeStruct((B, E, cap), F32)],
        compiler_params=pltpu.CompilerParams(dimension_semantics=("arbitrary",), vmem_limit_bytes=VMEM_LIMIT),
        name="topc",
    )(aff_t, aff3)


FFN_ROWS = 256
FFN_COLS = 256
AB_RING = 3


SLAB_X, SLAB_ACC = 0, 1


def _ffn_kernel(idxp_ref, idxn_ref, gv_ref, slab_in_hbm, wg_ref, wu_ref, wd_ref, slab_hbm,
                buf_ref, sems, *, nb, cap, d_model, ff):
    del slab_in_hbm
    s = pl.program_id(0)
    last = pl.num_programs(0) - 1
    a_cur = s % AB_RING
    a_nxt = (s + 1) % AB_RING
    a_prv = (s + 2) % AB_RING
    chunks = d_model // LANES

    def rows_of(start):
        return pl.ds(pl.multiple_of(start, chunks), chunks)

    def gather(r, j, sl):
        return pltpu.make_async_copy(slab_hbm.at[:, rows_of(r)], buf_ref.at[sl, :, rows_of(j * chunks)], sems.at[0])

    def scatter(r, j, sl):
        return pltpu.make_async_copy(buf_ref.at[sl, SLAB_ACC, rows_of(j * chunks)], slab_hbm.at[SLAB_ACC, rows_of(r)],
                                     sems.at[1])

    def wait_gather(sl):
        pltpu.make_async_copy(slab_hbm.at[:, pl.ds(0, cap * chunks)], buf_ref.at[sl], sems.at[0]).wait()

    def wait_scatter(sl):
        pltpu.make_async_copy(buf_ref.at[sl, SLAB_ACC], slab_hbm.at[SLAB_ACC, pl.ds(0, cap * chunks)],
                              sems.at[1]).wait()

    @pl.when(s == 0)
    def _():
        @pl.loop(0, cap)
        def _(j):
            r = idxp_ref[0, 0, j]
            gather(r, j, 0).start()
            gather(r, j, AB_RING - 1).start()
        wait_gather(AB_RING - 1)

    wait_gather(a_cur)

    halves = FFN_COLS // LANES
    nblk = cap // FFN_ROWS
    n_ct = ff // FFN_COLS
    assert d_model // FFN_COLS == n_ct

    n_issues = nblk * n_ct
    gather_rows = -(-cap // n_issues)
    scatter_rows = -(-cap // n_issues)
    next_gather, next_scatter, n_issue = [0], [0], [0]

    def issue_rows():
        n_issue[0] += 1
        lo, hi = next_gather[0], min(next_gather[0] + gather_rows, cap)
        next_gather[0] = hi
        for j in range(lo, hi):
            gather(idxn_ref[0, 0, j], j, a_nxt).start(priority=0)
        lo, hi = next_scatter[0], min(next_scatter[0] + scatter_rows, cap)
        next_scatter[0] = hi
        for j in range(lo, hi):
            scatter(idxp_ref[0, 0, j], j, a_prv).start(priority=1)

    def chunk_rows(rb, c):
        return pl.ds(rb * FFN_ROWS * chunks + c, FFN_ROWS, stride=chunks)

    def gate_rows(rb):
        g_row = gv_ref[0, rb]
        return jnp.concatenate(
            [jnp.broadcast_to(g_row[:, i * LANES:(i + 1) * LANES], (LANES, LANES)).T for i in range(FFN_ROWS // LANES)],
            axis=0)

    hid_prev = None
    for stage in range(nblk + 1):
        up, dn = stage < nblk, stage >= 1
        if up:
            xb = jnp.concatenate([buf_ref[a_cur, SLAB_X, chunk_rows(stage, c), :] for c in range(chunks)],
                                 axis=1).astype(BF16)
        if dn:
            gval = gate_rows(stage - 1)
        hid = []
        for ct in range(n_ct):
            cs = slice(ct * FFN_COLS, (ct + 1) * FFN_COLS)
            if up:
                hg = _dot(xb, wg_ref[0, :, cs])
                hu = _dot(xb, wu_ref[0, :, cs])
                hid.append((hg * _sigmoid(hg) * hu).astype(BF16))
            if dn:
                ye = _dot(hid_prev, wd_ref[0, :, cs])
                for i in range(halves):
                    buf_ref[a_cur, SLAB_ACC, chunk_rows(stage - 1, ct * halves + i), :] += (
                        ye[:, i * LANES:(i + 1) * LANES] * gval)
            if up:
                issue_rows()
        hid_prev = jnp.concatenate(hid, axis=1) if up else None
    assert next_gather[0] == cap and next_scatter[0] == cap and n_issue[0] == n_issues

    wait_scatter(a_prv)

    @pl.when(s == last)
    def _():
        wait_gather(a_nxt)

        @pl.loop(0, cap)
        def _(j):
            scatter(idxn_ref[0, 0, j], j, a_cur).start()
        wait_scatter(a_cur)


def _expert_ffn(idx, gval, slab, wg, wu, wd):
    B, E, cap = idx.shape
    d_model, ff = wg.shape[1], wg.shape[2]
    chunks = d_model // LANES
    assert B >= 3 and cap % FFN_ROWS == 0 and ff % FFN_COLS == 0 and d_model % FFN_COLS == 0
    ns = E * B
    nblk = cap // FFN_ROWS
    idx3 = idx.reshape(B * E, 1, cap)
    gv4 = gval.reshape(B * E, nblk, 1, FFN_ROWS)
    any_spec = pl.BlockSpec(memory_space=pl.ANY)
    wspec = lambda r, c: pl.BlockSpec((1, r, c), lambda s: (s // B, 0, 0))
    blk = lambda s: (s % B) * E + s // B
    ispec = lambda f: pl.BlockSpec((1, 1, cap), lambda s: (blk(f(s)), 0, 0), memory_space=pltpu.SMEM)
    kern = functools.partial(_ffn_kernel, nb=B, cap=cap, d_model=d_model, ff=ff)
    return pl.pallas_call(
        kern,
        grid=(ns,),
        in_specs=[ispec(lambda s: jnp.maximum(s - 1, 0)), ispec(lambda s: jnp.minimum(s + 1, ns - 1)),
                  pl.BlockSpec((1, nblk, 1, FFN_ROWS), lambda s: (blk(s), 0, 0, 0)),
                  any_spec, wspec(d_model, ff), wspec(d_model, ff), wspec(ff, d_model)],
        out_specs=any_spec,
        out_shape=jax.ShapeDtypeStruct(slab.shape, F32),
        scratch_shapes=[pltpu.VMEM((AB_RING, 2, cap * chunks, LANES), F32), pltpu.SemaphoreType.DMA((2,))],
        input_output_aliases={3: 0},
        compiler_params=pltpu.CompilerParams(dimension_semantics=("arbitrary",), vmem_limit_bytes=VMEM_LIMIT),
        name="expert_ffn",
    )(idx3, idx3, gv4, slab, wg, wu, wd)


def _untile_kernel(a_ref, o_ref):
    tm, chunks = o_ref.shape[0], o_ref.shape[1] // LANES
    for c in range(chunks):
        o_ref[:, c * LANES:(c + 1) * LANES] = a_ref[0, pl.ds(c, tm, stride=chunks), :]


def _untile(slab, plane, chunks, tm):
    N = slab.shape[1] // chunks
    return pl.pallas_call(
        _untile_kernel,
        grid=(N // tm,),
        in_specs=[pl.BlockSpec((1, tm * chunks, LANES), lambda i: (plane, i, 0))],
        out_specs=pl.BlockSpec((tm, chunks * LANES), lambda i: (i, 0)),
        out_shape=jax.ShapeDtypeStruct((N, chunks * LANES), F32),
        compiler_params=pltpu.CompilerParams(dimension_semantics=("arbitrary",), vmem_limit_bytes=VMEM_LIMIT),
        name="untile",
    )(slab)


def _layer(x, mem, norm_mix_g, w_in, att_q_norm_g, att_k_norm_g, att_sink, ml_conv_w, ml_conv_b,
           ml_gate_b, ml_out_norm_g, w_out, norm_mem_g, mem_kv_norm_g, w_mem_q, w_mem_kv,
           mem_q_norm_g, mem_k_norm_g, w_mem_o, norm_ffn_g, w_router, b_router,
           w_exp_gate, w_exp_up, w_exp_down):
    B, S, D = x.shape
    row = lambda v: v.reshape(1, -1).astype(F32)
    tm_in = min(1024, S)
    tm_mix = min(2 * MIX_ROWS, S)
    cap = CAPACITY_FACTOR * S // N_EXPERTS

    mem_k, mem_v = _mem_kv(mem, row(mem_kv_norm_g), w_mem_kv.astype(BF16), row(mem_k_norm_g))

    order = jnp.array(ATT_HEAD_ORDER)
    w_aq = w_in[:, _AQ[0]:_AQ[1]].reshape(D, ATT_HEADS, ATT_DH)[:, order].reshape(D, -1)
    w_main = jnp.concatenate([w_aq, w_in[:, _AQ[1]:_GATES[0]]], axis=1).astype(BF16)
    wg_t = w_in[:, _GATES[0]:_GATES[1]].T.astype(BF16)
    q_g = jnp.tile(row(att_q_norm_g), (1, ATT_HEADS)) * (ATT_DH ** -0.5)
    k_g = jnp.tile(row(att_k_norm_g), (1, ATT_KV))
    aq, ak, av, mqk, mv, mo, g_rows, g_cols = _inproj(x, row(norm_mix_g), w_main, wg_t, ml_gate_b.reshape(-1, 1).astype(F32),
                                               q_g, k_g, ml_conv_w.astype(F32), row(ml_conv_b), tm_in)

    att = _attention(att_sink.astype(F32), aq, ak, av)
    hf, hb = _mlstm(mqk, mv, g_rows, g_cols)

    att_w = ATT_HEADS * ATT_DH
    w_out_att = w_out[:att_w].reshape(ATT_HEADS, ATT_DH, D)[order].reshape(att_w, D)
    w_out_p = jnp.concatenate([w_out_att, w_out[att_w:]], axis=0).astype(BF16)
    slab, aff_t = _mix(x, att, hf, hb, mo, row(ml_out_norm_g), w_out_p, row(norm_mem_g),
                             w_mem_q.astype(BF16), row(mem_q_norm_g), mem_k, mem_v, w_mem_o.astype(BF16),
                             row(norm_ffn_g), w_router.astype(F32), row(b_router), tm_mix)

    idx, gval = _topc(aff_t, cap, D // LANES)
    slab = _expert_ffn(idx, gval, slab, w_exp_gate.astype(BF16), w_exp_up.astype(BF16), w_exp_down.astype(BF16))
    return _untile(slab, SLAB_ACC, D // LANES, tm_in).reshape(B, S, D)


def kernel(x, mem, norm_mix_g, w_in, att_q_norm_g, att_k_norm_g, att_sink, ml_conv_w, ml_conv_b, ml_gate_b,
           ml_out_norm_g, w_out, norm_mem_g, mem_kv_norm_g, w_mem_q, w_mem_kv, mem_q_norm_g, mem_k_norm_g,
           w_mem_o, norm_ffn_g, w_router, b_router, w_exp_gate, w_exp_up, w_exp_down):
    params = (norm_mix_g, w_in, att_q_norm_g, att_k_norm_g, att_sink, ml_conv_w, ml_conv_b, ml_gate_b,
              ml_out_norm_g, w_out, norm_mem_g, mem_kv_norm_g, w_mem_q, w_mem_kv, mem_q_norm_g, mem_k_norm_g,
              w_mem_o, norm_ffn_g, w_router, b_router, w_exp_gate, w_exp_up, w_exp_down)
    depth = norm_mix_g.shape[0]
    for l in range(depth):
        x = _layer(x, mem, *[p[l] for p in params])
    return x
```

```python
import functools

import jax
import jax.numpy as jnp
from jax import lax
from jax.experimental import pallas as pl
from jax.experimental.pallas import tpu as pltpu

F32 = jnp.float32
BF16 = jnp.bfloat16
I32 = jnp.int32

EPS = 1e-6
LANES = 128
BLK = 128
ATT_HEADS, ATT_KV, ATT_DH = 8, 2, 64
ML_HEADS, ML_DH = 4, 128
MEM_HEADS, MEM_DH = 4, 128
N_EXPERTS = 16
CAPACITY_FACTOR = 2
NEG = -1e30
HALO_X = 8
VMEM_LIMIT = 48 * 1024 * 1024
FFN_VMEM_LIMIT = 57 * 1024 * 1024

_NT = (((1,), (1,)), ((), ()))


def _dot(a, b):
    return jnp.dot(a, b, preferred_element_type=F32)


def _dot_nt(a, b):
    return lax.dot_general(a, b, _NT, preferred_element_type=F32)


def _split3(x):
    hi = x.astype(BF16)
    r1 = x - hi.astype(F32)
    mid = r1.astype(BF16)
    lo = (r1 - mid.astype(F32)).astype(BF16)
    return hi, mid, lo


def _dot01_nt(m01, x):
    hi, mid, lo = _split3(x)
    return _dot_nt(m01, hi) + _dot_nt(m01, mid) + _dot_nt(m01, lo)


def _dot01(x, m01):
    hi, mid, lo = _split3(x)
    return _dot(hi, m01) + _dot(mid, m01) + _dot(lo, m01)


def _rms(x, g):
    ms = jnp.mean(x * x, axis=-1, keepdims=True)
    return x * lax.rsqrt(ms + EPS) * g


def _sigmoid(x):
    return 1.0 / (1.0 + jnp.exp(-x))


def _iota(shape, dim):
    return lax.broadcasted_iota(I32, shape, dim)


def _mem_kv_kernel(mem_ref, g_ref, w_ref, kg_ref, k_ref, v_ref):
    mn = _rms(mem_ref[0], g_ref[...]).astype(BF16)
    kv = _dot(mn, w_ref[...])
    width = MEM_HEADS * MEM_DH
    for h in range(MEM_HEADS):
        sl = slice(h * MEM_DH, (h + 1) * MEM_DH)
        k_ref[0, :, sl] = _rms(kv[:, sl], kg_ref[...]).astype(BF16)
    v_ref[0] = kv[:, width:].astype(BF16)


def _mem_kv(mem, g, w_kv, k_g):
    B, M, D = mem.shape
    width = MEM_HEADS * MEM_DH
    full = lambda *s: pl.BlockSpec(s, lambda b: (0,) * len(s))
    return pl.pallas_call(
        _mem_kv_kernel,
        grid=(B,),
        in_specs=[pl.BlockSpec((1, M, D), lambda b: (b, 0, 0)), full(1, D), full(D, 2 * width), full(1, MEM_DH)],
        out_specs=[pl.BlockSpec((1, M, width), lambda b: (b, 0, 0))] * 2,
        out_shape=[jax.ShapeDtypeStruct((B, M, width), BF16)] * 2,
        compiler_params=pltpu.CompilerParams(dimension_semantics=("arbitrary",), vmem_limit_bytes=VMEM_LIMIT),
        name="mem_kv",
    )(mem, g, w_kv, k_g)


_AQ = (0, 512)
_AK = (512, 640)
_AV = (640, 768)
_MQK = (768, 1792)
_MV = (1792, 2304)
_MO = (2304, 2816)
_GATES = (2816, 2832)


GC_F, GC_GMAX, GC_EEND, GC_MLOC = 0, 1, 2, 3


INPROJ_ROWS = 256


def _inproj_kernel(x_ref, xp_ref, xn_ref, g_ref, w_ref, wgt_ref, gb_ref, qg_ref, kg_ref, bdq_ref, bdk_ref, trif_ref,
                   trib_ref, blk_ref, cw_ref, cb_ref, aq_ref, ak_ref, av_ref, mqk_ref, mv_ref, mo_ref, grow_ref, gcol_ref):
    tm = x_ref.shape[1]
    R = INPROJ_ROWS
    H = ML_HEADS
    pos = _iota((H, R), 1) % BLK
    raws = []
    for r0 in range(0, tm, R):
        rows = slice(r0, r0 + R)
        h = _rms(x_ref[0, rows, :], g_ref[...]).astype(BF16)
        gt = _dot_nt(wgt_ref[...], h) + gb_ref[...]
        logsig = jnp.minimum(gt, 0.0) - jnp.log(1.0 + jnp.exp(-jnp.abs(gt)))

        def chunk_scan(li, lf, tri_ref, fwd):
            f = _dot01(lf, tri_ref[...])
            g = li - f
            gmax = g
            sh = 1
            while sh < BLK:
                if fwd:
                    gmax = jnp.where(pos >= sh, jnp.maximum(gmax, pltpu.roll(gmax, sh, axis=1)), gmax)
                else:
                    gmax = jnp.where(pos < BLK - sh, jnp.maximum(gmax, pltpu.roll(gmax, R - sh, axis=1)), gmax)
                sh *= 2
            w_end = _dot01(lf, blk_ref[...]) + g
            m_loc = jnp.concatenate(
                [jnp.broadcast_to(jnp.max(w_end[:, c0:c0 + BLK], axis=1, keepdims=True), (H, BLK))
                 for c0 in range(0, R, BLK)], axis=1)
            return f, g, gmax, jnp.exp(w_end - m_loc), m_loc

        f_f, g_f, gm_f, ee_f, ml_f = chunk_scan(gt[0:H], logsig[H:2 * H], trif_ref, True)
        f_b, g_b, gm_b, ee_b, ml_b = chunk_scan(gt[2 * H:3 * H], logsig[3 * H:4 * H], trib_ref, False)
        grow_ref[0, :, rows] = jnp.concatenate([g_f, g_b], axis=0)
        kinds = [None] * 4
        kinds[GC_F], kinds[GC_GMAX], kinds[GC_EEND], kinds[GC_MLOC] = (f_f, f_b), (gm_f, gm_b), (ee_f, ee_b), (ml_f, ml_b)
        cols = jnp.concatenate([q for pair in kinds for q in pair], axis=0)
        pad = jnp.zeros((LANES - cols.shape[0], BLK), F32)
        for c0 in range(0, R, BLK):
            gcol_ref[0, r0 + c0:r0 + c0 + BLK, :] = jnp.concatenate([cols[:, c0:c0 + BLK], pad], axis=0).T
        sec = lambda s: _dot(h, w_ref[:, s[0]:s[1]])
        aq = sec(_AQ)
        ssq = _dot((aq * aq).astype(BF16), bdq_ref[...])
        aq_ref[0, rows, :] = (aq * lax.rsqrt(ssq * (1.0 / ATT_DH) + EPS) * qg_ref[...]).astype(BF16)
        ak = sec(_AK)
        ssk = _dot((ak * ak).astype(BF16), bdk_ref[...])
        ak_ref[0, rows, :] = (ak * lax.rsqrt(ssk * (1.0 / ATT_DH) + EPS) * kg_ref[...]).astype(BF16)
        av_ref[0, rows, :] = sec(_AV).astype(BF16)
        raws.append(sec(_MQK))
        mv_ref[0, rows, :] = sec(_MV).astype(BF16)
        mo_ref[0, rows, :] = sec(_MO).astype(BF16)
    i, ni = pl.program_id(1), pl.num_programs(1)
    h_halo = jnp.concatenate([_rms(xp_ref[0], g_ref[...]), _rms(xn_ref[0], g_ref[...])], axis=0).astype(BF16)
    halo = _dot(h_halo, w_ref[:, _MQK[0]:_MQK[1]])
    before = jnp.where(i == 0, 0.0, halo[HALO_X - 1:HALO_X, :])
    after = jnp.where(i == ni - 1, 0.0, halo[HALO_X:HALO_X + 1, :])
    raw = jnp.concatenate(raws, axis=0)
    r = _iota((tm, 1), 0)
    x_prev = jnp.where(r == 0, before, pltpu.roll(raw, 1, axis=0))
    x_next = jnp.where(r == tm - 1, after, pltpu.roll(raw, tm - 1, axis=0))
    y = cw_ref[0:1, :] * x_prev + cw_ref[1:2, :] * raw + cw_ref[2:3, :] * x_next + cb_ref[...]
    k_scale = jnp.where(_iota((1, y.shape[1]), 1) < ML_HEADS * ML_DH, 1.0, ML_DH ** -0.5)
    mqk_ref[0] = (y * _sigmoid(y) * k_scale).astype(BF16)


def _inproj(x, g, w_main, wg_t, gate_b, q_g, k_g, conv_w, conv_b, tm):
    B, S, D = x.shape
    per = tm // HALO_X
    last = S // HALO_X - 1
    bdq = (jnp.arange(512)[:, None] // ATT_DH == jnp.arange(512)[None, :] // ATT_DH).astype(BF16)
    bdk = bdq[:128, :128]
    R = min(INPROJ_ROWS, tm)
    s_from, s_to = jnp.arange(R)[:, None], jnp.arange(R)[None, :]
    same = s_from // BLK == s_to // BLK
    tri_f = (same & (s_from <= s_to)).astype(BF16)
    tri_b = (same & (s_from >= s_to)).astype(BF16)
    full = lambda *s: pl.BlockSpec(s, lambda b, i: (0,) * len(s))
    tok = lambda w: pl.BlockSpec((1, tm, w), lambda b, i: (b, i, 0))
    widths = (512, 128, 128, 1024, 512, 512)
    return pl.pallas_call(
        _inproj_kernel,
        grid=(B, S // tm),
        in_specs=[tok(D),
                  pl.BlockSpec((1, HALO_X, D), lambda b, i: (b, jnp.maximum(i * per - 1, 0), 0)),
                  pl.BlockSpec((1, HALO_X, D), lambda b, i: (b, jnp.minimum((i + 1) * per, last), 0)),
                  full(1, D), full(D, w_main.shape[1]), full(16, D), full(16, 1),
                  full(1, 512), full(1, 128), full(512, 512), full(128, 128), full(R, R), full(R, R), full(R, R),
                  full(3, 2 * ML_HEADS * ML_DH), full(1, 2 * ML_HEADS * ML_DH)],
        out_specs=[tok(w) for w in widths] + [pl.BlockSpec((1, 2 * ML_HEADS, tm), lambda b, i: (b, 0, i)), tok(LANES)],
        out_shape=[jax.ShapeDtypeStruct((B, S, w), BF16) for w in widths]
        + [jax.ShapeDtypeStruct((B, 2 * ML_HEADS, S), F32), jax.ShapeDtypeStruct((B, S, LANES), F32)],
        compiler_params=pltpu.CompilerParams(dimension_semantics=("arbitrary", "arbitrary"),
                                             vmem_limit_bytes=VMEM_LIMIT),
        name="inproj",
    )(x, x, x, g, w_main, wg_t, gate_b, q_g, k_g, bdq, bdk, tri_f, tri_b, same.astype(BF16), conv_w, conv_b)


ATT_GROUP = ATT_HEADS // ATT_KV
ATT_HEAD_ORDER = tuple(kv * ATT_GROUP + g for g in range(ATT_GROUP) for kv in range(ATT_KV))


ATT_QB = 8


def _attn_kernel(sink_ref, q_ref, kp_ref, kc_ref, kn_ref, vp_ref, vc_ref, vn_ref, bias_ref, o_ref):
    n = pl.program_id(1)
    nb = pl.num_programs(1)
    G, J = ATT_GROUP, ATT_QB
    kband = jnp.concatenate([kp_ref[0], kc_ref[0], kn_ref[0]], axis=0)
    vband = jnp.concatenate([vp_ref[0], vc_ref[0], vn_ref[0]], axis=0)
    first = _iota(kband.shape, 1) < ATT_DH
    kband, vband = kband.astype(F32), vband.astype(F32)
    keep_lanes = lambda a, mine: jnp.where(mine, a, 0.0).astype(BF16)
    si = _iota((1, (J + 2) * BLK), 1)
    edge = jnp.where(((si < BLK) & (n == 0)) | ((si >= (J + 1) * BLK) & (n == nb - 1)), NEG, 0.0)
    lane_first = _iota((1, LANES), 1) < ATT_DH
    k_kv = [keep_lanes(kband, first), keep_lanes(kband, ~first)]
    v_kv = [jnp.concatenate([keep_lanes(vband, m), m.astype(F32).astype(BF16)], axis=1) for m in (first, ~first)]
    for j in range(J):
        rows = slice(j * BLK, (j + 1) * BLK)
        band = slice(j * BLK, (j + 3) * BLK)
        q = q_ref[0, rows, :]
        qs = jnp.concatenate([q[:, g * LANES:(g + 1) * LANES] for g in range(G)], axis=0)
        tot = None
        sink_terms = []
        for kv in range(ATT_KV):
            s = _dot_nt(qs, k_kv[kv][band])
            ps, st = [], []
            for g in range(G):
                h = kv * G + g
                sink = sink_ref[h]
                sg = s[g * BLK:(g + 1) * BLK] + bias_ref[h] + edge[:, band]
                m = jnp.maximum(jnp.max(sg, axis=-1, keepdims=True), sink)
                ps.append(jnp.exp(sg - m).astype(BF16))
                st.append(jnp.exp(sink - m))
            part = _dot(jnp.concatenate(ps, axis=0), v_kv[kv][band])
            tot = part if tot is None else tot + part
            sink_terms.append(jnp.concatenate(st, axis=0))
        den = tot[:, LANES:] + jnp.where(lane_first, sink_terms[0], sink_terms[1])
        out = (tot[:, :LANES] / den).astype(BF16)
        o_ref[0, rows, :] = jnp.concatenate([out[g * BLK:(g + 1) * BLK] for g in range(G)], axis=1)


def _attention(sink, aq, ak, av):
    B, S, _ = aq.shape
    J = ATT_QB
    nb = S // (J * BLK)
    last_blk = S // BLK - 1
    kvw = ATT_KV * ATT_DH
    assert kvw == LANES and S % (J * BLK) == 0
    dist = jnp.abs(jnp.arange(BLK)[:, None] + BLK - jnp.arange(3 * BLK)[None, :]).astype(F32)
    slopes = jnp.exp2(-8.0 * jnp.arange(1, ATT_HEADS + 1, dtype=F32) / ATT_HEADS)
    bias = jnp.where(dist <= BLK, -slopes[:, None, None] * dist, NEG)
    prev = pl.BlockSpec((1, BLK, kvw), lambda b, n, *_: (b, jnp.maximum(J * n - 1, 0), 0))
    cur = pl.BlockSpec((1, J * BLK, kvw), lambda b, n, *_: (b, n, 0))
    nxt = pl.BlockSpec((1, BLK, kvw), lambda b, n, *_: (b, jnp.minimum(J * n + J, last_blk), 0))
    qspec = pl.BlockSpec((1, J * BLK, ATT_HEADS * ATT_DH), lambda b, n, *_: (b, n, 0))
    bspec = pl.BlockSpec((ATT_HEADS, BLK, 3 * BLK), lambda b, n, *_: (0, 0, 0))
    return pl.pallas_call(
        _attn_kernel,
        grid_spec=pltpu.PrefetchScalarGridSpec(
            num_scalar_prefetch=1, grid=(B, nb),
            in_specs=[qspec, prev, cur, nxt, prev, cur, nxt, bspec], out_specs=qspec),
        out_shape=jax.ShapeDtypeStruct(aq.shape, BF16),
        compiler_params=pltpu.CompilerParams(dimension_semantics=("arbitrary", "arbitrary"),
                                             vmem_limit_bytes=VMEM_LIMIT),
        name="win_attn",
    )(sink, aq, ak, ak, ak, av, av, av, bias)


ML_CPS = 8


def _mlstm_kernel(xf_ref, xb_ref, vf_ref, vb_ref, grf_ref, grb_ref, gcf_ref, gcb_ref, hf_ref, hb_ref, cn_ref, m_ref):
    c = pl.program_id(1)
    L = BLK
    width = ML_HEADS * ML_DH

    @pl.when(c == 0)
    def _():
        cn_ref[...] = jnp.zeros_like(cn_ref)
        m_ref[...] = jnp.zeros_like(m_ref)

    tt = _iota((L, L), 0)
    ss = _iota((L, L), 1)
    ones_v = jnp.ones((L, ML_DH), BF16)

    def chunk(d, sub, qk_ref, v_ref, grow_ref, gcol_ref, out_ref):
        H = ML_HEADS
        rows = slice(sub * L, (sub + 1) * L)
        keep = (ss <= tt) if d == 0 else (ss >= tt)
        g_row = grow_ref[0, H * d:H * (d + 1), rows]
        gc = gcol_ref[0, rows, :]
        kind = lambda i: gc[:, 2 * H * i + H * d:2 * H * i + H * (d + 1)]
        f, g_max, e_end, m_loc = kind(GC_F), kind(GC_GMAX), kind(GC_EEND), kind(GC_MLOC)[0:1, :]
        end = L - 1 if d == 0 else 0
        f_end = f[end:end + 1, :]
        m0 = m_ref[d, 0:1, 0:H]
        mm = jnp.maximum(m0, g_max)
        e_inter = jnp.exp(m0 - mm)
        floor = jnp.exp(-(f + mm))
        m_new = jnp.maximum(f_end + m0, m_loc)
        ca = jnp.exp(f_end + m0 - m_new)
        cb = jnp.exp(m_loc - m_new)
        m_ref[d, 0:1, 0:H] = m_new
        col = lambda a, h, w=ML_DH: jnp.broadcast_to(a[:, h:h + 1], (L, w))
        for h in range(H):
            u = H * d + h
            hs = slice(h * ML_DH, (h + 1) * ML_DH)
            qb = qk_ref[0, rows, hs]
            kb = qk_ref[0, rows, width + h * ML_DH: width + (h + 1) * ML_DH]
            v1 = jnp.concatenate([v_ref[0, rows, hs], ones_v], axis=1)
            cn0 = cn_ref[u]
            dec = jnp.where(keep, jnp.exp(g_row[h:h + 1, :] - col(mm, h)), 0.0)
            s_qk = (_dot_nt(qb, kb) * dec).astype(BF16)
            tot = col(e_inter, h, 2 * ML_DH) * _dot(qb, cn0.astype(BF16)) + _dot(s_qk, v1)
            out_ref[0, rows, hs] = tot[:, :ML_DH] / jnp.maximum(jnp.abs(tot[:, ML_DH:]), col(floor, h))
            ks = kb.astype(F32) * col(e_end, h)
            cn_ref[u] = ca[:, h:h + 1] * cn0 + cb[:, h:h + 1] * _dot(ks.T.astype(BF16), v1)

    for i in range(ML_CPS):
        chunk(0, i, xf_ref, vf_ref, grf_ref, gcf_ref, hf_ref)
        chunk(1, ML_CPS - 1 - i, xb_ref, vb_ref, grb_ref, gcb_ref, hb_ref)


def _mlstm(mqk, mv, g_rows, g_cols):
    B, S, _ = mqk.shape
    rows = ML_CPS * BLK
    ns = S // rows
    assert S % rows == 0
    width = ML_HEADS * ML_DH
    fwd = lambda c: c
    bwd = lambda c: ns - 1 - c
    qkspec = lambda ci: pl.BlockSpec((1, rows, 2 * width), lambda b, c: (b, ci(c), 0))
    vspec = lambda ci: pl.BlockSpec((1, rows, width), lambda b, c: (b, ci(c), 0))
    gspec = lambda ci: pl.BlockSpec((1, 2 * ML_HEADS, rows), lambda b, c: (b, 0, ci(c)))
    cspec = lambda ci: pl.BlockSpec((1, rows, LANES), lambda b, c: (b, ci(c), 0))
    units = 2 * ML_HEADS
    return pl.pallas_call(
        _mlstm_kernel,
        grid=(B, ns),
        in_specs=[qkspec(fwd), qkspec(bwd), vspec(fwd), vspec(bwd), gspec(fwd), gspec(bwd), cspec(fwd), cspec(bwd)],
        out_specs=[vspec(fwd), vspec(bwd)],
        out_shape=[jax.ShapeDtypeStruct((B, S, width), F32)] * 2,
        scratch_shapes=[pltpu.VMEM((units, ML_DH, 2 * ML_DH), F32), pltpu.VMEM((2, 8, LANES), F32)],
        compiler_params=pltpu.CompilerParams(dimension_semantics=("arbitrary", "arbitrary"),
                                             vmem_limit_bytes=VMEM_LIMIT),
        name="mlstm",
    )(mqk, mqk, mv, mv, g_rows, g_rows, g_cols, g_cols)


MIX_ROWS = 256


def _mix_kernel(x_ref, att_ref, hf_ref, hb_ref, mo_ref, og_ref, wo_ref,
                gm_ref, wq_ref, mqg_ref, k_ref, v_ref, wmo_ref,
                gf_ref, wr_ref, brt_ref,
                slab_ref, afft_ref):
    width = ML_HEADS * ML_DH
    k = k_ref[0]
    v = v_ref[0]
    wr = wr_ref[...]
    w_hi = wr.astype(BF16)
    w_lo = (wr - w_hi.astype(F32)).astype(BF16)
    for r0 in range(0, x_ref.shape[1], MIX_ROWS):
        rows = slice(r0, r0 + MIX_ROWS)
        x = x_ref[0, rows, :]
        ml = hf_ref[0, rows, :] + hb_ref[0, rows, :]
        mo = mo_ref[0, rows, :].astype(F32)
        parts = []
        for h in range(ML_HEADS):
            sl = slice(h * ML_DH, (h + 1) * ML_DH)
            parts.append((_sigmoid(mo[:, sl]) * _rms(ml[:, sl], og_ref[:, sl])).astype(BF16))
        ml_out = jnp.concatenate(parts, axis=1)
        y1 = x + _dot(att_ref[0, rows, :], wo_ref[0:width, :]) + _dot(ml_out, wo_ref[width:2 * width, :])
        h2 = _rms(y1, gm_ref[...]).astype(BF16)
        qm = _dot(h2, wq_ref[...])
        outs = []
        for h in range(MEM_HEADS):
            sl = slice(h * MEM_DH, (h + 1) * MEM_DH)
            qh = (_rms(qm[:, sl], mqg_ref[...]) * (MEM_DH ** -0.5)).astype(BF16)
            s = _dot_nt(qh, k[:, sl])
            p = jnp.exp(s - jnp.max(s, axis=-1, keepdims=True))
            o = _dot(p.astype(BF16), v[:, sl]) / jnp.sum(p, axis=-1, keepdims=True)
            outs.append(o.astype(BF16))
        y2 = y1 + _dot(jnp.concatenate(outs, axis=1), wmo_ref[...])
        h3 = _rms(y2, gf_ref[...])
        h_hi = h3.astype(BF16)
        h_lo = (h3 - h_hi.astype(F32)).astype(BF16)
        logits = _dot(h_hi, w_hi) + _dot(h_lo, w_hi) + _dot(h_hi, w_lo)
        logits_t = jnp.concatenate([logits[q0:q0 + LANES, :].T[:N_EXPERTS, :] for q0 in range(0, MIX_ROWS, LANES)],
                                   axis=1) + brt_ref[...]
        pt = jnp.exp(logits_t - jnp.max(logits_t, axis=0, keepdims=True))
        afft_ref[0, :, rows] = pt / jnp.sum(pt, axis=0, keepdims=True)
        chunks = y2.shape[1] // LANES
        for c in range(chunks):
            tiles = pl.ds(r0 * chunks + c, MIX_ROWS, stride=chunks)
            slab_ref[SLAB_ACC, tiles, :] = y2[:, c * LANES:(c + 1) * LANES]
            slab_ref[SLAB_X, tiles, :] = h3[:, c * LANES:(c + 1) * LANES]


def _mix(x, att, hf, hb, mo, out_g, w_out, g_mem, w_q, mq_g, mem_k, mem_v, w_mo, g_ffn, w_r, b_r, tm):
    B, S, D = x.shape
    M = mem_k.shape[1]
    width = ML_HEADS * ML_DH
    mw = MEM_HEADS * MEM_DH
    full = lambda *s: pl.BlockSpec(s, lambda b, i: (0,) * len(s))
    tok = lambda w: pl.BlockSpec((1, tm, w), lambda b, i: (b, i, 0))
    chunks = D // LANES
    tiled = pl.BlockSpec((2, tm * chunks, LANES), lambda b, i: (0, b * (S // tm) + i, 0))
    memspec = pl.BlockSpec((1, M, mw), lambda b, i: (b, 0, 0))
    tiled_shape = jax.ShapeDtypeStruct((2, B * S * chunks, LANES), F32)
    return pl.pallas_call(
        _mix_kernel,
        grid=(B, S // tm),
        in_specs=[tok(D), tok(width), tok(width), tok(width), tok(width), full(1, width), full(2 * width, D),
                  full(1, D), full(D, mw), full(1, MEM_DH), memspec, memspec, full(mw, D),
                  full(1, D), full(D, LANES), full(N_EXPERTS, 1)],
        out_specs=[tiled, pl.BlockSpec((1, N_EXPERTS, tm), lambda b, i: (b, 0, i))],
        out_shape=[tiled_shape, jax.ShapeDtypeStruct((B, N_EXPERTS, S), F32)],
        compiler_params=pltpu.CompilerParams(dimension_semantics=("arbitrary", "arbitrary"),
                                             vmem_limit_bytes=VMEM_LIMIT),
        name="mix_mem_router",
    )(x, att, hf, hb, mo, out_g, w_out, g_mem, w_q, mq_g, mem_k, mem_v, w_mo, g_ffn,
      jnp.pad(w_r, ((0, 0), (0, LANES - N_EXPERTS))), b_r.T)


def _topc_kernel(aff2_ref, aff3_ref, idx_ref, gval_ref, *, cap, seq, row_pitch):
    a2 = aff2_ref[0]
    bits2 = pltpu.bitcast(a2, I32)
    capf = float(cap)

    def bisect(i, lo):
        cand = lo | jnp.left_shift(jnp.int32(1), 30 - i)
        cnt = jnp.sum((bits2 >= cand).astype(F32), axis=1, keepdims=True)
        return jnp.where(cnt >= capf, cand, lo)

    thr_all = lax.fori_loop(0, 31, bisect, jnp.zeros((N_EXPERTS, 1), I32))
    need_all = capf - jnp.sum((bits2 > thr_all).astype(F32), axis=1, keepdims=True)

    T = aff3_ref.shape[2]
    tri_u = (_iota((LANES, LANES), 0) <= _iota((LANES, LANES), 1)).astype(BF16)
    tri_l = (_iota((LANES, LANES), 1) <= _iota((LANES, LANES), 0)).astype(BF16)
    ones8 = jnp.ones((8, LANES), BF16)
    before = _iota((T, T), 1) < _iota((T, T), 0)
    kcol = _iota((T, 1), 0).astype(F32)
    j = _iota((1, cap), 1).astype(F32)
    assert T < LANES
    tile_pad = jnp.zeros((LANES - T, LANES), F32)
    lane_pos = _iota((LANES, cap), 0).astype(F32)

    def tile_starts(maskb):
        tot_row = _dot_nt(ones8, maskb)[0:1, :]
        return jnp.sum(jnp.where(before, tot_row, 0.0), axis=1, keepdims=True)

    for e in range(N_EXPERTS):
        bits = pltpu.bitcast(aff3_ref[0, e], I32)
        thr = thr_all[e:e + 1, :]
        need = need_all[e:e + 1, :]
        gt = bits > thr
        eq = bits == thr
        eqb = eq.astype(BF16)
        eq_rank = _dot(eqb, tri_u) + tile_starts(eqb) - eq.astype(F32)
        sel = gt | (eq & (eq_rank < need))
        selb = sel.astype(BF16)
        cs = _dot(selb, tri_u)
        start = tile_starts(selb)
        end = start + cs[:, LANES - 1:LANES]
        onehot = (start <= j) & (j < end)
        ohf = onehot.astype(F32)
        tile_of = jnp.sum(ohf * kcol, axis=0, keepdims=True)
        j_loc = j - jnp.sum(ohf * start, axis=0, keepdims=True)
        cs_t = _dot_nt(tri_l, selb)
        ohb = onehot.astype(BF16)
        r_t = _dot(cs_t.astype(BF16), ohb)
        local = jnp.sum((r_t <= j_loc).astype(F32), axis=0, keepdims=True)
        idx_ref[0, e:e + 1, :] = ((tile_of * LANES + local).astype(I32) + pl.program_id(0) * seq) * row_pitch
        a_t = jnp.concatenate([aff3_ref[0, e], tile_pad], axis=0).T[:, :T]
        a_hi = a_t.astype(BF16)
        a_lo = (a_t - a_hi.astype(F32)).astype(BF16)
        a_tile = _dot(a_hi, ohb) + _dot(a_lo, ohb)
        gval_ref[0, e:e + 1, :] = jnp.sum(jnp.where(lane_pos == local, a_tile, 0.0), axis=0, keepdims=True)


def _topc(aff_t, cap, row_pitch):
    B, E, S = aff_t.shape
    T = S // LANES
    aff3 = aff_t.reshape(B, E, T, LANES)
    return pl.pallas_call(
        functools.partial(_topc_kernel, cap=cap, seq=S, row_pitch=row_pitch),
        grid=(B,),
        in_specs=[pl.BlockSpec((1, E, S), lambda b: (b, 0, 0)), pl.BlockSpec((1, E, T, LANES), lambda b: (b, 0, 0, 0))],
        out_specs=[pl.BlockSpec((1, E, cap), lambda b: (b, 0, 0))] * 2,
        out_shape=[jax.ShapeDtypeStruct((B, E, cap), I32), jax.ShapeDtypeStruct((B, E, cap), F32)],
        compiler_params=pltpu.CompilerParams(dimension_semantics=("arbitrary",), vmem_limit_bytes=VMEM_LIMIT),
        name="topc",
    )(aff_t, aff3)


FFN_ROWS = 256
FFN_COLS = 256
AB_RING = 3


SLAB_X, SLAB_ACC = 0, 1


def _ffn_kernel(idxp_ref, idxn_ref, gv_ref, slab_in_hbm, wg32_ref, wu32_ref, wd32_ref, slab_hbm,
                buf_ref, wg_ref, wu_ref, wd_ref, sems, *, nb, cap, d_model, ff):
    del slab_in_hbm
    s = pl.program_id(0)
    last = pl.num_programs(0) - 1

    @pl.when(s % nb == 0)
    def _():
        for src, dst in ((wg32_ref, wg_ref), (wu32_ref, wu_ref), (wd32_ref, wd_ref)):
            for r0 in range(0, src.shape[1], FFN_ROWS):
                dst[0, r0:r0 + FFN_ROWS, :] = src[0, r0:r0 + FFN_ROWS, :].astype(BF16)

    a_cur = s % AB_RING
    a_nxt = (s + 1) % AB_RING
    a_prv = (s + 2) % AB_RING
    chunks = d_model // LANES

    def rows_of(start):
        return pl.ds(pl.multiple_of(start, chunks), chunks)

    def gather(r, j, sl):
        return pltpu.make_async_copy(slab_hbm.at[:, rows_of(r)], buf_ref.at[sl, :, rows_of(j * chunks)], sems.at[0])

    def scatter(r, j, sl):
        return pltpu.make_async_copy(buf_ref.at[sl, SLAB_ACC, rows_of(j * chunks)], slab_hbm.at[SLAB_ACC, rows_of(r)],
                                     sems.at[1])

    def wait_gather(sl):
        pltpu.make_async_copy(slab_hbm.at[:, pl.ds(0, cap * chunks)], buf_ref.at[sl], sems.at[0]).wait()

    def wait_scatter(sl):
        pltpu.make_async_copy(buf_ref.at[sl, SLAB_ACC], slab_hbm.at[SLAB_ACC, pl.ds(0, cap * chunks)],
                              sems.at[1]).wait()

    @pl.when(s == 0)
    def _():
        @pl.loop(0, cap)
        def _(j):
            r = idxp_ref[0, 0, j]
            gather(r, j, 0).start()
            gather(r, j, AB_RING - 1).start()
        wait_gather(AB_RING - 1)

    wait_gather(a_cur)

    halves = FFN_COLS // LANES
    nblk = cap // FFN_ROWS
    n_ct = ff // FFN_COLS
    assert d_model // FFN_COLS == n_ct

    n_issues = 2 * nblk * n_ct
    gather_rows = -(-cap // n_issues)
    scatter_rows = -(-cap // n_issues)
    next_gather, next_scatter, n_issue = [0], [0], [0]

    def issue_rows():
        n_issue[0] += 1
        lo, hi = next_gather[0], min(next_gather[0] + gather_rows, cap)
        next_gather[0] = hi
        for j in range(lo, hi):
            gather(idxn_ref[0, 0, j], j, a_nxt).start(priority=0)
        lo, hi = next_scatter[0], min(next_scatter[0] + scatter_rows, cap)
        next_scatter[0] = hi
        for j in range(lo, hi):
            scatter(idxp_ref[0, 0, j], j, a_prv).start(priority=1)

    def chunk_rows(rb, c):
        return pl.ds(rb * FFN_ROWS * chunks + c, FFN_ROWS, stride=chunks)

    def gate_rows(rb):
        g_row = gv_ref[0, rb]
        return jnp.concatenate(
            [jnp.broadcast_to(g_row[:, i * LANES:(i + 1) * LANES], (LANES, LANES)).T for i in range(FFN_ROWS // LANES)],
            axis=0)

    hid_prev = None
    for stage in range(nblk + 1):
        up, dn = stage < nblk, stage >= 1
        if up:
            xb = jnp.concatenate([buf_ref[a_cur, SLAB_X, chunk_rows(stage, c), :] for c in range(chunks)],
                                 axis=1).astype(BF16)
        if dn:
            gval = gate_rows(stage - 1)
        hid = []
        for ct in range(n_ct):
            cs = slice(ct * FFN_COLS, (ct + 1) * FFN_COLS)
            if up:
                hg = _dot(xb, wg_ref[0, :, cs])
                issue_rows()
                hu = _dot(xb, wu_ref[0, :, cs])
                hid.append((hg * _sigmoid(hg) * hu).astype(BF16))
            if dn:
                ye = _dot(hid_prev, wd_ref[0, :, cs])
                for i in range(halves):
                    buf_ref[a_cur, SLAB_ACC, chunk_rows(stage - 1, ct * halves + i), :] += (
                        ye[:, i * LANES:(i + 1) * LANES] * gval)
            if up:
                issue_rows()
        hid_prev = jnp.concatenate(hid, axis=1) if up else None
    assert next_gather[0] == cap and next_scatter[0] == cap and n_issue[0] == n_issues

    wait_scatter(a_prv)

    @pl.when(s == last)
    def _():
        wait_gather(a_nxt)

        @pl.loop(0, cap)
        def _(j):
            scatter(idxn_ref[0, 0, j], j, a_cur).start()
        wait_scatter(a_cur)


def _expert_ffn(idx, gval, slab, wg, wu, wd):
    B, E, cap = idx.shape
    d_model, ff = wg.shape[1], wg.shape[2]
    chunks = d_model // LANES
    assert B >= 3 and cap % FFN_ROWS == 0 and ff % FFN_COLS == 0 and d_model % FFN_COLS == 0
    ns = E * B
    nblk = cap // FFN_ROWS
    idx3 = idx.reshape(B * E, 1, cap)
    gv4 = gval.reshape(B * E, nblk, 1, FFN_ROWS)
    any_spec = pl.BlockSpec(memory_space=pl.ANY)
    wspec = lambda r, c: pl.BlockSpec((1, r, c), lambda s: (s // B, 0, 0))
    blk = lambda s: (s % B) * E + s // B
    ispec = lambda f: pl.BlockSpec((1, 1, cap), lambda s: (blk(f(s)), 0, 0), memory_space=pltpu.SMEM)
    kern = functools.partial(_ffn_kernel, nb=B, cap=cap, d_model=d_model, ff=ff)
    return pl.pallas_call(
        kern,
        grid=(ns,),
        in_specs=[ispec(lambda s: jnp.maximum(s - 1, 0)), ispec(lambda s: jnp.minimum(s + 1, ns - 1)),
                  pl.BlockSpec((1, nblk, 1, FFN_ROWS), lambda s: (blk(s), 0, 0, 0)),
                  any_spec, wspec(d_model, ff), wspec(d_model, ff), wspec(ff, d_model)],
        out_specs=any_spec,
        out_shape=jax.ShapeDtypeStruct(slab.shape, F32),
        scratch_shapes=[pltpu.VMEM((AB_RING, 2, cap * chunks, LANES), F32),
                        pltpu.VMEM((1, d_model, ff), BF16), pltpu.VMEM((1, d_model, ff), BF16),
                        pltpu.VMEM((1, ff, d_model), BF16), pltpu.SemaphoreType.DMA((2,))],
        input_output_aliases={3: 0},
        compiler_params=pltpu.CompilerParams(dimension_semantics=("arbitrary",), vmem_limit_bytes=FFN_VMEM_LIMIT),
        name="expert_ffn",
    )(idx3, idx3, gv4, slab, wg, wu, wd)


def _untile_kernel(a_ref, o_ref):
    tm, chunks = o_ref.shape[0], o_ref.shape[1] // LANES
    for c in range(chunks):
        o_ref[:, c * LANES:(c + 1) * LANES] = a_ref[0, pl.ds(c, tm, stride=chunks), :]


def _untile(slab, plane, chunks, tm):
    N = slab.shape[1] // chunks
    return pl.pallas_call(
        _untile_kernel,
        grid=(N // tm,),
        in_specs=[pl.BlockSpec((1, tm * chunks, LANES), lambda i: (plane, i, 0))],
        out_specs=pl.BlockSpec((tm, chunks * LANES), lambda i: (i, 0)),
        out_shape=jax.ShapeDtypeStruct((N, chunks * LANES), F32),
        compiler_params=pltpu.CompilerParams(dimension_semantics=("arbitrary",), vmem_limit_bytes=VMEM_LIMIT),
        name="untile",
    )(slab)


def _layer(x, mem, norm_mix_g, w_in, att_q_norm_g, att_k_norm_g, att_sink, ml_conv_w, ml_conv_b,
           ml_gate_b, ml_out_norm_g, w_out, norm_mem_g, mem_kv_norm_g, w_mem_q, w_mem_kv,
           mem_q_norm_g, mem_k_norm_g, w_mem_o, norm_ffn_g, w_router, b_router,
           w_exp_gate, w_exp_up, w_exp_down):
    B, S, D = x.shape
    row = lambda v: v.reshape(1, -1).astype(F32)
    tm_in = min(1024, S)
    tm_mix = min(2 * MIX_ROWS, S)
    cap = CAPACITY_FACTOR * S // N_EXPERTS

    mem_k, mem_v = _mem_kv(mem, row(mem_kv_norm_g), w_mem_kv.astype(BF16), row(mem_k_norm_g))

    order = jnp.array(ATT_HEAD_ORDER)
    w_aq = w_in[:, _AQ[0]:_AQ[1]].reshape(D, ATT_HEADS, ATT_DH)[:, order].reshape(D, -1)
    w_main = jnp.concatenate([w_aq, w_in[:, _AQ[1]:_GATES[0]]], axis=1).astype(BF16)
    wg_t = w_in[:, _GATES[0]:_GATES[1]].T.astype(BF16)
    q_g = jnp.tile(row(att_q_norm_g), (1, ATT_HEADS)) * (ATT_DH ** -0.5)
    k_g = jnp.tile(row(att_k_norm_g), (1, ATT_KV))
    aq, ak, av, mqk, mv, mo, g_rows, g_cols = _inproj(x, row(norm_mix_g), w_main, wg_t, ml_gate_b.reshape(-1, 1).astype(F32),
                                               q_g, k_g, ml_conv_w.astype(F32), row(ml_conv_b), tm_in)

    att = _attention(att_sink.astype(F32), aq, ak, av)
    hf, hb = _mlstm(mqk, mv, g_rows, g_cols)

    att_w = ATT_HEADS * ATT_DH
    w_out_att = w_out[:att_w].reshape(ATT_HEADS, ATT_DH, D)[order].reshape(att_w, D)
    w_out_p = jnp.concatenate([w_out_att, w_out[att_w:]], axis=0).astype(BF16)
    slab, aff_t = _mix(x, att, hf, hb, mo, row(ml_out_norm_g), w_out_p, row(norm_mem_g),
                             w_mem_q.astype(BF16), row(mem_q_norm_g), mem_k, mem_v, w_mem_o.astype(BF16),
                             row(norm_ffn_g), w_router.astype(F32), row(b_router), tm_mix)

    idx, gval = _topc(aff_t, cap, D // LANES)
    slab = _expert_ffn(idx, gval, slab, w_exp_gate.astype(F32), w_exp_up.astype(F32), w_exp_down.astype(F32))
    return _untile(slab, SLAB_ACC, D // LANES, tm_in).reshape(B, S, D)


def kernel(x, mem, norm_mix_g, w_in, att_q_norm_g, att_k_norm_g, att_sink, ml_conv_w, ml_conv_b, ml_gate_b,
           ml_out_norm_g, w_out, norm_mem_g, mem_kv_norm_g, w_mem_q, w_mem_kv, mem_q_norm_g, mem_k_norm_g,
           w_mem_o, norm_ffn_g, w_router, b_router, w_exp_gate, w_exp_up, w_exp_down):
    params = (norm_mix_g, w_in, att_q_norm_g, att_k_norm_g, att_sink, ml_conv_w, ml_conv_b, ml_gate_b,
              ml_out_norm_g, w_out, norm_mem_g, mem_kv_norm_g, w_mem_q, w_mem_kv, mem_q_norm_g, mem_k_norm_g,
              w_mem_o, norm_ffn_g, w_router, b_router, w_exp_gate, w_exp_up, w_exp_down)
    depth = norm_mix_g.shape[0]
    for l in range(depth):
        x = _layer(x, mem, *[p[l] for p in params])
    return x
```

```python
import functools

import jax
import jax.numpy as jnp
from jax import lax
from jax.experimental import pallas as pl
from jax.experimental.pallas import tpu as pltpu

F32 = jnp.float32
BF16 = jnp.bfloat16
I32 = jnp.int32

EPS = 1e-6
LANES = 128
BLK = 128
ATT_HEADS, ATT_KV, ATT_DH = 8, 2, 64
ML_HEADS, ML_DH = 4, 128
MEM_HEADS, MEM_DH = 4, 128
N_EXPERTS = 16
CAPACITY_FACTOR = 2
NEG = -1e30
HALO_X = 8
VMEM_LIMIT = 48 * 1024 * 1024
FFN_VMEM_LIMIT = 57 * 1024 * 1024

_NT = (((1,), (1,)), ((), ()))


def _dot(a, b):
    return jnp.dot(a, b, preferred_element_type=F32)


def _dot_nt(a, b):
    return lax.dot_general(a, b, _NT, preferred_element_type=F32)


def _split3(x):
    hi = x.astype(BF16)
    r1 = x - hi.astype(F32)
    mid = r1.astype(BF16)
    lo = (r1 - mid.astype(F32)).astype(BF16)
    return hi, mid, lo


def _dot01_nt(m01, x):
    hi, mid, lo = _split3(x)
    return _dot_nt(m01, hi) + _dot_nt(m01, mid) + _dot_nt(m01, lo)


def _dot01(x, m01):
    hi, mid, lo = _split3(x)
    return _dot(hi, m01) + _dot(mid, m01) + _dot(lo, m01)


def _rms(x, g):
    ms = jnp.mean(x * x, axis=-1, keepdims=True)
    return x * lax.rsqrt(ms + EPS) * g


def _sigmoid(x):
    return 1.0 / (1.0 + jnp.exp(-x))


def _iota(shape, dim):
    return lax.broadcasted_iota(I32, shape, dim)


def _interleave(chains):
    chains = list(chains)
    while chains:
        for ch in list(chains):
            try:
                next(ch)
            except StopIteration:
                chains.remove(ch)


def _mem_kv_kernel(mem_ref, g_ref, w_ref, kg_ref, k_ref, v_ref):
    mn = _rms(mem_ref[0], g_ref[...]).astype(BF16)
    kv = _dot(mn, w_ref[...])
    width = MEM_HEADS * MEM_DH
    for h in range(MEM_HEADS):
        sl = slice(h * MEM_DH, (h + 1) * MEM_DH)
        k_ref[0, :, sl] = _rms(kv[:, sl], kg_ref[...]).astype(BF16)
    v_ref[0] = kv[:, width:].astype(BF16)


def _mem_kv(mem, g, w_kv, k_g):
    B, M, D = mem.shape
    width = MEM_HEADS * MEM_DH
    full = lambda *s: pl.BlockSpec(s, lambda b: (0,) * len(s))
    return pl.pallas_call(
        _mem_kv_kernel,
        grid=(B,),
        in_specs=[pl.BlockSpec((1, M, D), lambda b: (b, 0, 0)), full(1, D), full(D, 2 * width), full(1, MEM_DH)],
        out_specs=[pl.BlockSpec((1, M, width), lambda b: (b, 0, 0))] * 2,
        out_shape=[jax.ShapeDtypeStruct((B, M, width), BF16)] * 2,
        compiler_params=pltpu.CompilerParams(dimension_semantics=("arbitrary",), vmem_limit_bytes=VMEM_LIMIT),
        name="mem_kv",
    )(mem, g, w_kv, k_g)


_AQ = (0, 512)
_AK = (512, 640)
_AV = (640, 768)
_MQK = (768, 1792)
_MV = (1792, 2304)
_MO = (2304, 2816)
_GATES = (2816, 2832)


GC_F, GC_GMAX, GC_EEND, GC_MLOC = 0, 1, 2, 3


INPROJ_ROWS = 256


def _inproj_kernel(x_ref, xp_ref, xn_ref, g_ref, w_ref, wgt_ref, gb_ref, qg_ref, kg_ref, bdq_ref, bdk_ref, trif_ref,
                   trib_ref, blk_ref, cw_ref, cb_ref, aq_ref, ak_ref, av_ref, mqk_ref, mv_ref, mo_ref, grow_ref, gcol_ref):
    tm = x_ref.shape[1]
    R = INPROJ_ROWS
    H = ML_HEADS
    pos = _iota((H, R), 1) % BLK
    raws = {}

    def chain(r0):
        rows = slice(r0, r0 + R)
        h = _rms(x_ref[0, rows, :], g_ref[...]).astype(BF16)
        gt = _dot_nt(wgt_ref[...], h) + gb_ref[...]
        logsig = jnp.minimum(gt, 0.0) - jnp.log(1.0 + jnp.exp(-jnp.abs(gt)))
        yield

        def chunk_scan(li, lf, tri_ref, fwd):
            f = _dot01(lf, tri_ref[...])
            g = li - f
            gmax = g
            sh = 1
            while sh < BLK:
                if fwd:
                    gmax = jnp.where(pos >= sh, jnp.maximum(gmax, pltpu.roll(gmax, sh, axis=1)), gmax)
                else:
                    gmax = jnp.where(pos < BLK - sh, jnp.maximum(gmax, pltpu.roll(gmax, R - sh, axis=1)), gmax)
                sh *= 2
            w_end = _dot01(lf, blk_ref[...]) + g
            m_loc = jnp.concatenate(
                [jnp.broadcast_to(jnp.max(w_end[:, c0:c0 + BLK], axis=1, keepdims=True), (H, BLK))
                 for c0 in range(0, R, BLK)], axis=1)
            return f, g, gmax, jnp.exp(w_end - m_loc), m_loc

        f_f, g_f, gm_f, ee_f, ml_f = chunk_scan(gt[0:H], logsig[H:2 * H], trif_ref, True)
        yield
        f_b, g_b, gm_b, ee_b, ml_b = chunk_scan(gt[2 * H:3 * H], logsig[3 * H:4 * H], trib_ref, False)
        yield
        grow_ref[0, :, rows] = jnp.concatenate([g_f, g_b], axis=0)
        kinds = [None] * 4
        kinds[GC_F], kinds[GC_GMAX], kinds[GC_EEND], kinds[GC_MLOC] = (f_f, f_b), (gm_f, gm_b), (ee_f, ee_b), (ml_f, ml_b)
        cols = jnp.concatenate([q for pair in kinds for q in pair], axis=0)
        pad = jnp.zeros((LANES - cols.shape[0], BLK), F32)
        for c0 in range(0, R, BLK):
            gcol_ref[0, r0 + c0:r0 + c0 + BLK, :] = jnp.concatenate([cols[:, c0:c0 + BLK], pad], axis=0).T
        yield
        sec = lambda s: _dot(h, w_ref[:, s[0]:s[1]])
        aq = sec(_AQ)
        ssq = _dot((aq * aq).astype(BF16), bdq_ref[...])
        aq_ref[0, rows, :] = (aq * lax.rsqrt(ssq * (1.0 / ATT_DH) + EPS) * qg_ref[...]).astype(BF16)
        yield
        ak = sec(_AK)
        ssk = _dot((ak * ak).astype(BF16), bdk_ref[...])
        ak_ref[0, rows, :] = (ak * lax.rsqrt(ssk * (1.0 / ATT_DH) + EPS) * kg_ref[...]).astype(BF16)
        av_ref[0, rows, :] = sec(_AV).astype(BF16)
        yield
        raws[r0] = sec(_MQK)
        yield
        mv_ref[0, rows, :] = sec(_MV).astype(BF16)
        yield
        mo_ref[0, rows, :] = sec(_MO).astype(BF16)

    _interleave(chain(r0) for r0 in range(0, tm, R))
    i, ni = pl.program_id(1), pl.num_programs(1)
    h_halo = jnp.concatenate([_rms(xp_ref[0], g_ref[...]), _rms(xn_ref[0], g_ref[...])], axis=0).astype(BF16)
    halo = _dot(h_halo, w_ref[:, _MQK[0]:_MQK[1]])
    before = jnp.where(i == 0, 0.0, halo[HALO_X - 1:HALO_X, :])
    after = jnp.where(i == ni - 1, 0.0, halo[HALO_X:HALO_X + 1, :])
    raw = jnp.concatenate([raws[r0] for r0 in range(0, tm, R)], axis=0)
    r = _iota((tm, 1), 0)
    x_prev = jnp.where(r == 0, before, pltpu.roll(raw, 1, axis=0))
    x_next = jnp.where(r == tm - 1, after, pltpu.roll(raw, tm - 1, axis=0))
    y = cw_ref[0:1, :] * x_prev + cw_ref[1:2, :] * raw + cw_ref[2:3, :] * x_next + cb_ref[...]
    k_scale = jnp.where(_iota((1, y.shape[1]), 1) < ML_HEADS * ML_DH, 1.0, ML_DH ** -0.5)
    mqk_ref[0] = (y * _sigmoid(y) * k_scale).astype(BF16)


def _inproj(x, g, w_main, wg_t, gate_b, q_g, k_g, conv_w, conv_b, tm):
    B, S, D = x.shape
    per = tm // HALO_X
    last = S // HALO_X - 1
    bdq = (jnp.arange(512)[:, None] // ATT_DH == jnp.arange(512)[None, :] // ATT_DH).astype(BF16)
    bdk = bdq[:128, :128]
    R = min(INPROJ_ROWS, tm)
    s_from, s_to = jnp.arange(R)[:, None], jnp.arange(R)[None, :]
    same = s_from // BLK == s_to // BLK
    tri_f = (same & (s_from <= s_to)).astype(BF16)
    tri_b = (same & (s_from >= s_to)).astype(BF16)
    full = lambda *s: pl.BlockSpec(s, lambda b, i: (0,) * len(s))
    tok = lambda w: pl.BlockSpec((1, tm, w), lambda b, i: (b, i, 0))
    widths = (512, 128, 128, 1024, 512, 512)
    return pl.pallas_call(
        _inproj_kernel,
        grid=(B, S // tm),
        in_specs=[tok(D),
                  pl.BlockSpec((1, HALO_X, D), lambda b, i: (b, jnp.maximum(i * per - 1, 0), 0)),
                  pl.BlockSpec((1, HALO_X, D), lambda b, i: (b, jnp.minimum((i + 1) * per, last), 0)),
                  full(1, D), full(D, w_main.shape[1]), full(16, D), full(16, 1),
                  full(1, 512), full(1, 128), full(512, 512), full(128, 128), full(R, R), full(R, R), full(R, R),
                  full(3, 2 * ML_HEADS * ML_DH), full(1, 2 * ML_HEADS * ML_DH)],
        out_specs=[tok(w) for w in widths] + [pl.BlockSpec((1, 2 * ML_HEADS, tm), lambda b, i: (b, 0, i)), tok(LANES)],
        out_shape=[jax.ShapeDtypeStruct((B, S, w), BF16) for w in widths]
        + [jax.ShapeDtypeStruct((B, 2 * ML_HEADS, S), F32), jax.ShapeDtypeStruct((B, S, LANES), F32)],
        compiler_params=pltpu.CompilerParams(dimension_semantics=("arbitrary", "arbitrary"),
                                             vmem_limit_bytes=VMEM_LIMIT),
        name="inproj",
    )(x, x, x, g, w_main, wg_t, gate_b, q_g, k_g, bdq, bdk, tri_f, tri_b, same.astype(BF16), conv_w, conv_b)


ATT_GROUP = ATT_HEADS // ATT_KV
ATT_HEAD_ORDER = tuple(kv * ATT_GROUP + g for g in range(ATT_GROUP) for kv in range(ATT_KV))


ATT_QB = 8


def _attn_kernel(sink_ref, q_ref, kp_ref, kc_ref, kn_ref, vp_ref, vc_ref, vn_ref, bias_ref, o_ref):
    n = pl.program_id(1)
    nb = pl.num_programs(1)
    G, J = ATT_GROUP, ATT_QB
    kband = jnp.concatenate([kp_ref[0], kc_ref[0], kn_ref[0]], axis=0)
    vband = jnp.concatenate([vp_ref[0], vc_ref[0], vn_ref[0]], axis=0)
    first = _iota(kband.shape, 1) < ATT_DH
    kband, vband = kband.astype(F32), vband.astype(F32)
    keep_lanes = lambda a, mine: jnp.where(mine, a, 0.0).astype(BF16)
    si = _iota((1, (J + 2) * BLK), 1)
    edge = jnp.where(((si < BLK) & (n == 0)) | ((si >= (J + 1) * BLK) & (n == nb - 1)), NEG, 0.0)
    lane_first = _iota((1, LANES), 1) < ATT_DH
    k_kv = [keep_lanes(kband, first), keep_lanes(kband, ~first)]
    v_kv = [jnp.concatenate([keep_lanes(vband, m), m.astype(F32).astype(BF16)], axis=1) for m in (first, ~first)]
    def block(j):
        rows = slice(j * BLK, (j + 1) * BLK)
        band = slice(j * BLK, (j + 3) * BLK)
        q = q_ref[0, rows, :]
        qs = jnp.concatenate([q[:, g * LANES:(g + 1) * LANES] for g in range(G)], axis=0)
        tot = None
        sink_terms = []
        for kv in range(ATT_KV):
            s = _dot_nt(qs, k_kv[kv][band])
            ps, st = [], []
            for g in range(G):
                h = kv * G + g
                sink = sink_ref[h]
                sg = s[g * BLK:(g + 1) * BLK] + bias_ref[h] + edge[:, band]
                m = jnp.maximum(jnp.max(sg, axis=-1, keepdims=True), sink)
                ps.append(jnp.exp(sg - m).astype(BF16))
                st.append(jnp.exp(sink - m))
            yield
            part = _dot(jnp.concatenate(ps, axis=0), v_kv[kv][band])
            tot = part if tot is None else tot + part
            sink_terms.append(jnp.concatenate(st, axis=0))
            yield
        den = tot[:, LANES:] + jnp.where(lane_first, sink_terms[0], sink_terms[1])
        out = (tot[:, :LANES] / den).astype(BF16)
        o_ref[0, rows, :] = jnp.concatenate([out[g * BLK:(g + 1) * BLK] for g in range(G)], axis=1)

    _interleave(block(j) for j in range(J))


def _attention(sink, aq, ak, av):
    B, S, _ = aq.shape
    J = ATT_QB
    nb = S // (J * BLK)
    last_blk = S // BLK - 1
    kvw = ATT_KV * ATT_DH
    assert kvw == LANES and S % (J * BLK) == 0
    dist = jnp.abs(jnp.arange(BLK)[:, None] + BLK - jnp.arange(3 * BLK)[None, :]).astype(F32)
    slopes = jnp.exp2(-8.0 * jnp.arange(1, ATT_HEADS + 1, dtype=F32) / ATT_HEADS)
    bias = jnp.where(dist <= BLK, -slopes[:, None, None] * dist, NEG)
    prev = pl.BlockSpec((1, BLK, kvw), lambda b, n, *_: (b, jnp.maximum(J * n - 1, 0), 0))
    cur = pl.BlockSpec((1, J * BLK, kvw), lambda b, n, *_: (b, n, 0))
    nxt = pl.BlockSpec((1, BLK, kvw), lambda b, n, *_: (b, jnp.minimum(J * n + J, last_blk), 0))
    qspec = pl.BlockSpec((1, J * BLK, ATT_HEADS * ATT_DH), lambda b, n, *_: (b, n, 0))
    bspec = pl.BlockSpec((ATT_HEADS, BLK, 3 * BLK), lambda b, n, *_: (0, 0, 0))
    return pl.pallas_call(
        _attn_kernel,
        grid_spec=pltpu.PrefetchScalarGridSpec(
            num_scalar_prefetch=1, grid=(B, nb),
            in_specs=[qspec, prev, cur, nxt, prev, cur, nxt, bspec], out_specs=qspec),
        out_shape=jax.ShapeDtypeStruct(aq.shape, BF16),
        compiler_params=pltpu.CompilerParams(dimension_semantics=("arbitrary", "arbitrary"),
                                             vmem_limit_bytes=VMEM_LIMIT),
        name="win_attn",
    )(sink, aq, ak, ak, ak, av, av, av, bias)


ML_CPS = 8


def _mlstm_kernel(xf_ref, xb_ref, vf_ref, vb_ref, grf_ref, grb_ref, gcf_ref, gcb_ref, hf_ref, hb_ref, cn_ref, m_ref):
    c = pl.program_id(1)
    L = BLK
    width = ML_HEADS * ML_DH

    @pl.when(c == 0)
    def _():
        cn_ref[...] = jnp.zeros_like(cn_ref)
        m_ref[...] = jnp.zeros_like(m_ref)

    tt = _iota((L, L), 0)
    ss = _iota((L, L), 1)
    ones_v = jnp.ones((L, ML_DH), BF16)

    def chunk(d, sub, qk_ref, v_ref, grow_ref, gcol_ref, out_ref):
        H = ML_HEADS
        rows = slice(sub * L, (sub + 1) * L)
        keep = (ss <= tt) if d == 0 else (ss >= tt)
        g_row = grow_ref[0, H * d:H * (d + 1), rows]
        gc = gcol_ref[0, rows, :]
        kind = lambda i: gc[:, 2 * H * i + H * d:2 * H * i + H * (d + 1)]
        f, g_max, e_end, m_loc = kind(GC_F), kind(GC_GMAX), kind(GC_EEND), kind(GC_MLOC)[0:1, :]
        end = L - 1 if d == 0 else 0
        f_end = f[end:end + 1, :]
        m0 = m_ref[d, 0:1, 0:H]
        mm = jnp.maximum(m0, g_max)
        e_inter = jnp.exp(m0 - mm)
        floor = jnp.exp(-(f + mm))
        m_new = jnp.maximum(f_end + m0, m_loc)
        ca = jnp.exp(f_end + m0 - m_new)
        cb = jnp.exp(m_loc - m_new)
        m_ref[d, 0:1, 0:H] = m_new
        col = lambda a, h, w=ML_DH: jnp.broadcast_to(a[:, h:h + 1], (L, w))
        yield
        for h in range(H):
            u = H * d + h
            hs = slice(h * ML_DH, (h + 1) * ML_DH)
            qb = qk_ref[0, rows, hs]
            kb = qk_ref[0, rows, width + h * ML_DH: width + (h + 1) * ML_DH]
            v1 = jnp.concatenate([v_ref[0, rows, hs], ones_v], axis=1)
            cn0 = cn_ref[u]
            dec = jnp.where(keep, jnp.exp(g_row[h:h + 1, :] - col(mm, h)), 0.0)
            s_qk = (_dot_nt(qb, kb) * dec).astype(BF16)
            yield
            tot = col(e_inter, h, 2 * ML_DH) * _dot(qb, cn0.astype(BF16)) + _dot(s_qk, v1)
            out_ref[0, rows, hs] = tot[:, :ML_DH] / jnp.maximum(jnp.abs(tot[:, ML_DH:]), col(floor, h))
            ks = kb.astype(F32) * col(e_end, h)
            cn_ref[u] = ca[:, h:h + 1] * cn0 + cb[:, h:h + 1] * _dot(ks.T.astype(BF16), v1)
            yield

    def direction(d, *refs):
        for i in range(ML_CPS):
            yield from chunk(d, i if d == 0 else ML_CPS - 1 - i, *refs)

    _interleave([direction(0, xf_ref, vf_ref, grf_ref, gcf_ref, hf_ref),
                 direction(1, xb_ref, vb_ref, grb_ref, gcb_ref, hb_ref)])


def _mlstm(mqk, mv, g_rows, g_cols):
    B, S, _ = mqk.shape
    rows = ML_CPS * BLK
    ns = S // rows
    assert S % rows == 0
    width = ML_HEADS * ML_DH
    fwd = lambda c: c
    bwd = lambda c: ns - 1 - c
    qkspec = lambda ci: pl.BlockSpec((1, rows, 2 * width), lambda b, c: (b, ci(c), 0))
    vspec = lambda ci: pl.BlockSpec((1, rows, width), lambda b, c: (b, ci(c), 0))
    gspec = lambda ci: pl.BlockSpec((1, 2 * ML_HEADS, rows), lambda b, c: (b, 0, ci(c)))
    cspec = lambda ci: pl.BlockSpec((1, rows, LANES), lambda b, c: (b, ci(c), 0))
    units = 2 * ML_HEADS
    return pl.pallas_call(
        _mlstm_kernel,
        grid=(B, ns),
        in_specs=[qkspec(fwd), qkspec(bwd), vspec(fwd), vspec(bwd), gspec(fwd), gspec(bwd), cspec(fwd), cspec(bwd)],
        out_specs=[vspec(fwd), vspec(bwd)],
        out_shape=[jax.ShapeDtypeStruct((B, S, width), F32)] * 2,
        scratch_shapes=[pltpu.VMEM((units, ML_DH, 2 * ML_DH), F32), pltpu.VMEM((2, 8, LANES), F32)],
        compiler_params=pltpu.CompilerParams(dimension_semantics=("arbitrary", "arbitrary"),
                                             vmem_limit_bytes=VMEM_LIMIT),
        name="mlstm",
    )(mqk, mqk, mv, mv, g_rows, g_rows, g_cols, g_cols)


MIX_ROWS = 256


def _mix_kernel(x_ref, att_ref, hf_ref, hb_ref, mo_ref, og_ref, wo_ref,
                gm_ref, wq_ref, mqg_ref, k_ref, v_ref, wmo_ref,
                gf_ref, wr_ref, brt_ref,
                slab_ref, afft_ref):
    width = ML_HEADS * ML_DH
    k = k_ref[0]
    v = v_ref[0]
    wr = wr_ref[...]
    w_hi = wr.astype(BF16)
    w_lo = (wr - w_hi.astype(F32)).astype(BF16)
    def chain(r0):
        rows = slice(r0, r0 + MIX_ROWS)
        x = x_ref[0, rows, :]
        ml = hf_ref[0, rows, :] + hb_ref[0, rows, :]
        mo = mo_ref[0, rows, :].astype(F32)
        parts = []
        for h in range(ML_HEADS):
            sl = slice(h * ML_DH, (h + 1) * ML_DH)
            parts.append((_sigmoid(mo[:, sl]) * _rms(ml[:, sl], og_ref[:, sl])).astype(BF16))
        ml_out = jnp.concatenate(parts, axis=1)
        yield
        y1 = x + _dot(att_ref[0, rows, :], wo_ref[0:width, :]) + _dot(ml_out, wo_ref[width:2 * width, :])
        yield
        h2 = _rms(y1, gm_ref[...]).astype(BF16)
        qm = _dot(h2, wq_ref[...])
        yield
        outs = []
        for h in range(MEM_HEADS):
            sl = slice(h * MEM_DH, (h + 1) * MEM_DH)
            qh = (_rms(qm[:, sl], mqg_ref[...]) * (MEM_DH ** -0.5)).astype(BF16)
            s = _dot_nt(qh, k[:, sl])
            p = jnp.exp(s - jnp.max(s, axis=-1, keepdims=True))
            o = _dot(p.astype(BF16), v[:, sl]) / jnp.sum(p, axis=-1, keepdims=True)
            outs.append(o.astype(BF16))
            yield
        y2 = y1 + _dot(jnp.concatenate(outs, axis=1), wmo_ref[...])
        yield
        h3 = _rms(y2, gf_ref[...])
        h_hi = h3.astype(BF16)
        h_lo = (h3 - h_hi.astype(F32)).astype(BF16)
        logits = _dot(h_hi, w_hi) + _dot(h_lo, w_hi) + _dot(h_hi, w_lo)
        yield
        logits_t = jnp.concatenate([logits[q0:q0 + LANES, :].T[:N_EXPERTS, :] for q0 in range(0, MIX_ROWS, LANES)],
                                   axis=1) + brt_ref[...]
        pt = jnp.exp(logits_t - jnp.max(logits_t, axis=0, keepdims=True))
        afft_ref[0, :, rows] = pt / jnp.sum(pt, axis=0, keepdims=True)
        chunks = y2.shape[1] // LANES
        for c in range(chunks):
            tiles = pl.ds(r0 * chunks + c, MIX_ROWS, stride=chunks)
            slab_ref[SLAB_ACC, tiles, :] = y2[:, c * LANES:(c + 1) * LANES]
            slab_ref[SLAB_X, tiles, :] = h3[:, c * LANES:(c + 1) * LANES]

    _interleave(chain(r0) for r0 in range(0, x_ref.shape[1], MIX_ROWS))


def _mix(x, att, hf, hb, mo, out_g, w_out, g_mem, w_q, mq_g, mem_k, mem_v, w_mo, g_ffn, w_r, b_r, tm):
    B, S, D = x.shape
    M = mem_k.shape[1]
    width = ML_HEADS * ML_DH
    mw = MEM_HEADS * MEM_DH
    full = lambda *s: pl.BlockSpec(s, lambda b, i: (0,) * len(s))
    tok = lambda w: pl.BlockSpec((1, tm, w), lambda b, i: (b, i, 0))
    chunks = D // LANES
    tiled = pl.BlockSpec((2, tm * chunks, LANES), lambda b, i: (0, b * (S // tm) + i, 0))
    memspec = pl.BlockSpec((1, M, mw), lambda b, i: (b, 0, 0))
    tiled_shape = jax.ShapeDtypeStruct((2, B * S * chunks, LANES), F32)
    return pl.pallas_call(
        _mix_kernel,
        grid=(B, S // tm),
        in_specs=[tok(D), tok(width), tok(width), tok(width), tok(width), full(1, width), full(2 * width, D),
                  full(1, D), full(D, mw), full(1, MEM_DH), memspec, memspec, full(mw, D),
                  full(1, D), full(D, LANES), full(N_EXPERTS, 1)],
        out_specs=[tiled, pl.BlockSpec((1, N_EXPERTS, tm), lambda b, i: (b, 0, i))],
        out_shape=[tiled_shape, jax.ShapeDtypeStruct((B, N_EXPERTS, S), F32)],
        compiler_params=pltpu.CompilerParams(dimension_semantics=("arbitrary", "arbitrary"),
                                             vmem_limit_bytes=VMEM_LIMIT),
        name="mix_mem_router",
    )(x, att, hf, hb, mo, out_g, w_out, g_mem, w_q, mq_g, mem_k, mem_v, w_mo, g_ffn,
      jnp.pad(w_r, ((0, 0), (0, LANES - N_EXPERTS))), b_r.T)


TOPC_GROUP = 8


def _topc_kernel(aff2_ref, aff3_ref, idx_ref, gval_ref, *, cap, seq, row_pitch):
    a2 = aff2_ref[0]
    bits2 = pltpu.bitcast(a2, I32)
    capf = float(cap)

    def bisect(i, lo):
        cand = lo | jnp.left_shift(jnp.int32(1), 30 - i)
        cnt = jnp.sum((bits2 >= cand).astype(F32), axis=1, keepdims=True)
        return jnp.where(cnt >= capf, cand, lo)

    thr_all = lax.fori_loop(0, 31, bisect, jnp.zeros((N_EXPERTS, 1), I32))
    need_all = capf - jnp.sum((bits2 > thr_all).astype(F32), axis=1, keepdims=True)

    T = aff3_ref.shape[2]
    tri_u = (_iota((LANES, LANES), 0) <= _iota((LANES, LANES), 1)).astype(BF16)
    tri_l = (_iota((LANES, LANES), 1) <= _iota((LANES, LANES), 0)).astype(BF16)
    ones8 = jnp.ones((8, LANES), BF16)
    before = _iota((T, T), 1) < _iota((T, T), 0)
    kcol = _iota((T, 1), 0).astype(F32)
    j = _iota((1, cap), 1).astype(F32)
    assert T < LANES
    tile_pad = jnp.zeros((LANES - T, LANES), F32)
    lane_pos = _iota((LANES, cap), 0).astype(F32)

    def tile_starts(maskb):
        tot_row = _dot_nt(ones8, maskb)[0:1, :]
        return jnp.sum(jnp.where(before, tot_row, 0.0), axis=1, keepdims=True)

    def select(e):
        bits = pltpu.bitcast(aff3_ref[0, e], I32)
        thr = thr_all[e:e + 1, :]
        need = need_all[e:e + 1, :]
        gt = bits > thr
        eq = bits == thr
        eqb = eq.astype(BF16)
        eq_rank = _dot(eqb, tri_u) + tile_starts(eqb) - eq.astype(F32)
        return (gt | (eq & (eq_rank < need))).astype(BF16)

    def locate(selb):
        cs = _dot(selb, tri_u)
        start = tile_starts(selb)
        end = start + cs[:, LANES - 1:LANES]
        onehot = (start <= j) & (j < end)
        ohf = onehot.astype(F32)
        tile_of = jnp.sum(ohf * kcol, axis=0, keepdims=True)
        j_loc = j - jnp.sum(ohf * start, axis=0, keepdims=True)
        cs_t = _dot_nt(tri_l, selb)
        return onehot.astype(BF16), tile_of, j_loc, cs_t

    def emit(e, ohb, tile_of, j_loc, cs_t):
        r_t = _dot(cs_t.astype(BF16), ohb)
        local = jnp.sum((r_t <= j_loc).astype(F32), axis=0, keepdims=True)
        idx_ref[0, e:e + 1, :] = ((tile_of * LANES + local).astype(I32) + pl.program_id(0) * seq) * row_pitch
        a_t = jnp.concatenate([aff3_ref[0, e], tile_pad], axis=0).T[:, :T]
        a_hi = a_t.astype(BF16)
        a_lo = (a_t - a_hi.astype(F32)).astype(BF16)
        a_tile = _dot(a_hi, ohb) + _dot(a_lo, ohb)
        gval_ref[0, e:e + 1, :] = jnp.sum(jnp.where(lane_pos == local, a_tile, 0.0), axis=0, keepdims=True)

    for e0 in range(0, N_EXPERTS, TOPC_GROUP):
        group = range(e0, e0 + TOPC_GROUP)
        sels = [select(e) for e in group]
        locs = [locate(selb) for selb in sels]
        for e, loc in zip(group, locs):
            emit(e, *loc)


def _topc(aff_t, cap, row_pitch):
    B, E, S = aff_t.shape
    T = S // LANES
    aff3 = aff_t.reshape(B, E, T, LANES)
    return pl.pallas_call(
        functools.partial(_topc_kernel, cap=cap, seq=S, row_pitch=row_pitch),
        grid=(B,),
        in_specs=[pl.BlockSpec((1, E, S), lambda b: (b, 0, 0)), pl.BlockSpec((1, E, T, LANES), lambda b: (b, 0, 0, 0))],
        out_specs=[pl.BlockSpec((1, E, cap), lambda b: (b, 0, 0))] * 2,
        out_shape=[jax.ShapeDtypeStruct((B, E, cap), I32), jax.ShapeDtypeStruct((B, E, cap), F32)],
        compiler_params=pltpu.CompilerParams(dimension_semantics=("arbitrary",), vmem_limit_bytes=VMEM_LIMIT),
        name="topc",
    )(aff_t, aff3)


FFN_ROWS = 256
FFN_COLS = 256
AB_RING = 3


SLAB_X, SLAB_ACC = 0, 1


def _ffn_kernel(idxp_ref, idxn_ref, gv_ref, slab_in_hbm, wg32_ref, wu32_ref, wd32_ref, slab_hbm,
                buf_ref, wg_ref, wu_ref, wd_ref, sems, *, nb, cap, d_model, ff):
    del slab_in_hbm
    s = pl.program_id(0)
    last = pl.num_programs(0) - 1

    @pl.when(s % nb == 0)
    def _():
        for src, dst in ((wg32_ref, wg_ref), (wu32_ref, wu_ref), (wd32_ref, wd_ref)):
            for r0 in range(0, src.shape[1], FFN_ROWS):
                dst[0, r0:r0 + FFN_ROWS, :] = src[0, r0:r0 + FFN_ROWS, :].astype(BF16)

    a_cur = s % AB_RING
    a_nxt = (s + 1) % AB_RING
    a_prv = (s + 2) % AB_RING
    chunks = d_model // LANES

    def rows_of(start):
        return pl.ds(pl.multiple_of(start, chunks), chunks)

    def gather(r, j, sl):
        return pltpu.make_async_copy(slab_hbm.at[:, rows_of(r)], buf_ref.at[sl, :, rows_of(j * chunks)], sems.at[0])

    def scatter(r, j, sl):
        return pltpu.make_async_copy(buf_ref.at[sl, SLAB_ACC, rows_of(j * chunks)], slab_hbm.at[SLAB_ACC, rows_of(r)],
                                     sems.at[1])

    def wait_gather(sl):
        pltpu.make_async_copy(slab_hbm.at[:, pl.ds(0, cap * chunks)], buf_ref.at[sl], sems.at[0]).wait()

    def wait_scatter(sl):
        pltpu.make_async_copy(buf_ref.at[sl, SLAB_ACC], slab_hbm.at[SLAB_ACC, pl.ds(0, cap * chunks)],
                              sems.at[1]).wait()

    @pl.when(s == 0)
    def _():
        @pl.loop(0, cap)
        def _(j):
            r = idxp_ref[0, 0, j]
            gather(r, j, 0).start()
            gather(r, j, AB_RING - 1).start()
        wait_gather(AB_RING - 1)

    wait_gather(a_cur)

    halves = FFN_COLS // LANES
    nblk = cap // FFN_ROWS
    n_ct = ff // FFN_COLS
    assert d_model // FFN_COLS == n_ct

    n_issues = 2 * nblk * n_ct
    gather_rows = -(-cap // n_issues)
    scatter_rows = -(-cap // n_issues)
    next_gather, next_scatter, n_issue = [0], [0], [0]

    def issue_rows():
        n_issue[0] += 1
        lo, hi = next_gather[0], min(next_gather[0] + gather_rows, cap)
        next_gather[0] = hi
        for j in range(lo, hi):
            gather(idxn_ref[0, 0, j], j, a_nxt).start(priority=0)
        lo, hi = next_scatter[0], min(next_scatter[0] + scatter_rows, cap)
        next_scatter[0] = hi
        for j in range(lo, hi):
            scatter(idxp_ref[0, 0, j], j, a_prv).start(priority=1)

    def chunk_rows(rb, c):
        return pl.ds(rb * FFN_ROWS * chunks + c, FFN_ROWS, stride=chunks)

    def gate_rows(rb):
        g_row = gv_ref[0, rb]
        return jnp.concatenate(
            [jnp.broadcast_to(g_row[:, i * LANES:(i + 1) * LANES], (LANES, LANES)).T for i in range(FFN_ROWS // LANES)],
            axis=0)

    hid_prev = None
    for stage in range(nblk + 1):
        up, dn = stage < nblk, stage >= 1
        if up:
            xb = jnp.concatenate([buf_ref[a_cur, SLAB_X, chunk_rows(stage, c), :] for c in range(chunks)],
                                 axis=1).astype(BF16)
        if dn:
            gval = gate_rows(stage - 1)
        hid = []
        for ct in range(n_ct):
            cs = slice(ct * FFN_COLS, (ct + 1) * FFN_COLS)
            if up:
                hg = _dot(xb, wg_ref[0, :, cs])
                issue_rows()
                hu = _dot(xb, wu_ref[0, :, cs])
                hid.append((hg * _sigmoid(hg) * hu).astype(BF16))
            if dn:
                ye = _dot(hid_prev, wd_ref[0, :, cs])
                for i in range(halves):
                    buf_ref[a_cur, SLAB_ACC, chunk_rows(stage - 1, ct * halves + i), :] += (
                        ye[:, i * LANES:(i + 1) * LANES] * gval)
            if up:
                issue_rows()
        hid_prev = jnp.concatenate(hid, axis=1) if up else None
    assert next_gather[0] == cap and next_scatter[0] == cap and n_issue[0] == n_issues

    wait_scatter(a_prv)

    @pl.when(s == last)
    def _():
        wait_gather(a_nxt)

        @pl.loop(0, cap)
        def _(j):
            scatter(idxn_ref[0, 0, j], j, a_cur).start()
        wait_scatter(a_cur)


def _expert_ffn(idx, gval, slab, wg, wu, wd):
    B, E, cap = idx.shape
    d_model, ff = wg.shape[1], wg.shape[2]
    chunks = d_model // LANES
    assert B >= 3 and cap % FFN_ROWS == 0 and ff % FFN_COLS == 0 and d_model % FFN_COLS == 0
    ns = E * B
    nblk = cap // FFN_ROWS
    idx3 = idx.reshape(B * E, 1, cap)
    gv4 = gval.reshape(B * E, nblk, 1, FFN_ROWS)
    any_spec = pl.BlockSpec(memory_space=pl.ANY)
    wspec = lambda r, c: pl.BlockSpec((1, r, c), lambda s: (s // B, 0, 0))
    blk = lambda s: (s % B) * E + s // B
    ispec = lambda f: pl.BlockSpec((1, 1, cap), lambda s: (blk(f(s)), 0, 0), memory_space=pltpu.SMEM)
    kern = functools.partial(_ffn_kernel, nb=B, cap=cap, d_model=d_model, ff=ff)
    return pl.pallas_call(
        kern,
        grid=(ns,),
        in_specs=[ispec(lambda s: jnp.maximum(s - 1, 0)), ispec(lambda s: jnp.minimum(s + 1, ns - 1)),
                  pl.BlockSpec((1, nblk, 1, FFN_ROWS), lambda s: (blk(s), 0, 0, 0)),
                  any_spec, wspec(d_model, ff), wspec(d_model, ff), wspec(ff, d_model)],
        out_specs=any_spec,
        out_shape=jax.ShapeDtypeStruct(slab.shape, F32),
        scratch_shapes=[pltpu.VMEM((AB_RING, 2, cap * chunks, LANES), F32),
                        pltpu.VMEM((1, d_model, ff), BF16), pltpu.VMEM((1, d_model, ff), BF16),
                        pltpu.VMEM((1, ff, d_model), BF16), pltpu.SemaphoreType.DMA((2,))],
        input_output_aliases={3: 0},
        compiler_params=pltpu.CompilerParams(dimension_semantics=("arbitrary",), vmem_limit_bytes=FFN_VMEM_LIMIT),
        name="expert_ffn",
    )(idx3, idx3, gv4, slab, wg, wu, wd)


def _untile_kernel(a_ref, o_ref):
    tm, chunks = o_ref.shape[0], o_ref.shape[1] // LANES
    for c in range(chunks):
        o_ref[:, c * LANES:(c + 1) * LANES] = a_ref[0, pl.ds(c, tm, stride=chunks), :]


def _untile(slab, plane, chunks, tm):
    N = slab.shape[1] // chunks
    return pl.pallas_call(
        _untile_kernel,
        grid=(N // tm,),
        in_specs=[pl.BlockSpec((1, tm * chunks, LANES), lambda i: (plane, i, 0))],
        out_specs=pl.BlockSpec((tm, chunks * LANES), lambda i: (i, 0)),
        out_shape=jax.ShapeDtypeStruct((N, chunks * LANES), F32),
        compiler_params=pltpu.CompilerParams(dimension_semantics=("arbitrary",), vmem_limit_bytes=VMEM_LIMIT),
        name="untile",
    )(slab)


def _layer(x, mem, norm_mix_g, w_in, att_q_norm_g, att_k_norm_g, att_sink, ml_conv_w, ml_conv_b,
           ml_gate_b, ml_out_norm_g, w_out, norm_mem_g, mem_kv_norm_g, w_mem_q, w_mem_kv,
           mem_q_norm_g, mem_k_norm_g, w_mem_o, norm_ffn_g, w_router, b_router,
           w_exp_gate, w_exp_up, w_exp_down):
    B, S, D = x.shape
    row = lambda v: v.reshape(1, -1).astype(F32)
    tm_in = min(1024, S)
    tm_mix = min(2 * MIX_ROWS, S)
    cap = CAPACITY_FACTOR * S // N_EXPERTS

    mem_k, mem_v = _mem_kv(mem, row(mem_kv_norm_g), w_mem_kv.astype(BF16), row(mem_k_norm_g))

    order = jnp.array(ATT_HEAD_ORDER)
    w_aq = w_in[:, _AQ[0]:_AQ[1]].reshape(D, ATT_HEADS, ATT_DH)[:, order].reshape(D, -1)
    w_main = jnp.concatenate([w_aq, w_in[:, _AQ[1]:_GATES[0]]], axis=1).astype(BF16)
    wg_t = w_in[:, _GATES[0]:_GATES[1]].T.astype(BF16)
    q_g = jnp.tile(row(att_q_norm_g), (1, ATT_HEADS)) * (ATT_DH ** -0.5)
    k_g = jnp.tile(row(att_k_norm_g), (1, ATT_KV))
    aq, ak, av, mqk, mv, mo, g_rows, g_cols = _inproj(x, row(norm_mix_g), w_main, wg_t, ml_gate_b.reshape(-1, 1).astype(F32),
                                               q_g, k_g, ml_conv_w.astype(F32), row(ml_conv_b), tm_in)

    att = _attention(att_sink.astype(F32), aq, ak, av)
    hf, hb = _mlstm(mqk, mv, g_rows, g_cols)

    att_w = ATT_HEADS * ATT_DH
    w_out_att = w_out[:att_w].reshape(ATT_HEADS, ATT_DH, D)[order].reshape(att_w, D)
    w_out_p = jnp.concatenate([w_out_att, w_out[att_w:]], axis=0).astype(BF16)
    slab, aff_t = _mix(x, att, hf, hb, mo, row(ml_out_norm_g), w_out_p, row(norm_mem_g),
                             w_mem_q.astype(BF16), row(mem_q_norm_g), mem_k, mem_v, w_mem_o.astype(BF16),
                             row(norm_ffn_g), w_router.astype(F32), row(b_router), tm_mix)

    idx, gval = _topc(aff_t, cap, D // LANES)
    slab = _expert_ffn(idx, gval, slab, w_exp_gate.astype(F32), w_exp_up.astype(F32), w_exp_down.astype(F32))
    return _untile(slab, SLAB_ACC, D // LANES, tm_in).reshape(B, S, D)


def kernel(x, mem, norm_mix_g, w_in, att_q_norm_g, att_k_norm_g, att_sink, ml_conv_w, ml_conv_b, ml_gate_b,
           ml_out_norm_g, w_out, norm_mem_g, mem_kv_norm_g, w_mem_q, w_mem_kv, mem_q_norm_g, mem_k_norm_g,
           w_mem_o, norm_ffn_g, w_router, b_router, w_exp_gate, w_exp_up, w_exp_down):
    params = (norm_mix_g, w_in, att_q_norm_g, att_k_norm_g, att_sink, ml_conv_w, ml_conv_b, ml_gate_b,
              ml_out_norm_g, w_out, norm_mem_g, mem_kv_norm_g, w_mem_q, w_mem_kv, mem_q_norm_g, mem_k_norm_g,
              w_mem_o, norm_ffn_g, w_router, b_router, w_exp_gate, w_exp_up, w_exp_down)
    depth = norm_mix_g.shape[0]
    for l in range(depth):
        x = _layer(x, mem, *[p[l] for p in params])
    return x
```

```python
import functools

import jax
import jax.numpy as jnp
from jax import lax
from jax.experimental import pallas as pl
from jax.experimental.pallas import tpu as pltpu

F32 = jnp.float32
BF16 = jnp.bfloat16
I32 = jnp.int32

EPS = 1e-6
LANES = 128
BLK = 128
ATT_HEADS, ATT_KV, ATT_DH = 8, 2, 64
ML_HEADS, ML_DH = 4, 128
MEM_HEADS, MEM_DH = 4, 128
N_EXPERTS = 16
CAPACITY_FACTOR = 2
NEG = -1e30
HALO_X = 8
VMEM_LIMIT = 48 * 1024 * 1024
FFN_VMEM_LIMIT = 57 * 1024 * 1024

_NT = (((1,), (1,)), ((), ()))


def _dot(a, b):
    return jnp.dot(a, b, preferred_element_type=F32)


def _dot_nt(a, b):
    return lax.dot_general(a, b, _NT, preferred_element_type=F32)


def _split3(x):
    hi = x.astype(BF16)
    r1 = x - hi.astype(F32)
    mid = r1.astype(BF16)
    lo = (r1 - mid.astype(F32)).astype(BF16)
    return hi, mid, lo


def _dot01_nt(m01, x):
    hi, mid, lo = _split3(x)
    return _dot_nt(m01, hi) + _dot_nt(m01, mid) + _dot_nt(m01, lo)


def _dot01(x, m01):
    hi, mid, lo = _split3(x)
    return _dot(hi, m01) + _dot(mid, m01) + _dot(lo, m01)


def _rms(x, g):
    ms = jnp.mean(x * x, axis=-1, keepdims=True)
    return x * lax.rsqrt(ms + EPS) * g


def _sigmoid(x):
    return 1.0 / (1.0 + jnp.exp(-x))


def _iota(shape, dim):
    return lax.broadcasted_iota(I32, shape, dim)


def _round_robin(chains):
    chains = list(chains)
    while chains:
        for ch in list(chains):
            try:
                next(ch)
            except StopIteration:
                chains.remove(ch)
        yield


def _interleave(chains):
    chains = list(chains)
    while chains:
        for ch in list(chains):
            try:
                next(ch)
            except StopIteration:
                chains.remove(ch)


def _mem_kv_kernel(mem_ref, g_ref, w_ref, kg_ref, k_ref, v_ref):
    mn = _rms(mem_ref[0], g_ref[...]).astype(BF16)
    kv = _dot(mn, w_ref[...])
    width = MEM_HEADS * MEM_DH
    for h in range(MEM_HEADS):
        sl = slice(h * MEM_DH, (h + 1) * MEM_DH)
        k_ref[0, :, sl] = _rms(kv[:, sl], kg_ref[...]).astype(BF16)
    v_ref[0] = kv[:, width:].astype(BF16)


def _mem_kv(mem, g, w_kv, k_g):
    B, M, D = mem.shape
    width = MEM_HEADS * MEM_DH
    full = lambda *s: pl.BlockSpec(s, lambda b: (0,) * len(s))
    return pl.pallas_call(
        _mem_kv_kernel,
        grid=(B,),
        in_specs=[pl.BlockSpec((1, M, D), lambda b: (b, 0, 0)), full(1, D), full(D, 2 * width), full(1, MEM_DH)],
        out_specs=[pl.BlockSpec((1, M, width), lambda b: (b, 0, 0))] * 2,
        out_shape=[jax.ShapeDtypeStruct((B, M, width), BF16)] * 2,
        compiler_params=pltpu.CompilerParams(dimension_semantics=("arbitrary",), vmem_limit_bytes=VMEM_LIMIT),
        name="mem_kv",
    )(mem, g, w_kv, k_g)


_AQ = (0, 512)
_AK = (512, 640)
_AV = (640, 768)
_MQK = (768, 1792)
_MV = (1792, 2304)
_MO = (2304, 2816)
_GATES = (2816, 2832)


GC_F, GC_GMAX, GC_EEND, GC_MLOC = 0, 1, 2, 3


INPROJ_ROWS = 256


def _inproj_kernel(x_ref, xp_ref, xn_ref, g_ref, w_ref, wgt_ref, gb_ref, qg_ref, kg_ref, bdq_ref, bdk_ref, trif_ref,
                   trib_ref, blk_ref, cw_ref, cb_ref, aq_ref, ak_ref, av_ref, mqk_ref, mv_ref, mo_ref, grow_ref, gcol_ref):
    tm = x_ref.shape[1]
    R = INPROJ_ROWS
    H = ML_HEADS
    pos = _iota((H, R), 1) % BLK
    raws = {}

    def chain(r0):
        rows = slice(r0, r0 + R)
        h = _rms(x_ref[0, rows, :], g_ref[...]).astype(BF16)
        gt = _dot_nt(wgt_ref[...], h) + gb_ref[...]
        logsig = jnp.minimum(gt, 0.0) - jnp.log(1.0 + jnp.exp(-jnp.abs(gt)))
        yield

        def chunk_scan(li, lf, tri_ref, fwd):
            f = _dot01(lf, tri_ref[...])
            g = li - f
            gmax = g
            sh = 1
            while sh < BLK:
                if fwd:
                    gmax = jnp.where(pos >= sh, jnp.maximum(gmax, pltpu.roll(gmax, sh, axis=1)), gmax)
                else:
                    gmax = jnp.where(pos < BLK - sh, jnp.maximum(gmax, pltpu.roll(gmax, R - sh, axis=1)), gmax)
                sh *= 2
            w_end = _dot01(lf, blk_ref[...]) + g
            m_loc = jnp.concatenate(
                [jnp.broadcast_to(jnp.max(w_end[:, c0:c0 + BLK], axis=1, keepdims=True), (H, BLK))
                 for c0 in range(0, R, BLK)], axis=1)
            return f, g, gmax, jnp.exp(w_end - m_loc), m_loc

        f_f, g_f, gm_f, ee_f, ml_f = chunk_scan(gt[0:H], logsig[H:2 * H], trif_ref, True)
        yield
        f_b, g_b, gm_b, ee_b, ml_b = chunk_scan(gt[2 * H:3 * H], logsig[3 * H:4 * H], trib_ref, False)
        yield
        grow_ref[0, :, rows] = jnp.concatenate([g_f, g_b], axis=0)
        kinds = [None] * 4
        kinds[GC_F], kinds[GC_GMAX], kinds[GC_EEND], kinds[GC_MLOC] = (f_f, f_b), (gm_f, gm_b), (ee_f, ee_b), (ml_f, ml_b)
        cols = jnp.concatenate([q for pair in kinds for q in pair], axis=0)
        pad = jnp.zeros((LANES - cols.shape[0], BLK), F32)
        for c0 in range(0, R, BLK):
            gcol_ref[0, r0 + c0:r0 + c0 + BLK, :] = jnp.concatenate([cols[:, c0:c0 + BLK], pad], axis=0).T
        yield
        sec = lambda s: _dot(h, w_ref[:, s[0]:s[1]])
        aq = sec(_AQ)
        ssq = _dot((aq * aq).astype(BF16), bdq_ref[...])
        aq_ref[0, rows, :] = (aq * lax.rsqrt(ssq * (1.0 / ATT_DH) + EPS) * qg_ref[...]).astype(BF16)
        yield
        ak = sec(_AK)
        ssk = _dot((ak * ak).astype(BF16), bdk_ref[...])
        ak_ref[0, rows, :] = (ak * lax.rsqrt(ssk * (1.0 / ATT_DH) + EPS) * kg_ref[...]).astype(BF16)
        av_ref[0, rows, :] = sec(_AV).astype(BF16)
        yield
        raws[r0] = sec(_MQK)
        yield
        mv_ref[0, rows, :] = sec(_MV).astype(BF16)
        yield
        mo_ref[0, rows, :] = sec(_MO).astype(BF16)

    _interleave(chain(r0) for r0 in range(0, tm, R))
    i, ni = pl.program_id(1), pl.num_programs(1)
    h_halo = jnp.concatenate([_rms(xp_ref[0], g_ref[...]), _rms(xn_ref[0], g_ref[...])], axis=0).astype(BF16)
    halo = _dot(h_halo, w_ref[:, _MQK[0]:_MQK[1]])
    before = jnp.where(i == 0, 0.0, halo[HALO_X - 1:HALO_X, :])
    after = jnp.where(i == ni - 1, 0.0, halo[HALO_X:HALO_X + 1, :])
    raw = jnp.concatenate([raws[r0] for r0 in range(0, tm, R)], axis=0)
    r = _iota((tm, 1), 0)
    x_prev = jnp.where(r == 0, before, pltpu.roll(raw, 1, axis=0))
    x_next = jnp.where(r == tm - 1, after, pltpu.roll(raw, tm - 1, axis=0))
    y = cw_ref[0:1, :] * x_prev + cw_ref[1:2, :] * raw + cw_ref[2:3, :] * x_next + cb_ref[...]
    k_scale = jnp.where(_iota((1, y.shape[1]), 1) < ML_HEADS * ML_DH, 1.0, ML_DH ** -0.5)
    mqk_ref[0] = (y * _sigmoid(y) * k_scale).astype(BF16)


def _inproj(x, g, w_main, wg_t, gate_b, q_g, k_g, conv_w, conv_b, tm):
    B, S, D = x.shape
    per = tm // HALO_X
    last = S // HALO_X - 1
    bdq = (jnp.arange(512)[:, None] // ATT_DH == jnp.arange(512)[None, :] // ATT_DH).astype(BF16)
    bdk = bdq[:128, :128]
    R = min(INPROJ_ROWS, tm)
    s_from, s_to = jnp.arange(R)[:, None], jnp.arange(R)[None, :]
    same = s_from // BLK == s_to // BLK
    tri_f = (same & (s_from <= s_to)).astype(BF16)
    tri_b = (same & (s_from >= s_to)).astype(BF16)
    full = lambda *s: pl.BlockSpec(s, lambda b, i: (0,) * len(s))
    tok = lambda w: pl.BlockSpec((1, tm, w), lambda b, i: (b, i, 0))
    widths = (512, 128, 128, 1024, 512, 512)
    return pl.pallas_call(
        _inproj_kernel,
        grid=(B, S // tm),
        in_specs=[tok(D),
                  pl.BlockSpec((1, HALO_X, D), lambda b, i: (b, jnp.maximum(i * per - 1, 0), 0)),
                  pl.BlockSpec((1, HALO_X, D), lambda b, i: (b, jnp.minimum((i + 1) * per, last), 0)),
                  full(1, D), full(D, w_main.shape[1]), full(16, D), full(16, 1),
                  full(1, 512), full(1, 128), full(512, 512), full(128, 128), full(R, R), full(R, R), full(R, R),
                  full(3, 2 * ML_HEADS * ML_DH), full(1, 2 * ML_HEADS * ML_DH)],
        out_specs=[tok(w) for w in widths] + [pl.BlockSpec((1, 2 * ML_HEADS, tm), lambda b, i: (b, 0, i)), tok(LANES)],
        out_shape=[jax.ShapeDtypeStruct((B, S, w), BF16) for w in widths]
        + [jax.ShapeDtypeStruct((B, 2 * ML_HEADS, S), F32), jax.ShapeDtypeStruct((B, S, LANES), F32)],
        compiler_params=pltpu.CompilerParams(dimension_semantics=("arbitrary", "arbitrary"),
                                             vmem_limit_bytes=VMEM_LIMIT),
        name="inproj",
    )(x, x, x, g, w_main, wg_t, gate_b, q_g, k_g, bdq, bdk, tri_f, tri_b, same.astype(BF16), conv_w, conv_b)


ATT_GROUP = ATT_HEADS // ATT_KV
ATT_HEAD_ORDER = tuple(kv * ATT_GROUP + g for g in range(ATT_GROUP) for kv in range(ATT_KV))


ATT_QB = 8


def _attn_kernel(sink_ref, q_ref, kp_ref, kc_ref, kn_ref, vp_ref, vc_ref, vn_ref, bias_ref, o_ref):
    n = pl.program_id(1)
    nb = pl.num_programs(1)
    G, J = ATT_GROUP, ATT_QB
    kband = jnp.concatenate([kp_ref[0], kc_ref[0], kn_ref[0]], axis=0)
    vband = jnp.concatenate([vp_ref[0], vc_ref[0], vn_ref[0]], axis=0)
    first = _iota(kband.shape, 1) < ATT_DH
    kband, vband = kband.astype(F32), vband.astype(F32)
    keep_lanes = lambda a, mine: jnp.where(mine, a, 0.0).astype(BF16)
    si = _iota((1, (J + 2) * BLK), 1)
    edge = jnp.where(((si < BLK) & (n == 0)) | ((si >= (J + 1) * BLK) & (n == nb - 1)), NEG, 0.0)
    lane_first = _iota((1, LANES), 1) < ATT_DH
    k_kv = [keep_lanes(kband, first), keep_lanes(kband, ~first)]
    v_kv = [jnp.concatenate([keep_lanes(vband, m), m.astype(F32).astype(BF16)], axis=1) for m in (first, ~first)]
    def block(j):
        rows = slice(j * BLK, (j + 1) * BLK)
        band = slice(j * BLK, (j + 3) * BLK)
        q = q_ref[0, rows, :]
        qs = jnp.concatenate([q[:, g * LANES:(g + 1) * LANES] for g in range(G)], axis=0)
        tot = None
        sink_terms = []
        for kv in range(ATT_KV):
            s = _dot_nt(qs, k_kv[kv][band])
            ps, st = [], []
            for g in range(G):
                h = kv * G + g
                sink = sink_ref[h]
                sg = s[g * BLK:(g + 1) * BLK] + bias_ref[h] + edge[:, band]
                m = jnp.maximum(jnp.max(sg, axis=-1, keepdims=True), sink)
                ps.append(jnp.exp(sg - m).astype(BF16))
                st.append(jnp.exp(sink - m))
            yield
            part = _dot(jnp.concatenate(ps, axis=0), v_kv[kv][band])
            tot = part if tot is None else tot + part
            sink_terms.append(jnp.concatenate(st, axis=0))
            yield
        den = tot[:, LANES:] + jnp.where(lane_first, sink_terms[0], sink_terms[1])
        out = (tot[:, :LANES] / den).astype(BF16)
        o_ref[0, rows, :] = jnp.concatenate([out[g * BLK:(g + 1) * BLK] for g in range(G)], axis=1)

    _interleave(block(j) for j in range(J))


def _attention(sink, aq, ak, av):
    B, S, _ = aq.shape
    J = ATT_QB
    nb = S // (J * BLK)
    last_blk = S // BLK - 1
    kvw = ATT_KV * ATT_DH
    assert kvw == LANES and S % (J * BLK) == 0
    dist = jnp.abs(jnp.arange(BLK)[:, None] + BLK - jnp.arange(3 * BLK)[None, :]).astype(F32)
    slopes = jnp.exp2(-8.0 * jnp.arange(1, ATT_HEADS + 1, dtype=F32) / ATT_HEADS)
    bias = jnp.where(dist <= BLK, -slopes[:, None, None] * dist, NEG)
    prev = pl.BlockSpec((1, BLK, kvw), lambda b, n, *_: (b, jnp.maximum(J * n - 1, 0), 0))
    cur = pl.BlockSpec((1, J * BLK, kvw), lambda b, n, *_: (b, n, 0))
    nxt = pl.BlockSpec((1, BLK, kvw), lambda b, n, *_: (b, jnp.minimum(J * n + J, last_blk), 0))
    qspec = pl.BlockSpec((1, J * BLK, ATT_HEADS * ATT_DH), lambda b, n, *_: (b, n, 0))
    bspec = pl.BlockSpec((ATT_HEADS, BLK, 3 * BLK), lambda b, n, *_: (0, 0, 0))
    return pl.pallas_call(
        _attn_kernel,
        grid_spec=pltpu.PrefetchScalarGridSpec(
            num_scalar_prefetch=1, grid=(B, nb),
            in_specs=[qspec, prev, cur, nxt, prev, cur, nxt, bspec], out_specs=qspec),
        out_shape=jax.ShapeDtypeStruct(aq.shape, BF16),
        compiler_params=pltpu.CompilerParams(dimension_semantics=("arbitrary", "arbitrary"),
                                             vmem_limit_bytes=VMEM_LIMIT),
        name="win_attn",
    )(sink, aq, ak, ak, ak, av, av, av, bias)


ML_CPS = 8


def _mlstm_kernel(xf_ref, xb_ref, vf_ref, vb_ref, grf_ref, grb_ref, gcf_ref, gcb_ref, hf_ref, hb_ref, cn_ref, m_ref):
    c = pl.program_id(1)
    L = BLK
    width = ML_HEADS * ML_DH

    @pl.when(c == 0)
    def _():
        cn_ref[...] = jnp.zeros_like(cn_ref)
        m_ref[...] = jnp.zeros_like(m_ref)

    tt = _iota((L, L), 0)
    ss = _iota((L, L), 1)
    ones_v = jnp.ones((L, ML_DH), BF16)

    def chunk(d, sub, qk_ref, v_ref, grow_ref, gcol_ref, out_ref):
        H = ML_HEADS
        rows = slice(sub * L, (sub + 1) * L)
        keep = (ss <= tt) if d == 0 else (ss >= tt)
        g_row = grow_ref[0, H * d:H * (d + 1), rows]
        gc = gcol_ref[0, rows, :]
        kind = lambda i: gc[:, 2 * H * i + H * d:2 * H * i + H * (d + 1)]
        f, g_max, e_end, m_loc = kind(GC_F), kind(GC_GMAX), kind(GC_EEND), kind(GC_MLOC)[0:1, :]
        end = L - 1 if d == 0 else 0
        f_end = f[end:end + 1, :]
        m0 = m_ref[d, 0:1, 0:H]
        mm = jnp.maximum(m0, g_max)
        e_inter = jnp.exp(m0 - mm)
        floor = jnp.exp(-(f + mm))
        m_new = jnp.maximum(f_end + m0, m_loc)
        ca = jnp.exp(f_end + m0 - m_new)
        cb = jnp.exp(m_loc - m_new)
        m_ref[d, 0:1, 0:H] = m_new
        col = lambda a, h, w=ML_DH: jnp.broadcast_to(a[:, h:h + 1], (L, w))
        yield

        def head(h):
            u = H * d + h
            hs = slice(h * ML_DH, (h + 1) * ML_DH)
            qb = qk_ref[0, rows, hs]
            kb = qk_ref[0, rows, width + h * ML_DH: width + (h + 1) * ML_DH]
            v1 = jnp.concatenate([v_ref[0, rows, hs], ones_v], axis=1)
            cn0 = cn_ref[u]
            dec = jnp.where(keep, jnp.exp(g_row[h:h + 1, :] - col(mm, h)), 0.0)
            s_qk = (_dot_nt(qb, kb) * dec).astype(BF16)
            yield
            tot = col(e_inter, h, 2 * ML_DH) * _dot(qb, cn0.astype(BF16)) + _dot(s_qk, v1)
            out_ref[0, rows, hs] = tot[:, :ML_DH] / jnp.maximum(jnp.abs(tot[:, ML_DH:]), col(floor, h))
            ks = kb.astype(F32) * col(e_end, h)
            cn_ref[u] = ca[:, h:h + 1] * cn0 + cb[:, h:h + 1] * _dot(ks.T.astype(BF16), v1)

        yield from _round_robin(head(h) for h in range(H))

    def direction(d, *refs):
        for i in range(ML_CPS):
            yield from chunk(d, i if d == 0 else ML_CPS - 1 - i, *refs)

    _interleave([direction(0, xf_ref, vf_ref, grf_ref, gcf_ref, hf_ref),
                 direction(1, xb_ref, vb_ref, grb_ref, gcb_ref, hb_ref)])


def _mlstm(mqk, mv, g_rows, g_cols):
    B, S, _ = mqk.shape
    rows = ML_CPS * BLK
    ns = S // rows
    assert S % rows == 0
    width = ML_HEADS * ML_DH
    fwd = lambda c: c
    bwd = lambda c: ns - 1 - c
    qkspec = lambda ci: pl.BlockSpec((1, rows, 2 * width), lambda b, c: (b, ci(c), 0))
    vspec = lambda ci: pl.BlockSpec((1, rows, width), lambda b, c: (b, ci(c), 0))
    gspec = lambda ci: pl.BlockSpec((1, 2 * ML_HEADS, rows), lambda b, c: (b, 0, ci(c)))
    cspec = lambda ci: pl.BlockSpec((1, rows, LANES), lambda b, c: (b, ci(c), 0))
    units = 2 * ML_HEADS
    return pl.pallas_call(
        _mlstm_kernel,
        grid=(B, ns),
        in_specs=[qkspec(fwd), qkspec(bwd), vspec(fwd), vspec(bwd), gspec(fwd), gspec(bwd), cspec(fwd), cspec(bwd)],
        out_specs=[vspec(fwd), vspec(bwd)],
        out_shape=[jax.ShapeDtypeStruct((B, S, width), F32)] * 2,
        scratch_shapes=[pltpu.VMEM((units, ML_DH, 2 * ML_DH), F32), pltpu.VMEM((2, 8, LANES), F32)],
        compiler_params=pltpu.CompilerParams(dimension_semantics=("arbitrary", "arbitrary"),
                                             vmem_limit_bytes=VMEM_LIMIT),
        name="mlstm",
    )(mqk, mqk, mv, mv, g_rows, g_rows, g_cols, g_cols)


MIX_ROWS = 256


def _mix_kernel(x_ref, att_ref, hf_ref, hb_ref, mo_ref, og_ref, wo_ref,
                gm_ref, wq_ref, mqg_ref, k_ref, v_ref, wmo_ref,
                gf_ref, wr_ref, brt_ref,
                slab_ref, afft_ref):
    width = ML_HEADS * ML_DH
    k = k_ref[0]
    v = v_ref[0]
    wr = wr_ref[...]
    w_hi = wr.astype(BF16)
    w_lo = (wr - w_hi.astype(F32)).astype(BF16)
    def chain(r0):
        rows = slice(r0, r0 + MIX_ROWS)
        x = x_ref[0, rows, :]
        ml = hf_ref[0, rows, :] + hb_ref[0, rows, :]
        mo = mo_ref[0, rows, :].astype(F32)
        parts = []
        for h in range(ML_HEADS):
            sl = slice(h * ML_DH, (h + 1) * ML_DH)
            parts.append((_sigmoid(mo[:, sl]) * _rms(ml[:, sl], og_ref[:, sl])).astype(BF16))
        ml_out = jnp.concatenate(parts, axis=1)
        yield
        y1 = x + _dot(att_ref[0, rows, :], wo_ref[0:width, :]) + _dot(ml_out, wo_ref[width:2 * width, :])
        yield
        h2 = _rms(y1, gm_ref[...]).astype(BF16)
        qm = _dot(h2, wq_ref[...])
        yield
        outs = [None] * MEM_HEADS

        def mem_head(h):
            sl = slice(h * MEM_DH, (h + 1) * MEM_DH)
            qh = (_rms(qm[:, sl], mqg_ref[...]) * (MEM_DH ** -0.5)).astype(BF16)
            s = _dot_nt(qh, k[:, sl])
            yield
            p = jnp.exp(s - jnp.max(s, axis=-1, keepdims=True))
            yield
            o = _dot(p.astype(BF16), v[:, sl]) / jnp.sum(p, axis=-1, keepdims=True)
            outs[h] = o.astype(BF16)

        yield from _round_robin(mem_head(h) for h in range(MEM_HEADS))
        y2 = y1 + _dot(jnp.concatenate(outs, axis=1), wmo_ref[...])
        yield
        h3 = _rms(y2, gf_ref[...])
        h_hi = h3.astype(BF16)
        h_lo = (h3 - h_hi.astype(F32)).astype(BF16)
        logits = _dot(h_hi, w_hi) + _dot(h_lo, w_hi) + _dot(h_hi, w_lo)
        yield
        logits_t = jnp.concatenate([logits[q0:q0 + LANES, :].T[:N_EXPERTS, :] for q0 in range(0, MIX_ROWS, LANES)],
                                   axis=1) + brt_ref[...]
        pt = jnp.exp(logits_t - jnp.max(logits_t, axis=0, keepdims=True))
        afft_ref[0, :, rows] = pt / jnp.sum(pt, axis=0, keepdims=True)
        chunks = y2.shape[1] // LANES
        for c in range(chunks):
            tiles = pl.ds(r0 * chunks + c, MIX_ROWS, stride=chunks)
            slab_ref[SLAB_ACC, tiles, :] = y2[:, c * LANES:(c + 1) * LANES]
            slab_ref[SLAB_X, tiles, :] = h3[:, c * LANES:(c + 1) * LANES]

    _interleave(chain(r0) for r0 in range(0, x_ref.shape[1], MIX_ROWS))


def _mix(x, att, hf, hb, mo, out_g, w_out, g_mem, w_q, mq_g, mem_k, mem_v, w_mo, g_ffn, w_r, b_r, tm):
    B, S, D = x.shape
    M = mem_k.shape[1]
    width = ML_HEADS * ML_DH
    mw = MEM_HEADS * MEM_DH
    full = lambda *s: pl.BlockSpec(s, lambda b, i: (0,) * len(s))
    tok = lambda w: pl.BlockSpec((1, tm, w), lambda b, i: (b, i, 0))
    chunks = D // LANES
    tiled = pl.BlockSpec((2, tm * chunks, LANES), lambda b, i: (0, b * (S // tm) + i, 0))
    memspec = pl.BlockSpec((1, M, mw), lambda b, i: (b, 0, 0))
    tiled_shape = jax.ShapeDtypeStruct((2, B * S * chunks, LANES), F32)
    return pl.pallas_call(
        _mix_kernel,
        grid=(B, S // tm),
        in_specs=[tok(D), tok(width), tok(width), tok(width), tok(width), full(1, width), full(2 * width, D),
                  full(1, D), full(D, mw), full(1, MEM_DH), memspec, memspec, full(mw, D),
                  full(1, D), full(D, LANES), full(N_EXPERTS, 1)],
        out_specs=[tiled, pl.BlockSpec((1, N_EXPERTS, tm), lambda b, i: (b, 0, i))],
        out_shape=[tiled_shape, jax.ShapeDtypeStruct((B, N_EXPERTS, S), F32)],
        compiler_params=pltpu.CompilerParams(dimension_semantics=("arbitrary", "arbitrary"),
                                             vmem_limit_bytes=VMEM_LIMIT),
        name="mix_mem_router",
    )(x, att, hf, hb, mo, out_g, w_out, g_mem, w_q, mq_g, mem_k, mem_v, w_mo, g_ffn,
      jnp.pad(w_r, ((0, 0), (0, LANES - N_EXPERTS))), b_r.T)


TOPC_GROUP = 8


def _topc_kernel(aff2_ref, aff3_ref, idx_ref, gval_ref, *, cap, seq, row_pitch):
    a2 = aff2_ref[0]
    bits2 = pltpu.bitcast(a2, I32)
    capf = float(cap)

    def bisect(i, lo):
        cand = lo | jnp.left_shift(jnp.int32(1), 30 - i)
        cnt = jnp.sum((bits2 >= cand).astype(F32), axis=1, keepdims=True)
        return jnp.where(cnt >= capf, cand, lo)

    thr_all = lax.fori_loop(0, 31, bisect, jnp.zeros((N_EXPERTS, 1), I32))
    need_all = capf - jnp.sum((bits2 > thr_all).astype(F32), axis=1, keepdims=True)

    T = aff3_ref.shape[2]
    tri_u = (_iota((LANES, LANES), 0) <= _iota((LANES, LANES), 1)).astype(BF16)
    tri_l = (_iota((LANES, LANES), 1) <= _iota((LANES, LANES), 0)).astype(BF16)
    ones8 = jnp.ones((8, LANES), BF16)
    before = _iota((T, T), 1) < _iota((T, T), 0)
    kcol = _iota((T, 1), 0).astype(F32)
    j = _iota((1, cap), 1).astype(F32)
    assert T < LANES
    tile_pad = jnp.zeros((LANES - T, LANES), F32)
    lane_pos = _iota((LANES, cap), 0).astype(F32)

    def tile_starts(maskb):
        tot_row = _dot_nt(ones8, maskb)[0:1, :]
        return jnp.sum(jnp.where(before, tot_row, 0.0), axis=1, keepdims=True)

    def select(e):
        bits = pltpu.bitcast(aff3_ref[0, e], I32)
        thr = thr_all[e:e + 1, :]
        need = need_all[e:e + 1, :]
        gt = bits > thr
        eq = bits == thr
        eqb = eq.astype(BF16)
        eq_rank = _dot(eqb, tri_u) + tile_starts(eqb) - eq.astype(F32)
        return (gt | (eq & (eq_rank < need))).astype(BF16)

    def locate(selb):
        cs = _dot(selb, tri_u)
        start = tile_starts(selb)
        end = start + cs[:, LANES - 1:LANES]
        onehot = (start <= j) & (j < end)
        ohf = onehot.astype(F32)
        tile_of = jnp.sum(ohf * kcol, axis=0, keepdims=True)
        j_loc = j - jnp.sum(ohf * start, axis=0, keepdims=True)
        cs_t = _dot_nt(tri_l, selb)
        return onehot.astype(BF16), tile_of, j_loc, cs_t

    def emit(e, ohb, tile_of, j_loc, cs_t):
        r_t = _dot(cs_t.astype(BF16), ohb)
        local = jnp.sum((r_t <= j_loc).astype(F32), axis=0, keepdims=True)
        idx_ref[0, e:e + 1, :] = ((tile_of * LANES + local).astype(I32) + pl.program_id(0) * seq) * row_pitch
        a_t = jnp.concatenate([aff3_ref[0, e], tile_pad], axis=0).T[:, :T]
        a_hi = a_t.astype(BF16)
        a_lo = (a_t - a_hi.astype(F32)).astype(BF16)
        a_tile = _dot(a_hi, ohb) + _dot(a_lo, ohb)
        gval_ref[0, e:e + 1, :] = jnp.sum(jnp.where(lane_pos == local, a_tile, 0.0), axis=0, keepdims=True)

    for e0 in range(0, N_EXPERTS, TOPC_GROUP):
        group = range(e0, e0 + TOPC_GROUP)
        sels = [select(e) for e in group]
        locs = [locate(selb) for selb in sels]
        for e, loc in zip(group, locs):
            emit(e, *loc)


def _topc(aff_t, cap, row_pitch):
    B, E, S = aff_t.shape
    T = S // LANES
    aff3 = aff_t.reshape(B, E, T, LANES)
    return pl.pallas_call(
        functools.partial(_topc_kernel, cap=cap, seq=S, row_pitch=row_pitch),
        grid=(B,),
        in_specs=[pl.BlockSpec((1, E, S), lambda b: (b, 0, 0)), pl.BlockSpec((1, E, T, LANES), lambda b: (b, 0, 0, 0))],
        out_specs=[pl.BlockSpec((1, E, cap), lambda b: (b, 0, 0))] * 2,
        out_shape=[jax.ShapeDtypeStruct((B, E, cap), I32), jax.ShapeDtypeStruct((B, E, cap), F32)],
        compiler_params=pltpu.CompilerParams(dimension_semantics=("arbitrary",), vmem_limit_bytes=VMEM_LIMIT),
        name="topc",
    )(aff_t, aff3)


FFN_ROWS = 256
FFN_COLS = 256
AB_RING = 3


SLAB_X, SLAB_ACC = 0, 1


def _ffn_kernel(idxp_ref, idxn_ref, gv_ref, slab_in_hbm, wg32_ref, wu32_ref, wd32_ref, slab_hbm,
                buf_ref, wg_ref, wu_ref, wd_ref, sems, *, nb, cap, d_model, ff):
    del slab_in_hbm
    s = pl.program_id(0)
    last = pl.num_programs(0) - 1

    @pl.when(s % nb == 0)
    def _():
        for src, dst in ((wg32_ref, wg_ref), (wu32_ref, wu_ref), (wd32_ref, wd_ref)):
            for r0 in range(0, src.shape[1], FFN_ROWS):
                dst[0, r0:r0 + FFN_ROWS, :] = src[0, r0:r0 + FFN_ROWS, :].astype(BF16)

    a_cur = s % AB_RING
    a_nxt = (s + 1) % AB_RING
    a_prv = (s + 2) % AB_RING
    chunks = d_model // LANES

    def rows_of(start):
        return pl.ds(pl.multiple_of(start, chunks), chunks)

    def gather(r, j, sl):
        return pltpu.make_async_copy(slab_hbm.at[:, rows_of(r)], buf_ref.at[sl, :, rows_of(j * chunks)], sems.at[0])

    def scatter(r, j, sl):
        return pltpu.make_async_copy(buf_ref.at[sl, SLAB_ACC, rows_of(j * chunks)], slab_hbm.at[SLAB_ACC, rows_of(r)],
                                     sems.at[1])

    def wait_gather(sl):
        pltpu.make_async_copy(slab_hbm.at[:, pl.ds(0, cap * chunks)], buf_ref.at[sl], sems.at[0]).wait()

    def wait_scatter(sl):
        pltpu.make_async_copy(buf_ref.at[sl, SLAB_ACC], slab_hbm.at[SLAB_ACC, pl.ds(0, cap * chunks)],
                              sems.at[1]).wait()

    @pl.when(s == 0)
    def _():
        @pl.loop(0, cap)
        def _(j):
            r = idxp_ref[0, 0, j]
            gather(r, j, 0).start()
            gather(r, j, AB_RING - 1).start()
        wait_gather(AB_RING - 1)

    wait_gather(a_cur)

    halves = FFN_COLS // LANES
    nblk = cap // FFN_ROWS
    n_ct = ff // FFN_COLS
    assert d_model // FFN_COLS == n_ct

    n_issues = 2 * nblk * n_ct
    gather_rows = -(-cap // n_issues)
    scatter_rows = -(-cap // n_issues)
    next_gather, next_scatter, n_issue = [0], [0], [0]

    def issue_rows():
        n_issue[0] += 1
        lo, hi = next_gather[0], min(next_gather[0] + gather_rows, cap)
        next_gather[0] = hi
        for j in range(lo, hi):
            gather(idxn_ref[0, 0, j], j, a_nxt).start(priority=0)
        lo, hi = next_scatter[0], min(next_scatter[0] + scatter_rows, cap)
        next_scatter[0] = hi
        for j in range(lo, hi):
            scatter(idxp_ref[0, 0, j], j, a_prv).start(priority=1)

    def chunk_rows(rb, c):
        return pl.ds(rb * FFN_ROWS * chunks + c, FFN_ROWS, stride=chunks)

    def gate_rows(rb):
        g_row = gv_ref[0, rb]
        return jnp.concatenate(
            [jnp.broadcast_to(g_row[:, i * LANES:(i + 1) * LANES], (LANES, LANES)).T for i in range(FFN_ROWS // LANES)],
            axis=0)

    hid_prev = None
    for stage in range(nblk + 1):
        up, dn = stage < nblk, stage >= 1
        if up:
            xb = jnp.concatenate([buf_ref[a_cur, SLAB_X, chunk_rows(stage, c), :] for c in range(chunks)],
                                 axis=1).astype(BF16)
        if dn:
            gval = gate_rows(stage - 1)
        hid = []
        for ct in range(n_ct):
            cs = slice(ct * FFN_COLS, (ct + 1) * FFN_COLS)
            if up:
                hg = _dot(xb, wg_ref[0, :, cs])
                issue_rows()
                hu = _dot(xb, wu_ref[0, :, cs])
                hid.append((hg * _sigmoid(hg) * hu).astype(BF16))
            if dn:
                ye = _dot(hid_prev, wd_ref[0, :, cs])
                for i in range(halves):
                    buf_ref[a_cur, SLAB_ACC, chunk_rows(stage - 1, ct * halves + i), :] += (
                        ye[:, i * LANES:(i + 1) * LANES] * gval)
            if up:
                issue_rows()
        hid_prev = jnp.concatenate(hid, axis=1) if up else None
    assert next_gather[0] == cap and next_scatter[0] == cap and n_issue[0] == n_issues

    wait_scatter(a_prv)

    @pl.when(s == last)
    def _():
        wait_gather(a_nxt)

        @pl.loop(0, cap)
        def _(j):
            scatter(idxn_ref[0, 0, j], j, a_cur).start()
        wait_scatter(a_cur)


def _expert_ffn(idx, gval, slab, wg, wu, wd):
    B, E, cap = idx.shape
    d_model, ff = wg.shape[1], wg.shape[2]
    chunks = d_model // LANES
    assert B >= 3 and cap % FFN_ROWS == 0 and ff % FFN_COLS == 0 and d_model % FFN_COLS == 0
    ns = E * B
    nblk = cap // FFN_ROWS
    idx3 = idx.reshape(B * E, 1, cap)
    gv4 = gval.reshape(B * E, nblk, 1, FFN_ROWS)
    any_spec = pl.BlockSpec(memory_space=pl.ANY)
    wspec = lambda r, c: pl.BlockSpec((1, r, c), lambda s: (s // B, 0, 0))
    blk = lambda s: (s % B) * E + s // B
    ispec = lambda f: pl.BlockSpec((1, 1, cap), lambda s: (blk(f(s)), 0, 0), memory_space=pltpu.SMEM)
    kern = functools.partial(_ffn_kernel, nb=B, cap=cap, d_model=d_model, ff=ff)
    return pl.pallas_call(
        kern,
        grid=(ns,),
        in_specs=[ispec(lambda s: jnp.maximum(s - 1, 0)), ispec(lambda s: jnp.minimum(s + 1, ns - 1)),
                  pl.BlockSpec((1, nblk, 1, FFN_ROWS), lambda s: (blk(s), 0, 0, 0)),
                  any_spec, wspec(d_model, ff), wspec(d_model, ff), wspec(ff, d_model)],
        out_specs=any_spec,
        out_shape=jax.ShapeDtypeStruct(slab.shape, F32),
        scratch_shapes=[pltpu.VMEM((AB_RING, 2, cap * chunks, LANES), F32),
                        pltpu.VMEM((1, d_model, ff), BF16), pltpu.VMEM((1, d_model, ff), BF16),
                        pltpu.VMEM((1, ff, d_model), BF16), pltpu.SemaphoreType.DMA((2,))],
        input_output_aliases={3: 0},
        compiler_params=pltpu.CompilerParams(dimension_semantics=("arbitrary",), vmem_limit_bytes=FFN_VMEM_LIMIT),
        name="expert_ffn",
    )(idx3, idx3, gv4, slab, wg, wu, wd)


def _untile_kernel(a_ref, o_ref):
    tm, chunks = o_ref.shape[0], o_ref.shape[1] // LANES
    for c in range(chunks):
        o_ref[:, c * LANES:(c + 1) * LANES] = a_ref[0, pl.ds(c, tm, stride=chunks), :]


def _untile(slab, plane, chunks, tm):
    N = slab.shape[1] // chunks
    return pl.pallas_call(
        _untile_kernel,
        grid=(N // tm,),
        in_specs=[pl.BlockSpec((1, tm * chunks, LANES), lambda i: (plane, i, 0))],
        out_specs=pl.BlockSpec((tm, chunks * LANES), lambda i: (i, 0)),
        out_shape=jax.ShapeDtypeStruct((N, chunks * LANES), F32),
        compiler_params=pltpu.CompilerParams(dimension_semantics=("arbitrary",), vmem_limit_bytes=VMEM_LIMIT),
        name="untile",
    )(slab)


def _layer(x, mem, norm_mix_g, w_in, att_q_norm_g, att_k_norm_g, att_sink, ml_conv_w, ml_conv_b,
           ml_gate_b, ml_out_norm_g, w_out, norm_mem_g, mem_kv_norm_g, w_mem_q, w_mem_kv,
           mem_q_norm_g, mem_k_norm_g, w_mem_o, norm_ffn_g, w_router, b_router,
           w_exp_gate, w_exp_up, w_exp_down):
    B, S, D = x.shape
    row = lambda v: v.reshape(1, -1).astype(F32)
    tm_in = min(1024, S)
    tm_mix = min(2 * MIX_ROWS, S)
    cap = CAPACITY_FACTOR * S // N_EXPERTS

    mem_k, mem_v = _mem_kv(mem, row(mem_kv_norm_g), w_mem_kv.astype(BF16), row(mem_k_norm_g))

    order = jnp.array(ATT_HEAD_ORDER)
    w_aq = w_in[:, _AQ[0]:_AQ[1]].reshape(D, ATT_HEADS, ATT_DH)[:, order].reshape(D, -1)
    w_main = jnp.concatenate([w_aq, w_in[:, _AQ[1]:_GATES[0]]], axis=1).astype(BF16)
    wg_t = w_in[:, _GATES[0]:_GATES[1]].T.astype(BF16)
    q_g = jnp.tile(row(att_q_norm_g), (1, ATT_HEADS)) * (ATT_DH ** -0.5)
    k_g = jnp.tile(row(att_k_norm_g), (1, ATT_KV))
    aq, ak, av, mqk, mv, mo, g_rows, g_cols = _inproj(x, row(norm_mix_g), w_main, wg_t, ml_gate_b.reshape(-1, 1).astype(F32),
                                               q_g, k_g, ml_conv_w.astype(F32), row(ml_conv_b), tm_in)

    att = _attention(att_sink.astype(F32), aq, ak, av)
    hf, hb = _mlstm(mqk, mv, g_rows, g_cols)

    att_w = ATT_HEADS * ATT_DH
    w_out_att = w_out[:att_w].reshape(ATT_HEADS, ATT_DH, D)[order].reshape(att_w, D)
    w_out_p = jnp.concatenate([w_out_att, w_out[att_w:]], axis=0).astype(BF16)
    slab, aff_t = _mix(x, att, hf, hb, mo, row(ml_out_norm_g), w_out_p, row(norm_mem_g),
                             w_mem_q.astype(BF16), row(mem_q_norm_g), mem_k, mem_v, w_mem_o.astype(BF16),
                             row(norm_ffn_g), w_router.astype(F32), row(b_router), tm_mix)

    idx, gval = _topc(aff_t, cap, D // LANES)
    slab = _expert_ffn(idx, gval, slab, w_exp_gate.astype(F32), w_exp_up.astype(F32), w_exp_down.astype(F32))
    return _untile(slab, SLAB_ACC, D // LANES, tm_in).reshape(B, S, D)


def kernel(x, mem, norm_mix_g, w_in, att_q_norm_g, att_k_norm_g, att_sink, ml_conv_w, ml_conv_b, ml_gate_b,
           ml_out_norm_g, w_out, norm_mem_g, mem_kv_norm_g, w_mem_q, w_mem_kv, mem_q_norm_g, mem_k_norm_g,
           w_mem_o, norm_ffn_g, w_router, b_router, w_exp_gate, w_exp_up, w_exp_down):
    params = (norm_mix_g, w_in, att_q_norm_g, att_k_norm_g, att_sink, ml_conv_w, ml_conv_b, ml_gate_b,
              ml_out_norm_g, w_out, norm_mem_g, mem_kv_norm_g, w_mem_q, w_mem_kv, mem_q_norm_g, mem_k_norm_g,
              w_mem_o, norm_ffn_g, w_router, b_router, w_exp_gate, w_exp_up, w_exp_down)
    depth = norm_mix_g.shape[0]
    for l in range(depth):
        x = _layer(x, mem, *[p[l] for p in params])
    return x
```
